```python
import math
import jax, jax.numpy as jnp
from jax import lax
import numpy as np

D_MODEL = 2048
BATCH = 2
SEQ = 16384
DEPTH = 2

GRID_W = 64
CTX_LEN = 256
N_MIXERS = 2
N_ATTN_LAYERS = (DEPTH + 1) // 2
N_SSM_LAYERS = DEPTH // 2
N_MOD = 9
N_NORMS = 6
EPS = 1e-6
NEG_INF = -1e30
N_HEADS = 16
N_KV_HEADS = 4
HEAD_DIM = D_MODEL // N_HEADS
GQA_GROUP = N_HEADS // N_KV_HEADS
WINDOW = 128
BLOCK = 128
QKV_DIM = (N_HEADS + 2 * N_KV_HEADS) * HEAD_DIM
ROPE_BASE = 10000.0
ROPE_AXIS_DIM = HEAD_DIM // 2
SSM_GROUP_CH = 16
SSM_GROUPS = D_MODEL // SSM_GROUP_CH
SSM_STATE = 64
SSM_CHUNK = 128
DT_MIN = 1e-3
DT_MAX = 1e-1
D_FF = 5632

kernel_name = "hybrid_swa_s5_macaron_dit"


def rmsnorm(t, g):
    tf = t.astype(jnp.float32)
    tf = tf * lax.rsqrt(jnp.mean(tf * tf, axis=-1, keepdims=True) + EPS)
    return (tf * g.astype(jnp.float32)).astype(t.dtype)


def pre_sublayer(t, g, shift, scale):
    return rmsnorm(t, g) * (1 + scale) + shift


def post_sublayer(t, out, g, gate, weight):
    return t + weight * gate * rmsnorm(out, g)


def swiglu(h, w_in, w_out):
    g, u = jnp.split(h @ w_in, 2, axis=-1)
    return (jax.nn.silu(g) * u) @ w_out


def axial_rope_tables(rows):
    row = jnp.repeat(jnp.arange(rows, dtype=jnp.float32), GRID_W)
    col = jnp.tile(jnp.arange(GRID_W, dtype=jnp.float32), rows)
    inv_freq = ROPE_BASE ** (-jnp.arange(0, ROPE_AXIS_DIM, 2, dtype=jnp.float32) / ROPE_AXIS_DIM)
    ang = jnp.concatenate([row[:, None] * inv_freq, col[:, None] * inv_freq], axis=-1)
    return jnp.cos(ang), jnp.sin(ang)


def apply_axial_rope(t, cos, sin):
    tf = t.astype(jnp.float32)
    half = ROPE_AXIS_DIM // 2
    outs = []
    for a in range(2):
        seg = tf[..., a * ROPE_AXIS_DIM:(a + 1) * ROPE_AXIS_DIM]
        cs = cos[:, None, a * half:(a + 1) * half]
        sn = sin[:, None, a * half:(a + 1) * half]
        x1, x2 = seg[..., :half], seg[..., half:]
        outs += [x1 * cs - x2 * sn, x2 * cs + x1 * sn]
    return jnp.concatenate(outs, axis=-1).astype(t.dtype)


def softmax_with_sink(logits, sink_logit):
    full = jnp.concatenate([logits, jnp.broadcast_to(sink_logit, logits.shape[:-1] + (1,))], axis=-1)
    return jax.nn.softmax(full, axis=-1)[..., :-1]


def band_blocks(t, n_blk):
    b = t.shape[0]
    tb = t.reshape(b, n_blk, BLOCK, t.shape[2], t.shape[3])
    tp = jnp.pad(tb, ((0, 0), (1, 1), (0, 0), (0, 0), (0, 0)))
    return jnp.concatenate([tp[:, :-2], tp[:, 1:-1], tp[:, 2:]], axis=2)


def band_mask(n_blk, seq_len):
    i = jnp.arange(BLOCK)[:, None]
    j = jnp.arange(3 * BLOCK)[None, :]
    rel = j - i
    kpos = (jnp.arange(n_blk)[:, None, None] - 1) * BLOCK + j[None]
    return (rel >= BLOCK - WINDOW) & (rel <= BLOCK + WINDOW) & (kpos >= 0) & (kpos < seq_len)


def windowed_gqa(h_lat, h_ctx, w_in, w_out, sink, cos, sin, need_ctx_out):
    b, seq_len, _ = h_lat.shape
    c_len = h_ctx.shape[1]
    n_blk = seq_len // BLOCK
    scale = HEAD_DIM ** -0.5
    splits = [N_HEADS * HEAD_DIM, (N_HEADS + N_KV_HEADS) * HEAD_DIM]
    q, k, v = jnp.split(h_lat @ w_in, splits, axis=-1)
    q = apply_axial_rope(q.reshape(b, seq_len, N_HEADS, HEAD_DIM), cos, sin) * scale
    k = apply_axial_rope(k.reshape(b, seq_len, N_KV_HEADS, HEAD_DIM), cos, sin)
    v = v.reshape(b, seq_len, N_KV_HEADS, HEAD_DIM)
    qc, kc, vc = jnp.split(h_ctx @ w_in, splits, axis=-1)
    kc = kc.reshape(b, c_len, N_KV_HEADS, HEAD_DIM)
    vc = vc.reshape(b, c_len, N_KV_HEADS, HEAD_DIM)
    sink_logit = sink.astype(jnp.float32).reshape(N_KV_HEADS, GQA_GROUP, 1, 1)
    qb = q.reshape(b, n_blk, BLOCK, N_KV_HEADS, GQA_GROUP, HEAD_DIM)
    kb, vb = band_blocks(k, n_blk), band_blocks(v, n_blk)
    s_band = jnp.einsum('bnqhgd,bnkhd->bnhgqk', qb, kb).astype(jnp.float32)
    s_band = jnp.where(band_mask(n_blk, seq_len)[None, :, None, None], s_band, NEG_INF)
    s_ctx = jnp.einsum('bnqhgd,bchd->bnhgqc', qb, kc).astype(jnp.float32)
    p = softmax_with_sink(jnp.concatenate([s_band, s_ctx], axis=-1), sink_logit).astype(v.dtype)
    nb = 3 * BLOCK
    o = (jnp.einsum('bnhgqk,bnkhd->bnqhgd', p[..., :nb], vb)
         + jnp.einsum('bnhgqc,bchd->bnqhgd', p[..., nb:], vc))
    out_lat = o.reshape(b, seq_len, D_MODEL) @ w_out
    out_ctx = None
    if need_ctx_out:
        qc = qc.reshape(b, c_len, N_KV_HEADS, GQA_GROUP, HEAD_DIM) * scale
        s = jnp.einsum('bqhgd,bkhd->bhgqk', qc, kc).astype(jnp.float32)
        pc = softmax_with_sink(s, sink_logit).astype(vc.dtype)
        out_ctx = jnp.einsum('bhgqk,bkhd->bqhgd', pc, vc).reshape(b, c_len, D_MODEL) @ w_out
    return out_lat, out_ctx


def zoh_discretise(a_re, a_im, log_dt, b_re, b_im):
    a_re, a_im = a_re.astype(jnp.float32), a_im.astype(jnp.float32)
    b_re, b_im = b_re.astype(jnp.float32), b_im.astype(jnp.float32)
    dt = jnp.exp(log_dt.astype(jnp.float32))[:, None]
    mag = jnp.exp(a_re * dt)
    ab_re, ab_im = mag * jnp.cos(a_im * dt), mag * jnp.sin(a_im * dt)
    den = a_re * a_re + a_im * a_im
    x_re, x_im = ab_re - 1.0, ab_im
    f_re = ((x_re * a_re + x_im * a_im) / den)[..., None]
    f_im = ((x_im * a_re - x_re * a_im) / den)[..., None]
    bb_re = f_re * b_re - f_im * b_im
    bb_im = f_re * b_im + f_im * b_re
    return ab_re, ab_im, bb_re, bb_im


def _ssm_combine(e_i, e_j):
    ai_re, ai_im, bi_re, bi_im = e_i
    aj_re, aj_im, bj_re, bj_im = e_j
    return (aj_re * ai_re - aj_im * ai_im,
            aj_re * ai_im + aj_im * ai_re,
            aj_re * bi_re - aj_im * bi_im + bj_re,
            aj_re * bi_im + aj_im * bi_re + bj_im)


def s5_scan(u, h0_re, h0_im, disc, c_re, c_im):
    ab_re, ab_im, bb_re, bb_im = disc
    t_len = u.shape[0]
    uc = u.reshape((t_len // SSM_CHUNK, SSM_CHUNK) + u.shape[1:])
    a_re = jnp.broadcast_to(ab_re, (SSM_CHUNK, 1) + ab_re.shape)
    a_im = jnp.broadcast_to(ab_im, (SSM_CHUNK, 1) + ab_im.shape)

    def step(carry, u_blk):
        h_re, h_im = carry
        bu_re = jnp.einsum('tbgc,gpc->tbgp', u_blk, bb_re)
        bu_im = jnp.einsum('tbgc,gpc->tbgp', u_blk, bb_im)
        bu_re = bu_re.at[0].add(ab_re * h_re - ab_im * h_im)
        bu_im = bu_im.at[0].add(ab_re * h_im + ab_im * h_re)
        _, _, s_re, s_im = lax.associative_scan(_ssm_combine, (a_re, a_im, bu_re, bu_im), axis=0)
        y = jnp.einsum('tbgp,gcp->tbgc', s_re, c_re) - jnp.einsum('tbgp,gcp->tbgc', s_im, c_im)
        return (s_re[-1], s_im[-1]), y

    (h_re, h_im), y = lax.scan(step, (h0_re, h0_im), uc)
    return y.reshape(u.shape), h_re, h_im


def s5_mixer(h_lat, h_ctx, w_in, a_re, a_im, log_dt, b_re, b_im, c_re, c_im, d_skip, w_glu, need_ctx_out):
    b = h_lat.shape[0]

    def to_groups(h):
        u = (h @ w_in).astype(jnp.float32)
        t_len = u.shape[1]
        return u, jnp.transpose(u.reshape(b, t_len, SSM_GROUPS, SSM_GROUP_CH), (1, 0, 2, 3))

    u_lat, ut_lat = to_groups(h_lat)
    u_ctx, ut_ctx = to_groups(h_ctx)
    zero = jnp.zeros((b, SSM_GROUPS, SSM_STATE), jnp.float32)
    ys_lat, ys_ctx = [], []
    for dr in range(2):
        disc = zoh_discretise(a_re[dr], a_im[dr], log_dt[dr], b_re[dr], b_im[dr])
        cr, ci = c_re[dr].astype(jnp.float32), c_im[dr].astype(jnp.float32)
        uc = ut_ctx[::-1] if dr == 1 else ut_ctx
        ul = ut_lat[::-1] if dr == 1 else ut_lat
        yc, hr, hi = s5_scan(uc, zero, zero, disc, cr, ci)
        yl, _, _ = s5_scan(ul, hr, hi, disc, cr, ci)
        ys_ctx.append(yc[::-1] if dr == 1 else yc)
        ys_lat.append(yl[::-1] if dr == 1 else yl)

    def readout(y_t, u):
        t_len = u.shape[1]
        y = jnp.transpose(y_t, (1, 0, 2, 3)).reshape(b, t_len, D_MODEL) + d_skip.astype(jnp.float32) * u
        g = jax.nn.gelu(y).astype(h_lat.dtype)
        val, gate = jnp.split(g @ w_glu, 2, axis=-1)
        return val * jax.nn.sigmoid(gate)

    out_lat = readout(ys_lat[0] + ys_lat[1], u_lat)
    out_ctx = readout(ys_ctx[0] + ys_ctx[1], u_ctx) if need_ctx_out else None
    return out_lat, out_ctx


def setup_inputs(seed: int = 0) -> dict:
    key = jax.random.key(seed)
    ks = jax.random.split(key, 24)
    f32 = jnp.float32
    d, f = D_MODEL, D_FF
    na, ns = N_ATTN_LAYERS, N_SSM_LAYERS
    g, p, gc = SSM_GROUPS, SSM_STATE, SSM_GROUP_CH
    nrm = lambda k, shape, s: jax.random.normal(k, shape, f32) * s
    a_im_base = jnp.pi * jnp.arange(p, dtype=f32)
    return {
        "x": nrm(ks[0], (BATCH, SEQ, d), 1.0),
        "c": nrm(ks[1], (BATCH, d), 1.0),
        "ctx": nrm(ks[2], (BATCH, CTX_LEN, d), 1.0),
        "c_ctx": nrm(ks[3], (d,), 1.0),
        "ada_w": nrm(ks[4], (DEPTH, d, N_MOD * d), 0.5 * d ** -0.5),
        "ada_b": nrm(ks[5], (DEPTH, N_MOD * d), 0.02),
        "norm_g": 1.0 + nrm(ks[6], (DEPTH, N_NORMS, d), 0.02),
        "ffn_w_in": nrm(ks[7], (DEPTH, 2, d, 2 * f), d ** -0.5),
        "ffn_w_out": nrm(ks[8], (DEPTH, 2, f, d), f ** -0.5),
        "attn_w_in": nrm(ks[9], (na, d, QKV_DIM), d ** -0.5),
        "attn_w_out": nrm(ks[10], (na, d, d), d ** -0.5),
        "attn_sink": nrm(ks[11], (na, N_HEADS), 0.5),
        "ssm_w_in": nrm(ks[12], (ns, d, d), d ** -0.5),
        "ssm_a_re": -0.5 + nrm(ks[13], (ns, 2, g, p), 0.01),
        "ssm_a_im": a_im_base + nrm(ks[14], (ns, 2, g, p), 0.01),
        "ssm_log_dt": jax.random.uniform(ks[15], (ns, 2, g), f32, math.log(DT_MIN), math.log(DT_MAX)),
        "ssm_b_re": nrm(ks[16], (ns, 2, g, p, gc), (2 * gc) ** -0.5),
        "ssm_b_im": nrm(ks[17], (ns, 2, g, p, gc), (2 * gc) ** -0.5),
        "ssm_c_re": nrm(ks[18], (ns, 2, g, gc, p), (2 * p) ** -0.5),
        "ssm_c_im": nrm(ks[19], (ns, 2, g, gc, p), (2 * p) ** -0.5),
        "ssm_d": nrm(ks[20], (ns, d), 1.0),
        "ssm_w_glu": nrm(ks[21], (ns, d, 2 * d), d ** -0.5),
    }


def reference(x, c, ctx, c_ctx, ada_w, ada_b, norm_g, ffn_w_in, ffn_w_out,
              attn_w_in, attn_w_out, attn_sink,
              ssm_w_in, ssm_a_re, ssm_a_im, ssm_log_dt, ssm_b_re, ssm_b_im,
              ssm_c_re, ssm_c_im, ssm_d, ssm_w_glu):
    seq_len = x.shape[1]
    ROWS = seq_len // GRID_W
    cos, sin = axial_rope_tables(ROWS)
    c_act = jax.nn.silu(c)
    cc_act = jax.nn.silu(c_ctx)
    for i in range(DEPTH):
        need_ctx = i < DEPTH - 1
        mx = jnp.split((c_act @ ada_w[i] + ada_b[i])[:, None, :], N_MOD, axis=-1)
        mc = jnp.split((cc_act @ ada_w[i] + ada_b[i])[None, None, :], N_MOD, axis=-1)
        g = norm_g[i]
        x = post_sublayer(x, swiglu(pre_sublayer(x, g[0], mx[0], mx[1]), ffn_w_in[i, 0], ffn_w_out[i, 0]), g[1], mx[2], 0.5)
        ctx = post_sublayer(ctx, swiglu(pre_sublayer(ctx, g[0], mc[0], mc[1]), ffn_w_in[i, 0], ffn_w_out[i, 0]), g[1], mc[2], 0.5)
        hx = pre_sublayer(x, g[2], mx[3], mx[4])
        hc = pre_sublayer(ctx, g[2], mc[3], mc[4])
        j = i // N_MIXERS
        if i % N_MIXERS == 0:
            ox, oc = windowed_gqa(hx, hc, attn_w_in[j], attn_w_out[j], attn_sink[j], cos, sin, need_ctx)
        else:
            ox, oc = s5_mixer(hx, hc, ssm_w_in[j], ssm_a_re[j], ssm_a_im[j], ssm_log_dt[j],
                              ssm_b_re[j], ssm_b_im[j], ssm_c_re[j], ssm_c_im[j], ssm_d[j], ssm_w_glu[j], need_ctx)
        x = post_sublayer(x, ox, g[3], mx[5], 1.0)
        x = post_sublayer(x, swiglu(pre_sublayer(x, g[4], mx[6], mx[7]), ffn_w_in[i, 1], ffn_w_out[i, 1]), g[5], mx[8], 0.5)
        if need_ctx:
            ctx = post_sublayer(ctx, oc, g[3], mc[5], 1.0)
            ctx = post_sublayer(ctx, swiglu(pre_sublayer(ctx, g[4], mc[6], mc[7]), ffn_w_in[i, 1], ffn_w_out[i, 1]), g[5], mc[8], 0.5)
    return x
```

```python
import functools
import math

import jax
import jax.numpy as jnp
from jax import lax
from jax.experimental import pallas as pl
from jax.experimental.pallas import tpu as pltpu

EPS = 1e-6
NEG_INF = -1e30
N_MOD = 9
GRID_W = 64
N_HEADS = 16
N_KV_HEADS = 4
HEAD_DIM = 128
GQA_GROUP = N_HEADS // N_KV_HEADS
ATT_BLOCK = 128
ROPE_BASE = 10000.0
SSM_GROUP_CH = 16
SSM_STATE = 64
SSM_CHUNK = 16
GROUPS_PER_STEP = 2

TOKEN_TILE = 512
FF_TILE = 512
GLU_TILE = 512
MOD_TILE = 1024
MOD_ROWS = 16
VMEM_LIMIT = 56 * 1024 * 1024

BF16 = jnp.bfloat16
F32 = jnp.float32


def _params(*sem):
    return pltpu.CompilerParams(dimension_semantics=sem, vmem_limit_bytes=VMEM_LIMIT)


def _rms(t, g):
    return t * lax.rsqrt(jnp.mean(t * t, axis=-1, keepdims=True) + EPS) * g


def _sigmoid(t):
    return 1.0 / (1.0 + jnp.exp(-t))


def _gelu_tanh(t):
    return 0.5 * t * (1.0 + jnp.tanh(math.sqrt(2.0 / math.pi) * (t + 0.044715 * (t * t * t))))


def _mod_kernel(ct_ref, w_ref, b_ref, o_ref, *, n_rows):
    w = w_ref[0]
    for r in range(n_rows):
        col = ct_ref[:, r:r + 1]
        col = col * _sigmoid(col)
        o_ref[0, r:r + 1, :] = jnp.sum(w * col, axis=0, keepdims=True) + b_ref[0]
    o_ref[0, n_rows:, :] = jnp.zeros((o_ref.shape[1] - n_rows, o_ref.shape[2]), F32)


def _modulation(cvec_t, ada_w, ada_b, n_rows):
    depth, d, nd = ada_w.shape
    tn = MOD_TILE
    return pl.pallas_call(
        functools.partial(_mod_kernel, n_rows=n_rows),
        out_shape=jax.ShapeDtypeStruct((depth, 8, nd), F32),
        grid=(depth, nd // tn),
        in_specs=[
            pl.BlockSpec((d, 8), lambda l, j: (0, 0)),
            pl.BlockSpec((1, d, tn), lambda l, j: (l, 0, j)),
            pl.BlockSpec((1, 1, tn), lambda l, j: (l, 0, j)),
        ],
        out_specs=pl.BlockSpec((1, 8, tn), lambda l, j: (l, 0, j)),
        compiler_params=_params("arbitrary", "arbitrary"),
        name="adaln_mod",
    )(cvec_t, ada_w, ada_b.reshape(depth, 1, nd))


def _ffn_kernel(x_ref, mod_ref, g_ref, wg_ref, wu_ref, wo_ref, o_ref, h_scr, acc_scr, *, mi, gi):
    f = pl.program_id(1)

    @pl.when(f == 0)
    def _():
        h = _rms(x_ref[...], g_ref[gi:gi + 1, :]) * (1.0 + mod_ref[0, mi + 1:mi + 2, :]) + mod_ref[0, mi:mi + 1, :]
        h_scr[...] = h.astype(BF16)
        acc_scr[...] = jnp.zeros_like(acc_scr)

    h = h_scr[...]
    gate = jnp.dot(h, wg_ref[...], preferred_element_type=F32)
    up = jnp.dot(h, wu_ref[...], preferred_element_type=F32)
    act = (gate * _sigmoid(gate) * up).astype(BF16)
    acc_scr[...] += jnp.dot(act, wo_ref[...], preferred_element_type=F32)

    @pl.when(f == pl.num_programs(1) - 1)
    def _():
        y = _rms(acc_scr[...], g_ref[gi + 1:gi + 2, :])
        o_ref[...] = x_ref[...] + 0.5 * mod_ref[0, mi + 2:mi + 3, :] * y


def _ffn(xs, mod, g, w_in, w_out, *, mi, gi, n_tiles, seg_of):
    d = xs.shape[1]
    ff = w_out.shape[0]
    tm, tf = TOKEN_TILE, FF_TILE
    nf = ff // tf
    return pl.pallas_call(
        functools.partial(_ffn_kernel, mi=mi, gi=gi),
        out_shape=jax.ShapeDtypeStruct((n_tiles * tm, d), F32),
        grid=(n_tiles, nf),
        in_specs=[
            pl.BlockSpec((tm, d), lambda i, f: (i, 0)),
            pl.BlockSpec((1, MOD_ROWS, d), lambda i, f: (seg_of(i), 0, 0)),
            pl.BlockSpec((8, d), lambda i, f: (0, 0)),
            pl.BlockSpec((d, tf), lambda i, f: (0, f)),
            pl.BlockSpec((d, tf), lambda i, f: (0, nf + f)),
            pl.BlockSpec((tf, d), lambda i, f: (f, 0)),
        ],
        out_specs=pl.BlockSpec((tm, d), lambda i, f: (i, 0)),
        scratch_shapes=[pltpu.VMEM((tm, d), BF16), pltpu.VMEM((tm, d), F32)],
        compiler_params=_params("parallel", "arbitrary"),
        name="ffn_halfstep",
    )(xs, mod, g, w_in, w_in, w_out)


def _rope(t, cos, sin, even_quarter):
    partner = jnp.where(even_quarter, pltpu.roll(t, HEAD_DIM - 32, 1), pltpu.roll(t, 32, 1))
    return t * cos + partner * sin


def _qkv_kernel(x_ref, mod_ref, g_ref, w_ref, cos_ref, sin_ref, q_ref, k_ref, v_ref, *, mi, gi):
    h = _rms(x_ref[...], g_ref[gi:gi + 1, :]) * (1.0 + mod_ref[0, mi + 1:mi + 2, :]) + mod_ref[0, mi:mi + 1, :]
    qkv = jnp.dot(h.astype(BF16), w_ref[...], preferred_element_type=F32)
    cos, sin = cos_ref[...], sin_ref[...]
    lane = lax.broadcasted_iota(jnp.int32, cos.shape, 1)
    even_quarter = (lane & 32) == 0
    scale = HEAD_DIM ** -0.5
    nq, nk = N_HEADS * HEAD_DIM, N_KV_HEADS * HEAD_DIM
    for hh in range(N_HEADS):
        sl = slice(hh * HEAD_DIM, (hh + 1) * HEAD_DIM)
        q_ref[:, sl] = (_rope(qkv[:, sl], cos, sin, even_quarter) * scale).astype(BF16)
    for hh in range(N_KV_HEADS):
        sl = slice(hh * HEAD_DIM, (hh + 1) * HEAD_DIM)
        src = slice(nq + hh * HEAD_DIM, nq + (hh + 1) * HEAD_DIM)
        k_ref[:, sl] = _rope(qkv[:, src], cos, sin, even_quarter).astype(BF16)
    v_ref[...] = qkv[:, nq + nk:].astype(BF16)


def _qkv(xs, mod, g, w, cos, sin, *, mi, gi, n_tiles, seg_of, pos_of):
    d = xs.shape[1]
    tm = TOKEN_TILE
    nq, nk = N_HEADS * HEAD_DIM, N_KV_HEADS * HEAD_DIM
    rows = n_tiles * tm
    return pl.pallas_call(
        functools.partial(_qkv_kernel, mi=mi, gi=gi),
        out_shape=(jax.ShapeDtypeStruct((rows, nq), BF16), jax.ShapeDtypeStruct((rows, nk), BF16),
                   jax.ShapeDtypeStruct((rows, nk), BF16)),
        grid=(n_tiles,),
        in_specs=[
            pl.BlockSpec((tm, d), lambda i: (i, 0)),
            pl.BlockSpec((1, MOD_ROWS, d), lambda i: (seg_of(i), 0, 0)),
            pl.BlockSpec((8, d), lambda i: (0, 0)),
            pl.BlockSpec((d, nq + 2 * nk), lambda i: (0, 0), pipeline_mode=pl.Buffered(1)),
            pl.BlockSpec((tm, HEAD_DIM), lambda i: (pos_of(i), 0)),
            pl.BlockSpec((tm, HEAD_DIM), lambda i: (pos_of(i), 0)),
        ],
        out_specs=(pl.BlockSpec((tm, nq), lambda i: (i, 0)), pl.BlockSpec((tm, nk), lambda i: (i, 0)),
                   pl.BlockSpec((tm, nk), lambda i: (i, 0))),
        compiler_params=_params("parallel"),
        name="attn_qkv_rope",
    )(xs, mod, g, w, cos, sin)


def _proj_in_kernel(x_ref, mod_ref, g_ref, w_ref, o_ref, *, mi, gi):
    h = _rms(x_ref[...], g_ref[gi:gi + 1, :]) * (1.0 + mod_ref[0, mi + 1:mi + 2, :]) + mod_ref[0, mi:mi + 1, :]
    o_ref[...] = jnp.dot(h.astype(BF16), w_ref[...], preferred_element_type=F32).astype(o_ref.dtype)


def _proj_in(xs, mod, g, w, *, mi, gi, n_tiles, seg_of):
    d = xs.shape[1]
    tm = TOKEN_TILE
    return pl.pallas_call(
        functools.partial(_proj_in_kernel, mi=mi, gi=gi),
        out_shape=jax.ShapeDtypeStruct((n_tiles * tm, w.shape[1]), BF16),
        grid=(n_tiles,),
        in_specs=[
            pl.BlockSpec((tm, d), lambda i: (i, 0)),
            pl.BlockSpec((1, MOD_ROWS, d), lambda i: (seg_of(i), 0, 0)),
            pl.BlockSpec((8, d), lambda i: (0, 0)),
            pl.BlockSpec(w.shape, lambda i: (0, 0), pipeline_mode=pl.Buffered(1)),
        ],
        out_specs=pl.BlockSpec((tm, w.shape[1]), lambda i: (i, 0)),
        compiler_params=_params("parallel"),
        name="ssm_in_proj",
    )(xs, mod, g, w)


def _proj_out_kernel(a_ref, x_ref, mod_ref, g_ref, w_ref, o_ref, *, mi, gi):
    out = jnp.dot(a_ref[...], w_ref[...], preferred_element_type=F32)
    o_ref[...] = x_ref[...] + mod_ref[0, mi:mi + 1, :] * _rms(out, g_ref[gi:gi + 1, :])


def _proj_out(a, xs, mod, g, w, *, mi, gi, n_tiles, seg_of):
    d = xs.shape[1]
    tm = TOKEN_TILE
    return pl.pallas_call(
        functools.partial(_proj_out_kernel, mi=mi, gi=gi),
        out_shape=jax.ShapeDtypeStruct((n_tiles * tm, d), F32),
        grid=(n_tiles,),
        in_specs=[
            pl.BlockSpec((tm, a.shape[1]), lambda i: (i, 0)),
            pl.BlockSpec((tm, d), lambda i: (i, 0)),
            pl.BlockSpec((1, MOD_ROWS, d), lambda i: (seg_of(i), 0, 0)),
            pl.BlockSpec((8, d), lambda i: (0, 0)),
            pl.BlockSpec(w.shape, lambda i: (0, 0), pipeline_mode=pl.Buffered(1)),
        ],
        out_specs=pl.BlockSpec((tm, d), lambda i: (i, 0)),
        compiler_params=_params("parallel"),
        name="attn_out_proj",
    )(a, xs, mod, g, w)


def _attn_kernel(sink_ref, q_ref, kp_ref, ko_ref, kn_ref, vp_ref, vo_ref, vn_ref, kc_ref, vc_ref, o_ref,
                 *, n_blk, seq_len):
    n = pl.program_id(1)
    blk = ATT_BLOCK
    c_len = kc_ref.shape[0]
    n_keys = 3 * blk + c_len
    i = lax.broadcasted_iota(jnp.int32, (GQA_GROUP * blk, n_keys), 0) & (blk - 1)
    j = lax.broadcasted_iota(jnp.int32, (GQA_GROUP * blk, n_keys), 1)
    rel = j - i
    kpos = (n - 1) * blk + j
    kpos_end = jnp.where(n < n_blk, seq_len, 0)
    band = (rel >= 0) & (rel <= 2 * blk) & (kpos >= 0) & (kpos < kpos_end)
    mask = band | (j >= 3 * blk)
    row = lax.broadcasted_iota(jnp.int32, (GQA_GROUP * blk, 1), 0)
    for h in range(N_KV_HEADS):
        hs = slice(h * HEAD_DIM, (h + 1) * HEAD_DIM)
        qs = jnp.concatenate(
            [q_ref[:, (h * GQA_GROUP + gq) * HEAD_DIM:(h * GQA_GROUP + gq + 1) * HEAD_DIM] for gq in range(GQA_GROUP)],
            axis=0)
        kb = jnp.concatenate([kp_ref[:, hs], ko_ref[:, hs], kn_ref[:, hs], kc_ref[:, hs]], axis=0)
        vb = jnp.concatenate([vp_ref[:, hs], vo_ref[:, hs], vn_ref[:, hs], vc_ref[:, hs]], axis=0)
        s = lax.dot_general(qs, kb, (((1,), (1,)), ((), ())), preferred_element_type=F32)
        s = jnp.where(mask, s, NEG_INF)
        sink = jnp.full((GQA_GROUP * blk, 1), sink_ref[h * GQA_GROUP], F32)
        for gq in range(1, GQA_GROUP):
            sink = jnp.where(row >= gq * blk, sink_ref[h * GQA_GROUP + gq], sink)
        m = jnp.maximum(jnp.max(s, axis=-1, keepdims=True), sink)
        p = jnp.exp(s - m)
        denom = jnp.sum(p, axis=-1, keepdims=True) + jnp.exp(sink - m)
        o = jnp.dot(p.astype(BF16), vb, preferred_element_type=F32) / denom
        for gq in range(GQA_GROUP):
            col = (h * GQA_GROUP + gq) * HEAD_DIM
            o_ref[:, col:col + HEAD_DIM] = o[gq * blk:(gq + 1) * blk, :].astype(o_ref.dtype)


def _attention(q, k, v, sink, *, batch, seq_len, c_len):
    blk = ATT_BLOCK
    n_blk, c_blk = seq_len // blk, c_len // blk
    nq, nk = q.shape[1], k.shape[1]
    lat_blocks = batch * n_blk
    ctx_block0 = batch * seq_len // c_len

    def q_idx(b, n):
        return jnp.where(n < n_blk, b * n_blk + n, lat_blocks + b * c_blk + (n - n_blk))

    def band_idx(off):
        def idx(b, n, s):
            nn = jnp.clip(n + off, 0, n_blk - 1)
            return (b * n_blk + nn, 0)
        return idx

    kv_spec = [pl.BlockSpec((blk, nk), band_idx(off)) for off in (-1, 0, 1)]
    ctx_spec = pl.BlockSpec((c_len, nk), lambda b, n, s: (ctx_block0 + b, 0))
    grid_spec = pltpu.PrefetchScalarGridSpec(
        num_scalar_prefetch=1,
        grid=(batch, n_blk + c_blk),
        in_specs=[pl.BlockSpec((blk, nq), lambda b, n, s: (q_idx(b, n), 0))] + kv_spec + kv_spec + [ctx_spec, ctx_spec],
        out_specs=pl.BlockSpec((blk, nq), lambda b, n, s: (q_idx(b, n), 0)),
    )
    return pl.pallas_call(
        functools.partial(_attn_kernel, n_blk=n_blk, seq_len=seq_len),
        out_shape=jax.ShapeDtypeStruct(q.shape, BF16),
        grid_spec=grid_spec,
        compiler_params=_params("parallel", "arbitrary"),
        name="window_gqa",
    )(sink, q, k, k, k, v, v, v, k, v)


def _s5_kernel(u_ref, t_ref, ws_ref, wc_ref, al_ref, y_ref, s_scr, h_scr, *, batch, n_lat, n_ctx):
    half = GROUPS_PER_STEP * SSM_STATE
    u = u_ref[0]
    s_scr[...] = jnp.dot(u, ws_ref[0], preferred_element_type=F32)
    sub = 8
    a_f = tuple(jnp.broadcast_to(al_ref[0, r:r + 1, :], (sub, half)) for r in (0, 1))
    a_b = tuple(jnp.broadcast_to(al_ref[0, r:r + 1, :], (sub, half)) for r in (2, 3))
    ctx0 = batch * n_lat
    zero = jnp.zeros((sub, half), F32)
    sub_row = lax.broadcasted_iota(jnp.int32, (sub, half), 0)

    def scan_block(h, a, base, col, order, keep):
        h_re, h_im = h
        a_re, a_im = a
        base = pl.multiple_of(base, sub)
        s_re = s_scr[pl.ds(base, sub), col:col + half]
        s_im = s_scr[pl.ds(base, sub), col + half:col + 2 * half]
        e_re, e_im = zero, zero
        for r in order:
            if keep:
                e_re = jnp.where(sub_row == r, h_re, e_re)
                e_im = jnp.where(sub_row == r, h_im, e_im)
            r_re = jnp.broadcast_to(s_re[r:r + 1, :], (sub, half))
            r_im = jnp.broadcast_to(s_im[r:r + 1, :], (sub, half))
            h_re, h_im = a_re * h_re - a_im * h_im + r_re, a_re * h_im + a_im * h_re + r_im
        if keep:
            h_scr[pl.ds(base, sub), col:col + half] = e_re
            h_scr[pl.ds(base, sub), col + half:col + 2 * half] = e_im
        return h_re, h_im

    fwd_order, bwd_order = tuple(range(sub)), tuple(range(sub - 1, -1, -1))

    def make_step(first_row, n_rows, keep):
        def step(m, carry):
            out = []
            for b in range(batch):
                hf, hb = carry[2 * b], carry[2 * b + 1]
                lo = first_row + b * n_rows
                hf = scan_block(hf, a_f, lo + m * sub, 0, fwd_order, keep)
                hb = scan_block(hb, a_b, lo + n_rows - sub - m * sub, 2 * half, bwd_order, keep)
                out += [hf, hb]
            return tuple(out)
        return step

    carry = lax.fori_loop(0, n_ctx // sub, make_step(ctx0, n_ctx, False), ((zero, zero),) * (2 * batch))
    lax.fori_loop(0, n_lat // sub, make_step(0, n_lat, True), carry)

    rows = batch * n_lat
    width = u.shape[1] // GROUPS_PER_STEP
    y_state = jnp.dot(h_scr[...].astype(BF16), wc_ref[0], preferred_element_type=F32)
    for gi in range(GROUPS_PER_STEP):
        sl = slice(gi * width, (gi + 1) * width)
        y_ref[0, :, sl] = jnp.dot(u[:rows, sl], t_ref[0, gi], preferred_element_type=F32) + y_state[:, sl]


def _s5_core(ug, toep, ws, wc, al, *, batch, n_lat, n_ctx):
    n_steps, chunks, width = ug.shape
    rows = batch * n_lat
    return pl.pallas_call(
        functools.partial(_s5_kernel, batch=batch, n_lat=n_lat, n_ctx=n_ctx),
        out_shape=jax.ShapeDtypeStruct((n_steps, rows, width), F32),
        grid=(n_steps,),
        in_specs=[
            pl.BlockSpec((1, chunks, width), lambda j: (j, 0, 0)),
            pl.BlockSpec((1,) + toep.shape[1:], lambda j: (j, 0, 0, 0)),
            pl.BlockSpec((1,) + ws.shape[1:], lambda j: (j, 0, 0)),
            pl.BlockSpec((1,) + wc.shape[1:], lambda j: (j, 0, 0)),
            pl.BlockSpec((1,) + al.shape[1:], lambda j: (j, 0, 0)),
        ],
        out_specs=pl.BlockSpec((1, rows, width), lambda j: (j, 0, 0)),
        scratch_shapes=[pltpu.VMEM((chunks, 4 * GROUPS_PER_STEP * SSM_STATE), F32),
                        pltpu.VMEM((rows, 4 * GROUPS_PER_STEP * SSM_STATE), F32)],
        compiler_params=_params("parallel"),
        name="s5_chunked",
    )(ug, toep, ws, wc, al)


def _s5_weights(a_re, a_im, log_dt, b_re, b_im, c_re, c_im):
    L = SSM_CHUNK
    n_g, n_p = a_re.shape[1], a_re.shape[2]
    gc = b_re.shape[-1]
    dt = jnp.exp(log_dt)[..., None]
    mag = jnp.exp(a_re * dt)
    ab_re, ab_im = mag * jnp.cos(a_im * dt), mag * jnp.sin(a_im * dt)
    den = a_re * a_re + a_im * a_im
    x_re, x_im = ab_re - 1.0, ab_im
    f_re = ((x_re * a_re + x_im * a_im) / den)[..., None]
    f_im = ((x_im * a_re - x_re * a_im) / den)[..., None]
    bb_re = f_re * b_re - f_im * b_im
    bb_im = f_re * b_im + f_im * b_re
    pw_re, pw_im = [jnp.ones_like(ab_re)], [jnp.zeros_like(ab_im)]
    for _ in range(L):
        pr, pi = pw_re[-1], pw_im[-1]
        pw_re.append(pr * ab_re - pi * ab_im)
        pw_im.append(pr * ab_im + pi * ab_re)
    pw_re, pw_im = jnp.stack(pw_re), jnp.stack(pw_im)
    ca_re = c_re[None] * pw_re[:, :, :, None, :] - c_im[None] * pw_im[:, :, :, None, :]
    ca_im = c_re[None] * pw_im[:, :, :, None, :] + c_im[None] * pw_re[:, :, :, None, :]
    taps = (jnp.einsum('kdgcp,dgpe->kdgce', ca_re[:L], bb_re, precision=lax.Precision.HIGHEST)
            - jnp.einsum('kdgcp,dgpe->kdgce', ca_im[:L], bb_im, precision=lax.Precision.HIGHEST))
    s_i = jnp.arange(L)[:, None]
    t_i = jnp.arange(L)[None, :]
    lag = t_i - s_i
    k_f = jnp.where((lag >= 0)[:, :, None, None, None], taps[jnp.clip(lag, 0, L - 1), 0], 0.0)
    k_b = jnp.where((lag <= 0)[:, :, None, None, None], taps[jnp.clip(-lag, 0, L - 1), 1], 0.0)
    toep = jnp.transpose(k_f + k_b, (2, 0, 4, 1, 3))
    toep = toep.reshape(n_g // 2, 2, L * gc, L * gc)

    def inject(pr, pi, d):
        re = pr[:, :, :, None] * bb_re[d][None] - pi[:, :, :, None] * bb_im[d][None]
        im = pr[:, :, :, None] * bb_im[d][None] + pi[:, :, :, None] * bb_re[d][None]
        tr = lambda w: jnp.transpose(w, (1, 0, 3, 2)).reshape(n_g, L * gc, n_p)
        return tr(re), tr(im)

    inj = inject(pw_re[L - 1::-1, 0][:L], pw_im[L - 1::-1, 0][:L], 0) + inject(pw_re[:L, 1], pw_im[:L, 1], 1)
    def read(cr, ci):
        tr = lambda w: jnp.transpose(w, (1, 3, 0, 2)).reshape(n_g, n_p, L * gc)
        return tr(cr), tr(-ci)

    rd = read(ca_re[1:L + 1, 0], ca_im[1:L + 1, 0]) + read(ca_re[L:0:-1, 1], ca_im[L:0:-1, 1])

    eye = jnp.eye(2, dtype=F32)
    ws = jnp.stack([w.reshape(n_g // 2, 2, L * gc, n_p) for w in inj], axis=3)
    ws = jnp.einsum('jgsqp,gh->jgsqhp', ws, eye).reshape(n_g // 2, 2 * L * gc, 4 * 2 * n_p)
    wc = jnp.stack([w.reshape(n_g // 2, 2, n_p, L * gc) for w in rd], axis=1)
    wc = jnp.einsum('jqgpt,gh->jqgpht', wc, eye).reshape(n_g // 2, 4 * 2 * n_p, 2 * L * gc)
    al = jnp.stack([pw_re[L, 0], pw_im[L, 0], pw_re[L, 1], pw_im[L, 1]], axis=1)
    al = jnp.transpose(al.reshape(n_g // 2, 2, 4, n_p), (0, 2, 1, 3)).reshape(n_g // 2, 4, 2 * n_p)
    al = jnp.concatenate([al, jnp.zeros_like(al)], axis=1)
    return toep.astype(BF16), ws.astype(BF16), wc.astype(BF16), al


def _ssm_out_kernel(y_ref, u_ref, d_ref, x_ref, mod_ref, g_ref, wv_ref, wg_ref, o_ref, act_scr, out_scr, *, mi, gi):
    cix = pl.program_id(1)
    n_c = out_scr.shape[0]
    tn = out_scr.shape[2]

    @pl.when(cix == 0)
    def _():
        act_scr[...] = _gelu_tanh(y_ref[...] + d_ref[...] * u_ref[...].astype(F32)).astype(BF16)

    a = act_scr[...]
    val = jnp.dot(a, wv_ref[...], preferred_element_type=F32)
    gate = jnp.dot(a, wg_ref[...], preferred_element_type=F32)
    out_scr[cix] = val * _sigmoid(gate)

    @pl.when(cix == n_c - 1)
    def _():
        ssq = jnp.zeros((out_scr.shape[1], 1), F32)
        for cc in range(n_c):
            blk = out_scr[cc]
            ssq = ssq + jnp.sum(blk * blk, axis=-1, keepdims=True)
        inv = lax.rsqrt(ssq / (n_c * tn) + EPS)
        for cc in range(n_c):
            sl = slice(cc * tn, (cc + 1) * tn)
            o_ref[:, sl] = x_ref[:, sl] + mod_ref[0, mi:mi + 1, sl] * (out_scr[cc] * inv * g_ref[gi:gi + 1, sl])


def _ssm_out(y, u, d_skip, xs, mod, g, w_glu, *, mi, gi, n_tiles, seg_of):
    d = xs.shape[1]
    tm, tn = TOKEN_TILE, GLU_TILE
    n_c = d // tn
    return pl.pallas_call(
        functools.partial(_ssm_out_kernel, mi=mi, gi=gi),
        out_shape=jax.ShapeDtypeStruct((n_tiles * tm, d), F32),
        grid=(n_tiles, n_c),
        in_specs=[
            pl.BlockSpec((tm, d), lambda i, cc: (i, 0)),
            pl.BlockSpec((tm, d), lambda i, cc: (i, 0)),
            pl.BlockSpec((1, d), lambda i, cc: (0, 0)),
            pl.BlockSpec((tm, d), lambda i, cc: (i, 0)),
            pl.BlockSpec((1, MOD_ROWS, d), lambda i, cc: (seg_of(i), 0, 0)),
            pl.BlockSpec((8, d), lambda i, cc: (0, 0)),
            pl.BlockSpec((d, tn), lambda i, cc: (0, cc)),
            pl.BlockSpec((d, tn), lambda i, cc: (0, n_c + cc)),
        ],
        out_specs=pl.BlockSpec((tm, d), lambda i, cc: (i, 0)),
        scratch_shapes=[pltpu.VMEM((tm, d), BF16), pltpu.VMEM((n_c, tm, tn), F32)],
        compiler_params=_params("parallel", "arbitrary"),
        name="ssm_glu_out",
    )(y, u, d_skip.reshape(1, d), xs, mod, g, w_glu, w_glu)


def _rope_tables(seq_len, extra_rows):
    rows = seq_len // GRID_W
    row = jnp.repeat(jnp.arange(rows, dtype=F32), GRID_W)
    col = jnp.tile(jnp.arange(GRID_W, dtype=F32), rows)
    axis_dim = HEAD_DIM // 2
    inv_freq = ROPE_BASE ** (-jnp.arange(0, axis_dim, 2, dtype=F32) / axis_dim)
    ang_r, ang_c = row[:, None] * inv_freq, col[:, None] * inv_freq
    cos = jnp.concatenate([jnp.cos(ang_r)] * 2 + [jnp.cos(ang_c)] * 2, axis=-1)
    sin = jnp.concatenate([-jnp.sin(ang_r), jnp.sin(ang_r), -jnp.sin(ang_c), jnp.sin(ang_c)], axis=-1)
    cos = jnp.concatenate([cos, jnp.ones((extra_rows, HEAD_DIM), F32)], axis=0)
    sin = jnp.concatenate([sin, jnp.zeros((extra_rows, HEAD_DIM), F32)], axis=0)
    return cos, sin


def kernel(x, c, ctx, c_ctx, ada_w, ada_b, norm_g, ffn_w_in, ffn_w_out, attn_w_in, attn_w_out, attn_sink,
           ssm_w_in, ssm_a_re, ssm_a_im, ssm_log_dt, ssm_b_re, ssm_b_im, ssm_c_re, ssm_c_im, ssm_d, ssm_w_glu):
    batch, seq_len, d = x.shape
    c_len = ctx.shape[1]
    depth = ada_w.shape[0]
    tm = TOKEN_TILE
    n_lat_rows, n_ctx_rows = batch * seq_len, batch * c_len
    assert seq_len % tm == 0 and n_ctx_rows % tm == 0 and n_lat_rows % c_len == 0
    assert seq_len % ATT_BLOCK == 0 and c_len % ATT_BLOCK == 0 and seq_len % GRID_W == 0
    tiles_per_batch = seq_len // tm
    lat_tiles = n_lat_rows // tm
    all_tiles = lat_tiles + n_ctx_rows // tm

    def seg_of(i):
        return jnp.minimum(i // tiles_per_batch, batch)

    def pos_of(i):
        return jnp.where(i < lat_tiles, i % tiles_per_batch, tiles_per_batch + (i - lat_tiles))

    xs = jnp.concatenate([x.reshape(n_lat_rows, d), ctx.reshape(n_ctx_rows, d)], axis=0)

    n_cond = batch + 1
    cvec = jnp.concatenate([c, c_ctx[None], jnp.zeros((8 - n_cond, d), F32)], axis=0)
    mods = _modulation(cvec.T, ada_w, ada_b, n_cond)
    mods = mods[:, :n_cond].reshape(depth, n_cond, N_MOD, d)
    mods = jnp.pad(mods, ((0, 0), (0, 0), (0, MOD_ROWS - N_MOD), (0, 0)))
    gains = jnp.pad(norm_g, ((0, 0), (0, 8 - norm_g.shape[1]), (0, 0)))

    cos, sin = _rope_tables(seq_len, n_ctx_rows)

    for i in range(depth):
        need_ctx = i < depth - 1
        mod, g = mods[i], gains[i]
        j = i // 2
        xs = _ffn(xs, mod, g, ffn_w_in[i, 0].astype(BF16), ffn_w_out[i, 0].astype(BF16),
                  mi=0, gi=0, n_tiles=all_tiles, seg_of=seg_of)
        post_tiles = all_tiles if need_ctx else lat_tiles
        if i % 2 == 0:
            q, k, v = _qkv(xs, mod, g, attn_w_in[j].astype(BF16), cos, sin,
                           mi=3, gi=2, n_tiles=all_tiles, seg_of=seg_of, pos_of=pos_of)
            o = _attention(q, k, v, attn_sink[j], batch=batch, seq_len=seq_len, c_len=c_len)
            xs = _proj_out(o, xs, mod, g, attn_w_out[j].astype(BF16),
                           mi=5, gi=3, n_tiles=post_tiles, seg_of=seg_of)
        else:
            if need_ctx:
                raise NotImplementedError("context output of the S5 mixer is only needed when another layer follows")
            u = _proj_in(xs, mod, g, ssm_w_in[j].astype(BF16), mi=3, gi=2, n_tiles=all_tiles, seg_of=seg_of)
            L = SSM_CHUNK
            n_g = d // SSM_GROUP_CH
            chunks = (n_lat_rows + n_ctx_rows) // L
            ug = u.reshape(chunks, L, n_g // 2, 2, SSM_GROUP_CH)
            ug = jnp.transpose(ug, (2, 0, 3, 1, 4)).reshape(n_g // 2, chunks, 2 * L * SSM_GROUP_CH)
            toep, ws, wc, al = _s5_weights(ssm_a_re[j], ssm_a_im[j], ssm_log_dt[j], ssm_b_re[j], ssm_b_im[j],
                                           ssm_c_re[j], ssm_c_im[j])
            yg = _s5_core(ug, toep, ws, wc, al, batch=batch, n_lat=seq_len // L, n_ctx=c_len // L)
            y = yg.reshape(n_g // 2, n_lat_rows // L, 2, L, SSM_GROUP_CH)
            y = jnp.transpose(y, (1, 3, 0, 2, 4)).reshape(n_lat_rows, d)
            xs = _ssm_out(y, u, ssm_d[j], xs, mod, g, ssm_w_glu[j].astype(BF16),
                          mi=5, gi=3, n_tiles=post_tiles, seg_of=seg_of)
        xs = _ffn(xs, mod, g, ffn_w_in[i, 1].astype(BF16), ffn_w_out[i, 1].astype(BF16),
                  mi=6, gi=4, n_tiles=post_tiles, seg_of=seg_of)
    return xs[:n_lat_rows].reshape(batch, seq_len, d)
```

```python
import functools
import math

import jax
import jax.numpy as jnp
from jax import lax
from jax.experimental import pallas as pl
from jax.experimental.pallas import tpu as pltpu

EPS = 1e-6
NEG_INF = -1e30
N_MOD = 9
GRID_W = 64
N_HEADS = 16
N_KV_HEADS = 4
HEAD_DIM = 128
GQA_GROUP = N_HEADS // N_KV_HEADS
ATT_BLOCK = 128
ROPE_BASE = 10000.0
SSM_GROUP_CH = 16
SSM_STATE = 64
SSM_CHUNK = 16
GROUPS_PER_STEP = 2

TOKEN_TILE = 512
FF_TILE = 512
GLU_TILE = 512
MOD_TILE = 1024
MOD_ROWS = 16
VMEM_LIMIT = 56 * 1024 * 1024
LANES = 128

BF16 = jnp.bfloat16
F32 = jnp.float32


def _params(*sem):
    return pltpu.CompilerParams(dimension_semantics=sem, vmem_limit_bytes=VMEM_LIMIT)


def _rms(t, g):
    return t * lax.rsqrt(jnp.mean(t * t, axis=-1, keepdims=True) + EPS) * g


def _sigmoid(t):
    return 1.0 / (1.0 + jnp.exp(-t))


def _gelu_tanh(t):
    return 0.5 * t * (1.0 + jnp.tanh(math.sqrt(2.0 / math.pi) * (t + 0.044715 * (t * t * t))))


def _mod_kernel(ct_ref, w_ref, b_ref, o_ref, *, n_rows):
    w = w_ref[0]
    for r in range(n_rows):
        col = ct_ref[:, r:r + 1]
        col = col * _sigmoid(col)
        o_ref[0, r:r + 1, :] = jnp.sum(w * col, axis=0, keepdims=True) + b_ref[0]
    o_ref[0, n_rows:, :] = jnp.zeros((o_ref.shape[1] - n_rows, o_ref.shape[2]), F32)


def _modulation(cvec_t, ada_w, ada_b, n_rows):
    depth, d, nd = ada_w.shape
    tn = MOD_TILE
    return pl.pallas_call(
        functools.partial(_mod_kernel, n_rows=n_rows),
        out_shape=jax.ShapeDtypeStruct((depth, 8, nd), F32),
        grid=(depth, nd // tn),
        in_specs=[
            pl.BlockSpec((d, 8), lambda l, j: (0, 0)),
            pl.BlockSpec((1, d, tn), lambda l, j: (l, 0, j)),
            pl.BlockSpec((1, 1, tn), lambda l, j: (l, 0, j)),
        ],
        out_specs=pl.BlockSpec((1, 8, tn), lambda l, j: (l, 0, j)),
        compiler_params=_params("arbitrary", "arbitrary"),
        name="adaln_mod",
    )(cvec_t, ada_w, ada_b.reshape(depth, 1, nd))


def _ffn_kernel(x_ref, mod_ref, g_ref, wg_ref, wu_ref, wo_ref, o_ref, h_scr, acc_scr, *, mi, gi):
    f = pl.program_id(1)

    @pl.when(f == 0)
    def _():
        h = _rms(x_ref[...], g_ref[gi:gi + 1, :]) * (1.0 + mod_ref[0, mi + 1:mi + 2, :]) + mod_ref[0, mi:mi + 1, :]
        h_scr[...] = h.astype(BF16)
        acc_scr[...] = jnp.zeros_like(acc_scr)

    h = h_scr[...]
    gate = jnp.dot(h, wg_ref[...], preferred_element_type=F32)
    up = jnp.dot(h, wu_ref[...], preferred_element_type=F32)
    act = (gate * _sigmoid(gate) * up).astype(BF16)
    acc_scr[...] += jnp.dot(act, wo_ref[...], preferred_element_type=F32)

    @pl.when(f == pl.num_programs(1) - 1)
    def _():
        y = _rms(acc_scr[...], g_ref[gi + 1:gi + 2, :])
        o_ref[...] = x_ref[...] + 0.5 * mod_ref[0, mi + 2:mi + 3, :] * y


def _ffn(xs, mod, g, w_in, w_out, *, mi, gi, n_tiles, seg_of):
    d = xs.shape[1]
    ff = w_out.shape[0]
    tm, tf = TOKEN_TILE, FF_TILE
    nf = ff // tf
    return pl.pallas_call(
        functools.partial(_ffn_kernel, mi=mi, gi=gi),
        out_shape=jax.ShapeDtypeStruct((n_tiles * tm, d), F32),
        grid=(n_tiles, nf),
        in_specs=[
            pl.BlockSpec((tm, d), lambda i, f: (i, 0)),
            pl.BlockSpec((1, MOD_ROWS, d), lambda i, f: (seg_of(i), 0, 0)),
            pl.BlockSpec((8, d), lambda i, f: (0, 0)),
            pl.BlockSpec((d, tf), lambda i, f: (0, f)),
            pl.BlockSpec((d, tf), lambda i, f: (0, nf + f)),
            pl.BlockSpec((tf, d), lambda i, f: (f, 0)),
        ],
        out_specs=pl.BlockSpec((tm, d), lambda i, f: (i, 0)),
        scratch_shapes=[pltpu.VMEM((tm, d), BF16), pltpu.VMEM((tm, d), F32)],
        compiler_params=_params("parallel", "arbitrary"),
        name="ffn_halfstep",
    )(xs, mod, g, w_in, w_in, w_out)


def _rope(t, cos, sin, even_quarter):
    partner = jnp.where(even_quarter, pltpu.roll(t, HEAD_DIM - 32, 1), pltpu.roll(t, 32, 1))
    return t * cos + partner * sin


def _qkv_kernel(x_ref, mod_ref, g_ref, w_ref, cos_ref, sin_ref, q_ref, k_ref, v_ref, *, mi, gi):
    h = _rms(x_ref[...], g_ref[gi:gi + 1, :]) * (1.0 + mod_ref[0, mi + 1:mi + 2, :]) + mod_ref[0, mi:mi + 1, :]
    qkv = jnp.dot(h.astype(BF16), w_ref[...], preferred_element_type=F32)
    cos, sin = cos_ref[...], sin_ref[...]
    lane = lax.broadcasted_iota(jnp.int32, cos.shape, 1)
    even_quarter = (lane & 32) == 0
    scale = HEAD_DIM ** -0.5
    nq, nk = N_HEADS * HEAD_DIM, N_KV_HEADS * HEAD_DIM
    for hh in range(N_HEADS):
        sl = slice(hh * HEAD_DIM, (hh + 1) * HEAD_DIM)
        q_ref[:, sl] = (_rope(qkv[:, sl], cos, sin, even_quarter) * scale).astype(BF16)
    for hh in range(N_KV_HEADS):
        sl = slice(hh * HEAD_DIM, (hh + 1) * HEAD_DIM)
        src = slice(nq + hh * HEAD_DIM, nq + (hh + 1) * HEAD_DIM)
        k_ref[:, sl] = _rope(qkv[:, src], cos, sin, even_quarter).astype(BF16)
    v_ref[...] = qkv[:, nq + nk:].astype(BF16)


def _qkv(xs, mod, g, w, cos, sin, *, mi, gi, n_tiles, seg_of, pos_of):
    d = xs.shape[1]
    tm = TOKEN_TILE
    nq, nk = N_HEADS * HEAD_DIM, N_KV_HEADS * HEAD_DIM
    rows = n_tiles * tm
    return pl.pallas_call(
        functools.partial(_qkv_kernel, mi=mi, gi=gi),
        out_shape=(jax.ShapeDtypeStruct((rows, nq), BF16), jax.ShapeDtypeStruct((rows, nk), BF16),
                   jax.ShapeDtypeStruct((rows, nk), BF16)),
        grid=(n_tiles,),
        in_specs=[
            pl.BlockSpec((tm, d), lambda i: (i, 0)),
            pl.BlockSpec((1, MOD_ROWS, d), lambda i: (seg_of(i), 0, 0)),
            pl.BlockSpec((8, d), lambda i: (0, 0)),
            pl.BlockSpec((d, nq + 2 * nk), lambda i: (0, 0), pipeline_mode=pl.Buffered(1)),
            pl.BlockSpec((tm, HEAD_DIM), lambda i: (pos_of(i), 0)),
            pl.BlockSpec((tm, HEAD_DIM), lambda i: (pos_of(i), 0)),
        ],
        out_specs=(pl.BlockSpec((tm, nq), lambda i: (i, 0)), pl.BlockSpec((tm, nk), lambda i: (i, 0)),
                   pl.BlockSpec((tm, nk), lambda i: (i, 0))),
        compiler_params=_params("parallel"),
        name="attn_qkv_rope",
    )(xs, mod, g, w, cos, sin)


def _proj_in_kernel(x_ref, mod_ref, g_ref, w_ref, o_ref, slab_scr, *, mi, gi):
    h = _rms(x_ref[...], g_ref[gi:gi + 1, :]) * (1.0 + mod_ref[0, mi + 1:mi + 2, :]) + mod_ref[0, mi:mi + 1, :]
    u = jnp.dot(h.astype(BF16), w_ref[...], preferred_element_type=F32)
    n_slab, tm, lanes = slab_scr.shape
    pair = GROUPS_PER_STEP * SSM_GROUP_CH
    per_slab = lanes // pair
    for k in range(n_slab):
        slab_scr[k] = u[:, k * lanes:(k + 1) * lanes]
    for t in range(SSM_CHUNK):
        for k in range(n_slab):
            rows = slab_scr[k, pl.ds(t, tm // SSM_CHUNK, stride=SSM_CHUNK), :]
            for jj in range(per_slab):
                o_ref[k * per_slab + jj, :, t * pair:(t + 1) * pair] = rows[:, jj * pair:(jj + 1) * pair].astype(o_ref.dtype)


def _proj_in(xs, mod, g, w, *, mi, gi, n_tiles, seg_of):
    d = xs.shape[1]
    tm = TOKEN_TILE
    pair = GROUPS_PER_STEP * SSM_GROUP_CH
    n_pairs = w.shape[1] // pair
    return pl.pallas_call(
        functools.partial(_proj_in_kernel, mi=mi, gi=gi),
        out_shape=jax.ShapeDtypeStruct((n_pairs, n_tiles * tm // SSM_CHUNK, SSM_CHUNK * pair), BF16),
        grid=(n_tiles,),
        in_specs=[
            pl.BlockSpec((tm, d), lambda i: (i, 0)),
            pl.BlockSpec((1, MOD_ROWS, d), lambda i: (seg_of(i), 0, 0)),
            pl.BlockSpec((8, d), lambda i: (0, 0)),
            pl.BlockSpec(w.shape, lambda i: (0, 0), pipeline_mode=pl.Buffered(1)),
        ],
        out_specs=pl.BlockSpec((n_pairs, tm // SSM_CHUNK, SSM_CHUNK * pair), lambda i: (0, i, 0)),
        scratch_shapes=[pltpu.VMEM((w.shape[1] // LANES, tm, LANES), F32)],
        compiler_params=_params("parallel"),
        name="ssm_in_proj",
    )(xs, mod, g, w)


def _proj_out_kernel(a_ref, x_ref, mod_ref, g_ref, w_ref, o_ref, *, mi, gi):
    out = jnp.dot(a_ref[...], w_ref[...], preferred_element_type=F32)
    o_ref[...] = x_ref[...] + mod_ref[0, mi:mi + 1, :] * _rms(out, g_ref[gi:gi + 1, :])


def _proj_out(a, xs, mod, g, w, *, mi, gi, n_tiles, seg_of):
    d = xs.shape[1]
    tm = TOKEN_TILE
    return pl.pallas_call(
        functools.partial(_proj_out_kernel, mi=mi, gi=gi),
        out_shape=jax.ShapeDtypeStruct((n_tiles * tm, d), F32),
        grid=(n_tiles,),
        in_specs=[
            pl.BlockSpec((tm, a.shape[1]), lambda i: (i, 0)),
            pl.BlockSpec((tm, d), lambda i: (i, 0)),
            pl.BlockSpec((1, MOD_ROWS, d), lambda i: (seg_of(i), 0, 0)),
            pl.BlockSpec((8, d), lambda i: (0, 0)),
            pl.BlockSpec(w.shape, lambda i: (0, 0), pipeline_mode=pl.Buffered(1)),
        ],
        out_specs=pl.BlockSpec((tm, d), lambda i: (i, 0)),
        compiler_params=_params("parallel"),
        name="attn_out_proj",
    )(a, xs, mod, g, w)


def _attn_kernel(sink_ref, q_ref, kp_ref, ko_ref, kn_ref, vp_ref, vo_ref, vn_ref, kc_ref, vc_ref, o_ref,
                 *, n_blk, seq_len):
    n = pl.program_id(1)
    blk = ATT_BLOCK
    c_len = kc_ref.shape[0]
    n_keys = 3 * blk + c_len
    i = lax.broadcasted_iota(jnp.int32, (GQA_GROUP * blk, n_keys), 0) & (blk - 1)
    j = lax.broadcasted_iota(jnp.int32, (GQA_GROUP * blk, n_keys), 1)
    rel = j - i
    kpos = (n - 1) * blk + j
    kpos_end = jnp.where(n < n_blk, seq_len, 0)
    band = (rel >= 0) & (rel <= 2 * blk) & (kpos >= 0) & (kpos < kpos_end)
    mask = band | (j >= 3 * blk)
    row = lax.broadcasted_iota(jnp.int32, (GQA_GROUP * blk, 1), 0)
    for h in range(N_KV_HEADS):
        hs = slice(h * HEAD_DIM, (h + 1) * HEAD_DIM)
        qs = jnp.concatenate(
            [q_ref[:, (h * GQA_GROUP + gq) * HEAD_DIM:(h * GQA_GROUP + gq + 1) * HEAD_DIM] for gq in range(GQA_GROUP)],
            axis=0)
        kb = jnp.concatenate([kp_ref[:, hs], ko_ref[:, hs], kn_ref[:, hs], kc_ref[:, hs]], axis=0)
        vb = jnp.concatenate([vp_ref[:, hs], vo_ref[:, hs], vn_ref[:, hs], vc_ref[:, hs]], axis=0)
        s = lax.dot_general(qs, kb, (((1,), (1,)), ((), ())), preferred_element_type=F32)
        s = jnp.where(mask, s, NEG_INF)
        sink = jnp.full((GQA_GROUP * blk, 1), sink_ref[h * GQA_GROUP], F32)
        for gq in range(1, GQA_GROUP):
            sink = jnp.where(row >= gq * blk, sink_ref[h * GQA_GROUP + gq], sink)
        m = jnp.maximum(jnp.max(s, axis=-1, keepdims=True), sink)
        p = jnp.exp(s - m)
        denom = jnp.sum(p, axis=-1, keepdims=True) + jnp.exp(sink - m)
        o = jnp.dot(p.astype(BF16), vb, preferred_element_type=F32) / denom
        for gq in range(GQA_GROUP):
            col = (h * GQA_GROUP + gq) * HEAD_DIM
            o_ref[:, col:col + HEAD_DIM] = o[gq * blk:(gq + 1) * blk, :].astype(o_ref.dtype)


def _attention(q, k, v, sink, *, batch, seq_len, c_len):
    blk = ATT_BLOCK
    n_blk, c_blk = seq_len // blk, c_len // blk
    nq, nk = q.shape[1], k.shape[1]
    lat_blocks = batch * n_blk
    ctx_block0 = batch * seq_len // c_len

    def q_idx(b, n):
        return jnp.where(n < n_blk, b * n_blk + n, lat_blocks + b * c_blk + (n - n_blk))

    def band_idx(off):
        def idx(b, n, s):
            nn = jnp.clip(n + off, 0, n_blk - 1)
            return (b * n_blk + nn, 0)
        return idx

    kv_spec = [pl.BlockSpec((blk, nk), band_idx(off)) for off in (-1, 0, 1)]
    ctx_spec = pl.BlockSpec((c_len, nk), lambda b, n, s: (ctx_block0 + b, 0))
    grid_spec = pltpu.PrefetchScalarGridSpec(
        num_scalar_prefetch=1,
        grid=(batch, n_blk + c_blk),
        in_specs=[pl.BlockSpec((blk, nq), lambda b, n, s: (q_idx(b, n), 0))] + kv_spec + kv_spec + [ctx_spec, ctx_spec],
        out_specs=pl.BlockSpec((blk, nq), lambda b, n, s: (q_idx(b, n), 0)),
    )
    return pl.pallas_call(
        functools.partial(_attn_kernel, n_blk=n_blk, seq_len=seq_len),
        out_shape=jax.ShapeDtypeStruct(q.shape, BF16),
        grid_spec=grid_spec,
        compiler_params=_params("parallel", "arbitrary"),
        name="window_gqa",
    )(sink, q, k, k, k, v, v, v, k, v)


def _s5_kernel(u_ref, t_ref, ws_ref, wc_ref, al_ref, d_ref, y_ref, s_scr, h_scr, *, batch, n_lat, n_ctx):
    half = GROUPS_PER_STEP * SSM_STATE
    u = u_ref[0]
    s_scr[...] = jnp.dot(u, ws_ref[0], preferred_element_type=F32)
    sub = 8
    a_f = tuple(jnp.broadcast_to(al_ref[0, r:r + 1, :], (sub, half)) for r in (0, 1))
    a_b = tuple(jnp.broadcast_to(al_ref[0, r:r + 1, :], (sub, half)) for r in (2, 3))
    ctx0 = batch * n_lat
    zero = jnp.zeros((sub, half), F32)
    sub_row = lax.broadcasted_iota(jnp.int32, (sub, half), 0)

    def scan_block(h, a, base, col, order, keep):
        h_re, h_im = h
        a_re, a_im = a
        base = pl.multiple_of(base, sub)
        s_re = s_scr[pl.ds(base, sub), col:col + half]
        s_im = s_scr[pl.ds(base, sub), col + half:col + 2 * half]
        e_re, e_im = zero, zero
        for r in order:
            if keep:
                e_re = jnp.where(sub_row == r, h_re, e_re)
                e_im = jnp.where(sub_row == r, h_im, e_im)
            r_re = jnp.broadcast_to(s_re[r:r + 1, :], (sub, half))
            r_im = jnp.broadcast_to(s_im[r:r + 1, :], (sub, half))
            h_re, h_im = a_re * h_re - a_im * h_im + r_re, a_re * h_im + a_im * h_re + r_im
        if keep:
            h_scr[pl.ds(base, sub), col:col + half] = e_re
            h_scr[pl.ds(base, sub), col + half:col + 2 * half] = e_im
        return h_re, h_im

    fwd_order, bwd_order = tuple(range(sub)), tuple(range(sub - 1, -1, -1))

    def make_step(first_row, n_rows, keep):
        def step(m, carry):
            out = []
            for b in range(batch):
                hf, hb = carry[2 * b], carry[2 * b + 1]
                lo = first_row + b * n_rows
                hf = scan_block(hf, a_f, lo + m * sub, 0, fwd_order, keep)
                hb = scan_block(hb, a_b, lo + n_rows - sub - m * sub, 2 * half, bwd_order, keep)
                out += [hf, hb]
            return tuple(out)
        return step

    carry = lax.fori_loop(0, n_ctx // sub, make_step(ctx0, n_ctx, False), ((zero, zero),) * (2 * batch))
    lax.fori_loop(0, n_lat // sub, make_step(0, n_lat, True), carry)

    rows = batch * n_lat
    u_lat = u[:rows]
    y_ref[0] = (jnp.dot(u_lat, t_ref[0], preferred_element_type=F32)
                + jnp.dot(h_scr[...].astype(BF16), wc_ref[0], preferred_element_type=F32)
                + d_ref[0] * u_lat.astype(F32))


def _s5_core(ug, toep, ws, wc, al, d_lanes, *, batch, n_lat, n_ctx):
    n_steps, chunks, width = ug.shape
    assert n_lat % 8 == 0 and n_ctx % 8 == 0
    rows = batch * n_lat
    return pl.pallas_call(
        functools.partial(_s5_kernel, batch=batch, n_lat=n_lat, n_ctx=n_ctx),
        out_shape=jax.ShapeDtypeStruct((n_steps, rows, width), F32),
        grid=(n_steps,),
        in_specs=[
            pl.BlockSpec((1, chunks, width), lambda j: (j, 0, 0)),
            pl.BlockSpec((1,) + toep.shape[1:], lambda j: (j, 0, 0)),
            pl.BlockSpec((1,) + ws.shape[1:], lambda j: (j, 0, 0)),
            pl.BlockSpec((1,) + wc.shape[1:], lambda j: (j, 0, 0)),
            pl.BlockSpec((1,) + al.shape[1:], lambda j: (j, 0, 0)),
            pl.BlockSpec((1, 1, width), lambda j: (j, 0, 0)),
        ],
        out_specs=pl.BlockSpec((1, rows, width), lambda j: (j, 0, 0)),
        scratch_shapes=[pltpu.VMEM((chunks, 4 * GROUPS_PER_STEP * SSM_STATE), F32),
                        pltpu.VMEM((rows, 4 * GROUPS_PER_STEP * SSM_STATE), F32)],
        compiler_params=_params("parallel"),
        name="s5_chunked",
    )(ug, toep, ws, wc, al, d_lanes)


def _s5_weights(a_re, a_im, log_dt, b_re, b_im, c_re, c_im):
    L = SSM_CHUNK
    n_g, n_p = a_re.shape[1], a_re.shape[2]
    gc = b_re.shape[-1]
    dt = jnp.exp(log_dt)[..., None]
    mag = jnp.exp(a_re * dt)
    ab_re, ab_im = mag * jnp.cos(a_im * dt), mag * jnp.sin(a_im * dt)
    den = a_re * a_re + a_im * a_im
    x_re, x_im = ab_re - 1.0, ab_im
    f_re = ((x_re * a_re + x_im * a_im) / den)[..., None]
    f_im = ((x_im * a_re - x_re * a_im) / den)[..., None]
    bb_re = f_re * b_re - f_im * b_im
    bb_im = f_re * b_im + f_im * b_re
    pw_re, pw_im = [jnp.ones_like(ab_re)], [jnp.zeros_like(ab_im)]
    for _ in range(L):
        pr, pi = pw_re[-1], pw_im[-1]
        pw_re.append(pr * ab_re - pi * ab_im)
        pw_im.append(pr * ab_im + pi * ab_re)
    pw_re, pw_im = jnp.stack(pw_re), jnp.stack(pw_im)
    ca_re = c_re[None] * pw_re[:, :, :, None, :] - c_im[None] * pw_im[:, :, :, None, :]
    ca_im = c_re[None] * pw_im[:, :, :, None, :] + c_im[None] * pw_re[:, :, :, None, :]
    taps = (jnp.einsum('kdgcp,dgpe->kdgce', ca_re[:L], bb_re, precision=lax.Precision.HIGHEST)
            - jnp.einsum('kdgcp,dgpe->kdgce', ca_im[:L], bb_im, precision=lax.Precision.HIGHEST))
    by_lag = jnp.concatenate([taps[L - 1:0:-1, 1], taps[:1, 0] + taps[:1, 1], taps[1:, 0]], axis=0)
    toep = jnp.stack([by_lag[L - 1 - s:2 * L - 1 - s] for s in range(L)], axis=0)
    n_j = n_g // 2
    eye = jnp.eye(2, dtype=F32)
    toep = jnp.einsum('stjgce,gh->jsgethc', toep.reshape(L, L, n_j, 2, gc, gc), eye)
    toep = toep.reshape(n_j, L * 2 * gc, L * 2 * gc)

    def inject(pr, pi, d):
        re = pr[:, :, :, None] * bb_re[d][None] - pi[:, :, :, None] * bb_im[d][None]
        im = pr[:, :, :, None] * bb_im[d][None] + pi[:, :, :, None] * bb_re[d][None]
        tr = lambda w: jnp.transpose(w, (1, 0, 3, 2)).reshape(n_g, L * gc, n_p)
        return tr(re), tr(im)

    inj = inject(pw_re[L - 1::-1, 0][:L], pw_im[L - 1::-1, 0][:L], 0) + inject(pw_re[:L, 1], pw_im[:L, 1], 1)
    def read(cr, ci):
        tr = lambda w: jnp.transpose(w, (1, 3, 0, 2)).reshape(n_g, n_p, L * gc)
        return tr(cr), tr(-ci)

    rd = read(ca_re[1:L + 1, 0], ca_im[1:L + 1, 0]) + read(ca_re[L:0:-1, 1], ca_im[L:0:-1, 1])

    ws = jnp.stack([w.reshape(n_j, 2, L, gc, n_p) for w in inj], axis=0)
    ws = jnp.einsum('qjgsep,gh->jsgeqhp', ws, eye).reshape(n_j, L * 2 * gc, 4 * 2 * n_p)
    wc = jnp.stack([w.reshape(n_j, 2, n_p, L, gc) for w in rd], axis=0)
    wc = jnp.einsum('qjgptc,gh->jqgpthc', wc, eye).reshape(n_j, 4 * 2 * n_p, L * 2 * gc)
    al = jnp.stack([pw_re[L, 0], pw_im[L, 0], pw_re[L, 1], pw_im[L, 1]], axis=1)
    al = jnp.transpose(al.reshape(n_g // 2, 2, 4, n_p), (0, 2, 1, 3)).reshape(n_g // 2, 4, 2 * n_p)
    al = jnp.concatenate([al, jnp.zeros_like(al)], axis=1)
    return toep.astype(BF16), ws.astype(BF16), wc.astype(BF16), al


def _ssm_out_kernel(y_ref, x_ref, mod_ref, g_ref, wv_ref, wg_ref, o_ref, slab_scr, act_scr, out_scr, *, mi, gi):
    cix = pl.program_id(1)
    n_c = out_scr.shape[0]
    tn = out_scr.shape[2]

    @pl.when(cix == 0)
    def _():
        n_slab, tm, lanes = slab_scr.shape
        pair = GROUPS_PER_STEP * SSM_GROUP_CH
        per_slab = lanes // pair
        for t in range(SSM_CHUNK):
            for k in range(n_slab):
                slab_scr[k, pl.ds(t, tm // SSM_CHUNK, stride=SSM_CHUNK), :] = jnp.concatenate(
                    [y_ref[k * per_slab + jj, :, t * pair:(t + 1) * pair] for jj in range(per_slab)], axis=-1)
        for k in range(n_slab):
            act_scr[:, k * lanes:(k + 1) * lanes] = _gelu_tanh(slab_scr[k]).astype(BF16)

    a = act_scr[...]
    val = jnp.dot(a, wv_ref[...], preferred_element_type=F32)
    gate = jnp.dot(a, wg_ref[...], preferred_element_type=F32)
    out_scr[cix] = val * _sigmoid(gate)

    @pl.when(cix == n_c - 1)
    def _():
        ssq = jnp.zeros((out_scr.shape[1], 1), F32)
        for cc in range(n_c):
            blk = out_scr[cc]
            ssq = ssq + jnp.sum(blk * blk, axis=-1, keepdims=True)
        inv = lax.rsqrt(ssq / (n_c * tn) + EPS)
        for cc in range(n_c):
            sl = slice(cc * tn, (cc + 1) * tn)
            o_ref[:, sl] = x_ref[:, sl] + mod_ref[0, mi:mi + 1, sl] * (out_scr[cc] * inv * g_ref[gi:gi + 1, sl])


def _ssm_out(yg, xs, mod, g, w_glu, *, mi, gi, n_tiles, seg_of):
    d = xs.shape[1]
    tm, tn = TOKEN_TILE, GLU_TILE
    n_c = d // tn
    n_pairs, _, width = yg.shape
    return pl.pallas_call(
        functools.partial(_ssm_out_kernel, mi=mi, gi=gi),
        out_shape=jax.ShapeDtypeStruct((n_tiles * tm, d), F32),
        grid=(n_tiles, n_c),
        in_specs=[
            pl.BlockSpec((n_pairs, tm // SSM_CHUNK, width), lambda i, cc: (0, i, 0)),
            pl.BlockSpec((tm, d), lambda i, cc: (i, 0)),
            pl.BlockSpec((1, MOD_ROWS, d), lambda i, cc: (seg_of(i), 0, 0)),
            pl.BlockSpec((8, d), lambda i, cc: (0, 0)),
            pl.BlockSpec((d, tn), lambda i, cc: (0, cc)),
            pl.BlockSpec((d, tn), lambda i, cc: (0, n_c + cc)),
        ],
        out_specs=pl.BlockSpec((tm, d), lambda i, cc: (i, 0)),
        scratch_shapes=[pltpu.VMEM((d // LANES, tm, LANES), F32), pltpu.VMEM((tm, d), BF16),
                        pltpu.VMEM((n_c, tm, tn), F32)],
        compiler_params=_params("parallel", "arbitrary"),
        name="ssm_glu_out",
    )(yg, xs, mod, g, w_glu, w_glu)


def _rope_tables(seq_len, extra_rows):
    rows = seq_len // GRID_W
    row = jnp.repeat(jnp.arange(rows, dtype=F32), GRID_W)
    col = jnp.tile(jnp.arange(GRID_W, dtype=F32), rows)
    axis_dim = HEAD_DIM // 2
    inv_freq = ROPE_BASE ** (-jnp.arange(0, axis_dim, 2, dtype=F32) / axis_dim)
    ang_r, ang_c = row[:, None] * inv_freq, col[:, None] * inv_freq
    cos = jnp.concatenate([jnp.cos(ang_r)] * 2 + [jnp.cos(ang_c)] * 2, axis=-1)
    sin = jnp.concatenate([-jnp.sin(ang_r), jnp.sin(ang_r), -jnp.sin(ang_c), jnp.sin(ang_c)], axis=-1)
    cos = jnp.concatenate([cos, jnp.ones((extra_rows, HEAD_DIM), F32)], axis=0)
    sin = jnp.concatenate([sin, jnp.zeros((extra_rows, HEAD_DIM), F32)], axis=0)
    return cos, sin


def kernel(x, c, ctx, c_ctx, ada_w, ada_b, norm_g, ffn_w_in, ffn_w_out, attn_w_in, attn_w_out, attn_sink,
           ssm_w_in, ssm_a_re, ssm_a_im, ssm_log_dt, ssm_b_re, ssm_b_im, ssm_c_re, ssm_c_im, ssm_d, ssm_w_glu):
    batch, seq_len, d = x.shape
    c_len = ctx.shape[1]
    depth = ada_w.shape[0]
    tm = TOKEN_TILE
    n_lat_rows, n_ctx_rows = batch * seq_len, batch * c_len
    assert seq_len % tm == 0 and n_ctx_rows % tm == 0 and n_lat_rows % c_len == 0
    assert seq_len % ATT_BLOCK == 0 and c_len % ATT_BLOCK == 0 and seq_len % GRID_W == 0
    tiles_per_batch = seq_len // tm
    lat_tiles = n_lat_rows // tm
    all_tiles = lat_tiles + n_ctx_rows // tm

    def seg_of(i):
        return jnp.minimum(i // tiles_per_batch, batch)

    def pos_of(i):
        return jnp.where(i < lat_tiles, i % tiles_per_batch, tiles_per_batch + (i - lat_tiles))

    xs = jnp.concatenate([x.reshape(n_lat_rows, d), ctx.reshape(n_ctx_rows, d)], axis=0)

    n_cond = batch + 1
    cvec = jnp.concatenate([c, c_ctx[None], jnp.zeros((8 - n_cond, d), F32)], axis=0)
    mods = _modulation(cvec.T, ada_w, ada_b, n_cond)
    mods = mods[:, :n_cond].reshape(depth, n_cond, N_MOD, d)
    mods = jnp.pad(mods, ((0, 0), (0, 0), (0, MOD_ROWS - N_MOD), (0, 0)))
    gains = jnp.pad(norm_g, ((0, 0), (0, 8 - norm_g.shape[1]), (0, 0)))

    cos, sin = _rope_tables(seq_len, n_ctx_rows)

    for i in range(depth):
        need_ctx = i < depth - 1
        mod, g = mods[i], gains[i]
        j = i // 2
        xs = _ffn(xs, mod, g, ffn_w_in[i, 0].astype(BF16), ffn_w_out[i, 0].astype(BF16),
                  mi=0, gi=0, n_tiles=all_tiles, seg_of=seg_of)
        post_tiles = all_tiles if need_ctx else lat_tiles
        if i % 2 == 0:
            q, k, v = _qkv(xs, mod, g, attn_w_in[j].astype(BF16), cos, sin,
                           mi=3, gi=2, n_tiles=all_tiles, seg_of=seg_of, pos_of=pos_of)
            o = _attention(q, k, v, attn_sink[j], batch=batch, seq_len=seq_len, c_len=c_len)
            xs = _proj_out(o, xs, mod, g, attn_w_out[j].astype(BF16),
                           mi=5, gi=3, n_tiles=post_tiles, seg_of=seg_of)
        else:
            if need_ctx:
                raise NotImplementedError("context output of the S5 mixer is only needed when another layer follows")
            ug = _proj_in(xs, mod, g, ssm_w_in[j].astype(BF16), mi=3, gi=2, n_tiles=all_tiles, seg_of=seg_of)
            L = SSM_CHUNK
            pair = GROUPS_PER_STEP * SSM_GROUP_CH
            toep, ws, wc, al = _s5_weights(ssm_a_re[j], ssm_a_im[j], ssm_log_dt[j], ssm_b_re[j], ssm_b_im[j],
                                           ssm_c_re[j], ssm_c_im[j])
            d_lanes = jnp.tile(ssm_d[j].reshape(d // pair, 1, pair), (1, 1, L))
            yg = _s5_core(ug, toep, ws, wc, al, d_lanes, batch=batch, n_lat=seq_len // L, n_ctx=c_len // L)
            xs = _ssm_out(yg, xs, mod, g, ssm_w_glu[j].astype(BF16),
                          mi=5, gi=3, n_tiles=post_tiles, seg_of=seg_of)
        xs = _ffn(xs, mod, g, ffn_w_in[i, 1].astype(BF16), ffn_w_out[i, 1].astype(BF16),
                  mi=6, gi=4, n_tiles=post_tiles, seg_of=seg_of)
    return xs[:n_lat_rows].reshape(batch, seq_len, d)
```

```python
import functools
import math

import jax
import jax.numpy as jnp
from jax import lax
from jax.experimental import pallas as pl
from jax.experimental.pallas import tpu as pltpu

EPS = 1e-6
NEG_INF = -1e30
N_MOD = 9
GRID_W = 64
N_HEADS = 16
N_KV_HEADS = 4
HEAD_DIM = 128
GQA_GROUP = N_HEADS // N_KV_HEADS
ATT_BLOCK = 128
ROPE_BASE = 10000.0
SSM_GROUP_CH = 16
SSM_STATE = 64
SSM_CHUNK = 16
GROUPS_PER_STEP = 2

TOKEN_TILE = 512
FF_TILE = 512
GLU_TILE = 512
GLU_TOKEN_TILE = 256
MOD_TILE = 1024
MOD_ROWS = 16
VMEM_LIMIT = 56 * 1024 * 1024
LANES = 128

BF16 = jnp.bfloat16
F32 = jnp.float32


def _params(*sem):
    return pltpu.CompilerParams(dimension_semantics=sem, vmem_limit_bytes=VMEM_LIMIT)


def _rms(t, g):
    return t * lax.rsqrt(jnp.mean(t * t, axis=-1, keepdims=True) + EPS) * g


def _sigmoid(t):
    return 1.0 / (1.0 + jnp.exp(-t))


def _gelu_tanh(t):
    return 0.5 * t * (1.0 + jnp.tanh(math.sqrt(2.0 / math.pi) * (t + 0.044715 * (t * t * t))))


def _mod_kernel(ct_ref, w_ref, b_ref, o_ref, *, n_rows):
    w = w_ref[0]
    for r in range(n_rows):
        col = ct_ref[:, r:r + 1]
        col = col * _sigmoid(col)
        o_ref[0, r:r + 1, :] = jnp.sum(w * col, axis=0, keepdims=True) + b_ref[0]
    o_ref[0, n_rows:, :] = jnp.zeros((o_ref.shape[1] - n_rows, o_ref.shape[2]), F32)


def _modulation(cvec_t, ada_w, ada_b, n_rows):
    depth, d, nd = ada_w.shape
    tn = MOD_TILE
    return pl.pallas_call(
        functools.partial(_mod_kernel, n_rows=n_rows),
        out_shape=jax.ShapeDtypeStruct((depth, 8, nd), F32),
        grid=(depth, nd // tn),
        in_specs=[
            pl.BlockSpec((d, 8), lambda l, j: (0, 0)),
            pl.BlockSpec((1, d, tn), lambda l, j: (l, 0, j)),
            pl.BlockSpec((1, 1, tn), lambda l, j: (l, 0, j)),
        ],
        out_specs=pl.BlockSpec((1, 8, tn), lambda l, j: (l, 0, j)),
        compiler_params=_params("arbitrary", "arbitrary"),
        name="adaln_mod",
    )(cvec_t, ada_w, ada_b.reshape(depth, 1, nd))


def _ffn_kernel(x_ref, tail_ref, mod_ref, g_ref, wg_ref, wu_ref, wo_ref, o_ref, h_scr, acc_scr, *, mi, gi, head_tiles):
    def read_x():
        if head_tiles is None:
            return x_ref[...]
        return jnp.where(pl.program_id(0) < head_tiles, x_ref[...], tail_ref[...])

    f = pl.program_id(1)

    @pl.when(f == 0)
    def _():
        h = _rms(read_x(), g_ref[gi:gi + 1, :]) * (1.0 + mod_ref[0, mi + 1:mi + 2, :]) + mod_ref[0, mi:mi + 1, :]
        h_scr[...] = h.astype(BF16)
        acc_scr[...] = jnp.zeros_like(acc_scr)

    h = h_scr[...]
    gate = jnp.dot(h, wg_ref[...], preferred_element_type=F32)
    up = jnp.dot(h, wu_ref[...], preferred_element_type=F32)
    act = (gate * _sigmoid(gate) * up).astype(BF16)
    acc_scr[...] += jnp.dot(act, wo_ref[...], preferred_element_type=F32)

    @pl.when(f == pl.num_programs(1) - 1)
    def _():
        y = _rms(acc_scr[...], g_ref[gi + 1:gi + 2, :])
        o_ref[...] = read_x() + 0.5 * mod_ref[0, mi + 2:mi + 3, :] * y


def _ffn(xs, mod, g, w_in, w_out, which, *, mi, gi, n_tiles, seg_of, tail=None):
    d = xs.shape[1]
    ff = w_out.shape[2]
    tm, tf = TOKEN_TILE, FF_TILE
    nf = ff // tf
    layer, half = which
    if tail is None:
        head_tiles = None
        tokens = [xs, xs]
        token_specs = [pl.BlockSpec((tm, d), lambda i, f: (i, 0)), pl.BlockSpec((8, d), lambda i, f: (0, 0))]
    else:
        head_tiles = xs.shape[0] // tm
        tokens = [xs, tail]
        token_specs = [pl.BlockSpec((tm, d), lambda i, f: (jnp.minimum(i, head_tiles - 1), 0)),
                       pl.BlockSpec((tm, d), lambda i, f: (jnp.maximum(i - head_tiles, 0), 0),
                                    pipeline_mode=pl.Buffered(1))]
    return pl.pallas_call(
        functools.partial(_ffn_kernel, mi=mi, gi=gi, head_tiles=head_tiles),
        out_shape=jax.ShapeDtypeStruct((n_tiles * tm, d), F32),
        grid=(n_tiles, nf),
        in_specs=token_specs + [
            pl.BlockSpec((1, MOD_ROWS, d), lambda i, f: (seg_of(i), 0, 0)),
            pl.BlockSpec((8, d), lambda i, f: (0, 0)),
            pl.BlockSpec((None, None, d, tf), lambda i, f: (layer, half, 0, f)),
            pl.BlockSpec((None, None, d, tf), lambda i, f: (layer, half, 0, nf + f)),
            pl.BlockSpec((None, None, tf, d), lambda i, f: (layer, half, f, 0)),
        ],
        out_specs=pl.BlockSpec((tm, d), lambda i, f: (i, 0)),
        scratch_shapes=[pltpu.VMEM((tm, d), BF16), pltpu.VMEM((tm, d), F32)],
        compiler_params=_params("parallel", "arbitrary"),
        name="ffn_halfstep",
    )(*tokens, mod, g, w_in, w_in, w_out)


def _rope(t, cos, sin, even_quarter):
    partner = jnp.where(even_quarter, pltpu.roll(t, HEAD_DIM - 32, 1), pltpu.roll(t, 32, 1))
    return t * cos + partner * sin


def _qkv_kernel(x_ref, mod_ref, g_ref, w_ref, cos_ref, sin_ref, q_ref, k_ref, v_ref, *, mi, gi):
    h = _rms(x_ref[...], g_ref[gi:gi + 1, :]) * (1.0 + mod_ref[0, mi + 1:mi + 2, :]) + mod_ref[0, mi:mi + 1, :]
    qkv = jnp.dot(h.astype(BF16), w_ref[...], preferred_element_type=F32)
    cos, sin = cos_ref[...], sin_ref[...]
    lane = lax.broadcasted_iota(jnp.int32, cos.shape, 1)
    even_quarter = (lane & 32) == 0
    scale = HEAD_DIM ** -0.5
    nq, nk = N_HEADS * HEAD_DIM, N_KV_HEADS * HEAD_DIM
    for hh in range(N_HEADS):
        sl = slice(hh * HEAD_DIM, (hh + 1) * HEAD_DIM)
        q_ref[:, sl] = (_rope(qkv[:, sl], cos, sin, even_quarter) * scale).astype(BF16)
    for hh in range(N_KV_HEADS):
        sl = slice(hh * HEAD_DIM, (hh + 1) * HEAD_DIM)
        src = slice(nq + hh * HEAD_DIM, nq + (hh + 1) * HEAD_DIM)
        k_ref[:, sl] = _rope(qkv[:, src], cos, sin, even_quarter).astype(BF16)
    v_ref[...] = qkv[:, nq + nk:].astype(BF16)


def _qkv(xs, mod, g, w, cos, sin, *, mi, gi, n_tiles, seg_of, pos_of):
    d = xs.shape[1]
    tm = TOKEN_TILE
    nq, nk = N_HEADS * HEAD_DIM, N_KV_HEADS * HEAD_DIM
    rows = n_tiles * tm
    return pl.pallas_call(
        functools.partial(_qkv_kernel, mi=mi, gi=gi),
        out_shape=(jax.ShapeDtypeStruct((rows, nq), BF16), jax.ShapeDtypeStruct((rows, nk), BF16),
                   jax.ShapeDtypeStruct((rows, nk), BF16)),
        grid=(n_tiles,),
        in_specs=[
            pl.BlockSpec((tm, d), lambda i: (i, 0)),
            pl.BlockSpec((1, MOD_ROWS, d), lambda i: (seg_of(i), 0, 0)),
            pl.BlockSpec((8, d), lambda i: (0, 0)),
            pl.BlockSpec((d, nq + 2 * nk), lambda i: (0, 0), pipeline_mode=pl.Buffered(1)),
            pl.BlockSpec((tm, HEAD_DIM), lambda i: (pos_of(i), 0)),
            pl.BlockSpec((tm, HEAD_DIM), lambda i: (pos_of(i), 0)),
        ],
        out_specs=(pl.BlockSpec((tm, nq), lambda i: (i, 0)), pl.BlockSpec((tm, nk), lambda i: (i, 0)),
                   pl.BlockSpec((tm, nk), lambda i: (i, 0))),
        compiler_params=_params("parallel"),
        name="attn_qkv_rope",
    )(xs, mod, g, w, cos, sin)


def _proj_in_kernel(x_ref, mod_ref, g_ref, w_ref, o_ref, slab_scr, *, mi, gi):
    h = _rms(x_ref[...], g_ref[gi:gi + 1, :]) * (1.0 + mod_ref[0, mi + 1:mi + 2, :]) + mod_ref[0, mi:mi + 1, :]
    u = jnp.dot(h.astype(BF16), w_ref[...], preferred_element_type=F32)
    n_slab, tm, lanes = slab_scr.shape
    pair = GROUPS_PER_STEP * SSM_GROUP_CH
    per_slab = lanes // pair
    for k in range(n_slab):
        slab_scr[k] = u[:, k * lanes:(k + 1) * lanes]
    for t in range(SSM_CHUNK):
        for k in range(n_slab):
            rows = slab_scr[k, pl.ds(t, tm // SSM_CHUNK, stride=SSM_CHUNK), :]
            for jj in range(per_slab):
                o_ref[k * per_slab + jj, :, t * pair:(t + 1) * pair] = rows[:, jj * pair:(jj + 1) * pair].astype(o_ref.dtype)


def _proj_in(xs, mod, g, w, *, mi, gi, n_tiles, seg_of):
    d = xs.shape[1]
    tm = TOKEN_TILE
    pair = GROUPS_PER_STEP * SSM_GROUP_CH
    n_pairs = w.shape[1] // pair
    return pl.pallas_call(
        functools.partial(_proj_in_kernel, mi=mi, gi=gi),
        out_shape=jax.ShapeDtypeStruct((n_pairs, n_tiles * tm // SSM_CHUNK, SSM_CHUNK * pair), BF16),
        grid=(n_tiles,),
        in_specs=[
            pl.BlockSpec((tm, d), lambda i: (i, 0)),
            pl.BlockSpec((1, MOD_ROWS, d), lambda i: (seg_of(i), 0, 0)),
            pl.BlockSpec((8, d), lambda i: (0, 0)),
            pl.BlockSpec(w.shape, lambda i: (0, 0), pipeline_mode=pl.Buffered(1)),
        ],
        out_specs=pl.BlockSpec((n_pairs, tm // SSM_CHUNK, SSM_CHUNK * pair), lambda i: (0, i, 0)),
        scratch_shapes=[pltpu.VMEM((w.shape[1] // LANES, tm, LANES), F32)],
        compiler_params=_params("parallel"),
        name="ssm_in_proj",
    )(xs, mod, g, w)


def _proj_out_kernel(a_ref, x_ref, mod_ref, g_ref, w_ref, o_ref, *, mi, gi):
    out = jnp.dot(a_ref[...], w_ref[...], preferred_element_type=F32)
    o_ref[...] = x_ref[...] + mod_ref[0, mi:mi + 1, :] * _rms(out, g_ref[gi:gi + 1, :])


def _proj_out(a, xs, mod, g, w, *, mi, gi, n_tiles, seg_of):
    d = xs.shape[1]
    tm = TOKEN_TILE
    return pl.pallas_call(
        functools.partial(_proj_out_kernel, mi=mi, gi=gi),
        out_shape=jax.ShapeDtypeStruct((n_tiles * tm, d), F32),
        grid=(n_tiles,),
        in_specs=[
            pl.BlockSpec((tm, a.shape[1]), lambda i: (i, 0)),
            pl.BlockSpec((tm, d), lambda i: (i, 0)),
            pl.BlockSpec((1, MOD_ROWS, d), lambda i: (seg_of(i), 0, 0)),
            pl.BlockSpec((8, d), lambda i: (0, 0)),
            pl.BlockSpec(w.shape, lambda i: (0, 0), pipeline_mode=pl.Buffered(1)),
        ],
        out_specs=pl.BlockSpec((tm, d), lambda i: (i, 0)),
        compiler_params=_params("parallel"),
        name="attn_out_proj",
    )(a, xs, mod, g, w)


def _attn_kernel(sink_ref, q_ref, kp_ref, ko_ref, kn_ref, vp_ref, vo_ref, vn_ref, kc_ref, vc_ref, o_ref,
                 *, n_blk, seq_len):
    n = pl.program_id(1)
    blk = ATT_BLOCK
    c_len = kc_ref.shape[0]
    n_keys = 3 * blk + c_len
    i = lax.broadcasted_iota(jnp.int32, (GQA_GROUP * blk, n_keys), 0) & (blk - 1)
    j = lax.broadcasted_iota(jnp.int32, (GQA_GROUP * blk, n_keys), 1)
    rel = j - i
    kpos = (n - 1) * blk + j
    kpos_end = jnp.where(n < n_blk, seq_len, 0)
    band = (rel >= 0) & (rel <= 2 * blk) & (kpos >= 0) & (kpos < kpos_end)
    mask = band | (j >= 3 * blk)
    row = lax.broadcasted_iota(jnp.int32, (GQA_GROUP * blk, 1), 0)
    for h in range(N_KV_HEADS):
        hs = slice(h * HEAD_DIM, (h + 1) * HEAD_DIM)
        qs = jnp.concatenate(
            [q_ref[:, (h * GQA_GROUP + gq) * HEAD_DIM:(h * GQA_GROUP + gq + 1) * HEAD_DIM] for gq in range(GQA_GROUP)],
            axis=0)
        kb = jnp.concatenate([kp_ref[:, hs], ko_ref[:, hs], kn_ref[:, hs], kc_ref[:, hs]], axis=0)
        vb = jnp.concatenate([vp_ref[:, hs], vo_ref[:, hs], vn_ref[:, hs], vc_ref[:, hs]], axis=0)
        s = lax.dot_general(qs, kb, (((1,), (1,)), ((), ())), preferred_element_type=F32)
        s = jnp.where(mask, s, NEG_INF)
        sink = jnp.full((GQA_GROUP * blk, 1), sink_ref[h * GQA_GROUP], F32)
        for gq in range(1, GQA_GROUP):
            sink = jnp.where(row >= gq * blk, sink_ref[h * GQA_GROUP + gq], sink)
        m = jnp.maximum(jnp.max(s, axis=-1, keepdims=True), sink)
        p = jnp.exp(s - m)
        denom = jnp.sum(p, axis=-1, keepdims=True) + jnp.exp(sink - m)
        o = jnp.dot(p.astype(BF16), vb, preferred_element_type=F32) / denom
        for gq in range(GQA_GROUP):
            col = (h * GQA_GROUP + gq) * HEAD_DIM
            o_ref[:, col:col + HEAD_DIM] = o[gq * blk:(gq + 1) * blk, :].astype(o_ref.dtype)


def _attention(q, k, v, sink, *, batch, seq_len, c_len):
    blk = ATT_BLOCK
    n_blk, c_blk = seq_len // blk, c_len // blk
    nq, nk = q.shape[1], k.shape[1]
    lat_blocks = batch * n_blk
    ctx_block0 = batch * seq_len // c_len

    def q_idx(b, n):
        return jnp.where(n < n_blk, b * n_blk + n, lat_blocks + b * c_blk + (n - n_blk))

    def band_idx(off):
        def idx(b, n, s):
            nn = jnp.clip(n + off, 0, n_blk - 1)
            return (b * n_blk + nn, 0)
        return idx

    kv_spec = [pl.BlockSpec((blk, nk), band_idx(off)) for off in (-1, 0, 1)]
    ctx_spec = pl.BlockSpec((c_len, nk), lambda b, n, s: (ctx_block0 + b, 0))
    grid_spec = pltpu.PrefetchScalarGridSpec(
        num_scalar_prefetch=1,
        grid=(batch, n_blk + c_blk),
        in_specs=[pl.BlockSpec((blk, nq), lambda b, n, s: (q_idx(b, n), 0))] + kv_spec + kv_spec + [ctx_spec, ctx_spec],
        out_specs=pl.BlockSpec((blk, nq), lambda b, n, s: (q_idx(b, n), 0)),
    )
    return pl.pallas_call(
        functools.partial(_attn_kernel, n_blk=n_blk, seq_len=seq_len),
        out_shape=jax.ShapeDtypeStruct(q.shape, BF16),
        grid_spec=grid_spec,
        compiler_params=_params("parallel", "arbitrary"),
        name="window_gqa",
    )(sink, q, k, k, k, v, v, v, k, v)


def _s5_kernel(u_ref, t_ref, ws_ref, wc_ref, al_ref, d_ref, y_ref, s_scr, h_scr, *, batch, n_lat, n_ctx):
    half = GROUPS_PER_STEP * SSM_STATE
    u = u_ref[0]
    s_scr[...] = jnp.dot(u, ws_ref[0], preferred_element_type=F32)
    sub = 8
    a_f = tuple(jnp.broadcast_to(al_ref[0, r:r + 1, :], (sub, half)) for r in (0, 1))
    a_b = tuple(jnp.broadcast_to(al_ref[0, r:r + 1, :], (sub, half)) for r in (2, 3))
    ctx0 = batch * n_lat
    zero = jnp.zeros((sub, half), F32)
    sub_row = lax.broadcasted_iota(jnp.int32, (sub, half), 0)

    def scan_block(h, a, base, col, order, keep):
        h_re, h_im = h
        a_re, a_im = a
        base = pl.multiple_of(base, sub)
        s_re = s_scr[pl.ds(base, sub), col:col + half]
        s_im = s_scr[pl.ds(base, sub), col + half:col + 2 * half]
        e_re, e_im = zero, zero
        for r in order:
            if keep:
                e_re = jnp.where(sub_row == r, h_re, e_re)
                e_im = jnp.where(sub_row == r, h_im, e_im)
            r_re = jnp.broadcast_to(s_re[r:r + 1, :], (sub, half))
            r_im = jnp.broadcast_to(s_im[r:r + 1, :], (sub, half))
            h_re, h_im = a_re * h_re - a_im * h_im + r_re, a_re * h_im + a_im * h_re + r_im
        if keep:
            h_scr[pl.ds(base, sub), col:col + half] = e_re
            h_scr[pl.ds(base, sub), col + half:col + 2 * half] = e_im
        return h_re, h_im

    fwd_order, bwd_order = tuple(range(sub)), tuple(range(sub - 1, -1, -1))

    def make_step(first_row, n_rows, keep):
        def step(m, carry):
            out = []
            for b in range(batch):
                hf, hb = carry[2 * b], carry[2 * b + 1]
                lo = first_row + b * n_rows
                hf = scan_block(hf, a_f, lo + m * sub, 0, fwd_order, keep)
                hb = scan_block(hb, a_b, lo + n_rows - sub - m * sub, 2 * half, bwd_order, keep)
                out += [hf, hb]
            return tuple(out)
        return step

    carry = lax.fori_loop(0, n_ctx // sub, make_step(ctx0, n_ctx, False), ((zero, zero),) * (2 * batch))
    lax.fori_loop(0, n_lat // sub, make_step(0, n_lat, True), carry)

    rows = batch * n_lat
    u_lat = u[:rows]
    y_ref[0] = (jnp.dot(u_lat, t_ref[0], preferred_element_type=F32)
                + jnp.dot(h_scr[...].astype(BF16), wc_ref[0], preferred_element_type=F32)
                + d_ref[0] * u_lat.astype(F32))


def _s5_core(ug, toep, ws, wc, al, d_lanes, *, batch, n_lat, n_ctx):
    n_steps, chunks, width = ug.shape
    assert n_lat % 8 == 0 and n_ctx % 8 == 0
    rows = batch * n_lat
    return pl.pallas_call(
        functools.partial(_s5_kernel, batch=batch, n_lat=n_lat, n_ctx=n_ctx),
        out_shape=jax.ShapeDtypeStruct((n_steps, rows, width), F32),
        grid=(n_steps,),
        in_specs=[
            pl.BlockSpec((1, chunks, width), lambda j: (j, 0, 0)),
            pl.BlockSpec((1,) + toep.shape[1:], lambda j: (j, 0, 0)),
            pl.BlockSpec((1,) + ws.shape[1:], lambda j: (j, 0, 0)),
            pl.BlockSpec((1,) + wc.shape[1:], lambda j: (j, 0, 0)),
            pl.BlockSpec((1,) + al.shape[1:], lambda j: (j, 0, 0)),
            pl.BlockSpec((1, 1, width), lambda j: (j, 0, 0)),
        ],
        out_specs=pl.BlockSpec((1, rows, width), lambda j: (j, 0, 0)),
        scratch_shapes=[pltpu.VMEM((chunks, 4 * GROUPS_PER_STEP * SSM_STATE), F32),
                        pltpu.VMEM((rows, 4 * GROUPS_PER_STEP * SSM_STATE), F32)],
        compiler_params=_params("parallel"),
        name="s5_chunked",
    )(ug, toep, ws, wc, al, d_lanes)


def _s5_weights(a_re, a_im, log_dt, b_re, b_im, c_re, c_im):
    L = SSM_CHUNK
    n_g, n_p = a_re.shape[1], a_re.shape[2]
    gc = b_re.shape[-1]
    dt = jnp.exp(log_dt)[..., None]
    mag = jnp.exp(a_re * dt)
    ab_re, ab_im = mag * jnp.cos(a_im * dt), mag * jnp.sin(a_im * dt)
    den = a_re * a_re + a_im * a_im
    x_re, x_im = ab_re - 1.0, ab_im
    f_re = ((x_re * a_re + x_im * a_im) / den)[..., None]
    f_im = ((x_im * a_re - x_re * a_im) / den)[..., None]
    bb_re = f_re * b_re - f_im * b_im
    bb_im = f_re * b_im + f_im * b_re
    pw_re, pw_im = [jnp.ones_like(ab_re)], [jnp.zeros_like(ab_im)]
    for _ in range(L):
        pr, pi = pw_re[-1], pw_im[-1]
        pw_re.append(pr * ab_re - pi * ab_im)
        pw_im.append(pr * ab_im + pi * ab_re)
    pw_re, pw_im = jnp.stack(pw_re), jnp.stack(pw_im)
    ca_re = c_re[None] * pw_re[:, :, :, None, :] - c_im[None] * pw_im[:, :, :, None, :]
    ca_im = c_re[None] * pw_im[:, :, :, None, :] + c_im[None] * pw_re[:, :, :, None, :]
    taps = (jnp.einsum('kdgcp,dgpe->kdgce', ca_re[:L], bb_re, precision=lax.Precision.HIGHEST)
            - jnp.einsum('kdgcp,dgpe->kdgce', ca_im[:L], bb_im, precision=lax.Precision.HIGHEST))
    by_lag = jnp.concatenate([taps[L - 1:0:-1, 1], taps[:1, 0] + taps[:1, 1], taps[1:, 0]], axis=0)
    toep = jnp.stack([by_lag[L - 1 - s:2 * L - 1 - s] for s in range(L)], axis=0)
    n_j = n_g // 2
    eye = jnp.eye(2, dtype=F32)
    toep = jnp.einsum('stjgce,gh->jsgethc', toep.reshape(L, L, n_j, 2, gc, gc), eye)
    toep = toep.reshape(n_j, L * 2 * gc, L * 2 * gc)

    def inject(pr, pi, d):
        re = pr[:, :, :, None] * bb_re[d][None] - pi[:, :, :, None] * bb_im[d][None]
        im = pr[:, :, :, None] * bb_im[d][None] + pi[:, :, :, None] * bb_re[d][None]
        tr = lambda w: jnp.transpose(w, (1, 0, 3, 2)).reshape(n_g, L * gc, n_p)
        return tr(re), tr(im)

    inj = inject(pw_re[L - 1::-1, 0][:L], pw_im[L - 1::-1, 0][:L], 0) + inject(pw_re[:L, 1], pw_im[:L, 1], 1)
    def read(cr, ci):
        tr = lambda w: jnp.transpose(w, (1, 3, 0, 2)).reshape(n_g, n_p, L * gc)
        return tr(cr), tr(-ci)

    rd = read(ca_re[1:L + 1, 0], ca_im[1:L + 1, 0]) + read(ca_re[L:0:-1, 1], ca_im[L:0:-1, 1])

    ws = jnp.stack([w.reshape(n_j, 2, L, gc, n_p) for w in inj], axis=0)
    ws = jnp.einsum('qjgsep,gh->jsgeqhp', ws, eye).reshape(n_j, L * 2 * gc, 4 * 2 * n_p)
    wc = jnp.stack([w.reshape(n_j, 2, n_p, L, gc) for w in rd], axis=0)
    wc = jnp.einsum('qjgptc,gh->jqgpthc', wc, eye).reshape(n_j, 4 * 2 * n_p, L * 2 * gc)
    al = jnp.stack([pw_re[L, 0], pw_im[L, 0], pw_re[L, 1], pw_im[L, 1]], axis=1)
    al = jnp.transpose(al.reshape(n_g // 2, 2, 4, n_p), (0, 2, 1, 3)).reshape(n_g // 2, 4, 2 * n_p)
    al = jnp.concatenate([al, jnp.zeros_like(al)], axis=1)
    return toep.astype(BF16), ws.astype(BF16), wc.astype(BF16), al


def _ssm_out_kernel(y_ref, x_ref, mod_ref, g_ref, w_ref, o_ref, slab_scr, act_scr, out_scr, *, mi, gi):
    n_slab, tm, lanes = slab_scr.shape
    pair = GROUPS_PER_STEP * SSM_GROUP_CH
    per_slab = lanes // pair
    for t in range(SSM_CHUNK):
        for k in range(n_slab):
            slab_scr[k, pl.ds(t, tm // SSM_CHUNK, stride=SSM_CHUNK), :] = jnp.concatenate(
                [y_ref[k * per_slab + jj, :, t * pair:(t + 1) * pair] for jj in range(per_slab)], axis=-1)
    for k in range(n_slab):
        act_scr[:, k * lanes:(k + 1) * lanes] = _gelu_tanh(slab_scr[k]).astype(BF16)

    n_c, _, tn = out_scr.shape
    d = n_c * tn
    a = act_scr[...]
    ssq = jnp.zeros((tm, 1), F32)
    for cc in range(n_c):
        val = jnp.dot(a, w_ref[:, cc * tn:(cc + 1) * tn], preferred_element_type=F32)
        gate = jnp.dot(a, w_ref[:, d + cc * tn:d + (cc + 1) * tn], preferred_element_type=F32)
        blk = val * _sigmoid(gate)
        out_scr[cc] = blk
        ssq = ssq + jnp.sum(blk * blk, axis=-1, keepdims=True)
    inv = lax.rsqrt(ssq / d + EPS)
    for cc in range(n_c):
        sl = slice(cc * tn, (cc + 1) * tn)
        o_ref[:, sl] = x_ref[:, sl] + mod_ref[0, mi:mi + 1, sl] * (out_scr[cc] * inv * g_ref[gi:gi + 1, sl])


def _ssm_out(yg, xs, mod, g, w_glu, *, mi, gi, n_rows, seg_of_row):
    d = xs.shape[1]
    tm, tn = GLU_TOKEN_TILE, GLU_TILE
    n_c = d // tn
    n_pairs, _, width = yg.shape
    return pl.pallas_call(
        functools.partial(_ssm_out_kernel, mi=mi, gi=gi),
        out_shape=jax.ShapeDtypeStruct((n_rows, d), F32),
        grid=(n_rows // tm,),
        in_specs=[
            pl.BlockSpec((n_pairs, tm // SSM_CHUNK, width), lambda i: (0, i, 0)),
            pl.BlockSpec((tm, d), lambda i: (i, 0)),
            pl.BlockSpec((1, MOD_ROWS, d), lambda i: (seg_of_row(i * tm), 0, 0)),
            pl.BlockSpec((8, d), lambda i: (0, 0)),
            pl.BlockSpec(w_glu.shape, lambda i: (0, 0), pipeline_mode=pl.Buffered(1)),
        ],
        out_specs=pl.BlockSpec((tm, d), lambda i: (i, 0)),
        scratch_shapes=[pltpu.VMEM((d // LANES, tm, LANES), F32), pltpu.VMEM((tm, d), BF16),
                        pltpu.VMEM((n_c, tm, tn), F32)],
        compiler_params=_params("parallel"),
        name="ssm_glu_out",
    )(yg, xs, mod, g, w_glu)


def _rope_tables(seq_len, extra_rows):
    rows = seq_len // GRID_W
    row = jnp.repeat(jnp.arange(rows, dtype=F32), GRID_W)
    col = jnp.tile(jnp.arange(GRID_W, dtype=F32), rows)
    axis_dim = HEAD_DIM // 2
    inv_freq = ROPE_BASE ** (-jnp.arange(0, axis_dim, 2, dtype=F32) / axis_dim)
    ang_r, ang_c = row[:, None] * inv_freq, col[:, None] * inv_freq
    cos = jnp.concatenate([jnp.cos(ang_r)] * 2 + [jnp.cos(ang_c)] * 2, axis=-1)
    sin = jnp.concatenate([-jnp.sin(ang_r), jnp.sin(ang_r), -jnp.sin(ang_c), jnp.sin(ang_c)], axis=-1)
    cos = jnp.concatenate([cos, jnp.ones((extra_rows, HEAD_DIM), F32)], axis=0)
    sin = jnp.concatenate([sin, jnp.zeros((extra_rows, HEAD_DIM), F32)], axis=0)
    return cos, sin


def kernel(x, c, ctx, c_ctx, ada_w, ada_b, norm_g, ffn_w_in, ffn_w_out, attn_w_in, attn_w_out, attn_sink,
           ssm_w_in, ssm_a_re, ssm_a_im, ssm_log_dt, ssm_b_re, ssm_b_im, ssm_c_re, ssm_c_im, ssm_d, ssm_w_glu):
    batch, seq_len, d = x.shape
    c_len = ctx.shape[1]
    depth = ada_w.shape[0]
    tm = TOKEN_TILE
    n_lat_rows, n_ctx_rows = batch * seq_len, batch * c_len
    assert seq_len % tm == 0 and n_ctx_rows % tm == 0 and n_lat_rows % c_len == 0
    assert seq_len % ATT_BLOCK == 0 and c_len % ATT_BLOCK == 0 and seq_len % GRID_W == 0
    tiles_per_batch = seq_len // tm
    lat_tiles = n_lat_rows // tm
    all_tiles = lat_tiles + n_ctx_rows // tm

    def seg_of(i):
        return jnp.minimum(i // tiles_per_batch, batch)

    def pos_of(i):
        return jnp.where(i < lat_tiles, i % tiles_per_batch, tiles_per_batch + (i - lat_tiles))

    def seg_of_row(r):
        return jnp.minimum(r // seq_len, batch)

    xs, tail = x.reshape(n_lat_rows, d), ctx.reshape(n_ctx_rows, d)
    w_ff_in, w_ff_out = ffn_w_in.astype(BF16), ffn_w_out.astype(BF16)

    n_cond = batch + 1
    cvec = jnp.concatenate([c, c_ctx[None], jnp.zeros((8 - n_cond, d), F32)], axis=0)
    mods = _modulation(cvec.T, ada_w, ada_b, n_cond)
    mods = mods[:, :n_cond].reshape(depth, n_cond, N_MOD, d)
    mods = jnp.pad(mods, ((0, 0), (0, 0), (0, MOD_ROWS - N_MOD), (0, 0)))
    gains = jnp.pad(norm_g, ((0, 0), (0, 8 - norm_g.shape[1]), (0, 0)))

    cos, sin = _rope_tables(seq_len, n_ctx_rows)

    for i in range(depth):
        need_ctx = i < depth - 1
        mod, g = mods[i], gains[i]
        j = i // 2
        xs = _ffn(xs, mod, g, w_ff_in, w_ff_out, (i, 0), mi=0, gi=0, n_tiles=all_tiles, seg_of=seg_of, tail=tail)
        tail = None
        post_tiles = all_tiles if need_ctx else lat_tiles
        if i % 2 == 0:
            q, k, v = _qkv(xs, mod, g, attn_w_in[j].astype(BF16), cos, sin,
                           mi=3, gi=2, n_tiles=all_tiles, seg_of=seg_of, pos_of=pos_of)
            o = _attention(q, k, v, attn_sink[j], batch=batch, seq_len=seq_len, c_len=c_len)
            xs = _proj_out(o, xs, mod, g, attn_w_out[j].astype(BF16),
                           mi=5, gi=3, n_tiles=post_tiles, seg_of=seg_of)
        else:
            if need_ctx:
                raise NotImplementedError("context output of the S5 mixer is only needed when another layer follows")
            ug = _proj_in(xs, mod, g, ssm_w_in[j].astype(BF16), mi=3, gi=2, n_tiles=all_tiles, seg_of=seg_of)
            L = SSM_CHUNK
            pair = GROUPS_PER_STEP * SSM_GROUP_CH
            toep, ws, wc, al = _s5_weights(ssm_a_re[j], ssm_a_im[j], ssm_log_dt[j], ssm_b_re[j], ssm_b_im[j],
                                           ssm_c_re[j], ssm_c_im[j])
            d_lanes = jnp.tile(ssm_d[j].reshape(d // pair, 1, pair), (1, 1, L))
            yg = _s5_core(ug, toep, ws, wc, al, d_lanes, batch=batch, n_lat=seq_len // L, n_ctx=c_len // L)
            xs = _ssm_out(yg, xs, mod, g, ssm_w_glu[j].astype(BF16),
                          mi=5, gi=3, n_rows=post_tiles * tm, seg_of_row=seg_of_row)
        xs = _ffn(xs, mod, g, w_ff_in, w_ff_out, (i, 1), mi=6, gi=4, n_tiles=post_tiles, seg_of=seg_of)
    return xs[:n_lat_rows].reshape(batch, seq_len, d)
```

```python
import functools
import math

import jax
import jax.numpy as jnp
from jax import lax
from jax.experimental import pallas as pl
from jax.experimental.pallas import tpu as pltpu

EPS = 1e-6
NEG_INF = -1e30
N_MOD = 9
GRID_W = 64
N_HEADS = 16
N_KV_HEADS = 4
HEAD_DIM = 128
GQA_GROUP = N_HEADS // N_KV_HEADS
ATT_BLOCK = 128
ROPE_BASE = 10000.0
SSM_GROUP_CH = 16
SSM_STATE = 64
SSM_CHUNK = 16
GROUPS_PER_STEP = 2

TOKEN_TILE = 512
FF_TILE = 512
GLU_TILE = 512
GLU_TOKEN_TILE = 256
MOD_TILE = 1024
MOD_ROWS = 16
VMEM_LIMIT = 56 * 1024 * 1024
LANES = 128

BF16 = jnp.bfloat16
F32 = jnp.float32


def _params(*sem):
    return pltpu.CompilerParams(dimension_semantics=sem, vmem_limit_bytes=VMEM_LIMIT)


def _rms(t, g):
    return t * lax.rsqrt(jnp.mean(t * t, axis=-1, keepdims=True) + EPS) * g


def _sigmoid(t):
    return 1.0 / (1.0 + jnp.exp(-t))


def _gelu_tanh(t):
    return 0.5 * t * (1.0 + jnp.tanh(math.sqrt(2.0 / math.pi) * (t + 0.044715 * (t * t * t))))


def _mod_kernel(ct_ref, w_ref, b_ref, o_ref, *, n_rows):
    w = w_ref[0]
    for r in range(n_rows):
        col = ct_ref[:, r:r + 1]
        col = col * _sigmoid(col)
        o_ref[0, r:r + 1, :] = jnp.sum(w * col, axis=0, keepdims=True) + b_ref[0]
    o_ref[0, n_rows:, :] = jnp.zeros((o_ref.shape[1] - n_rows, o_ref.shape[2]), F32)


def _modulation(cvec_t, ada_w, ada_b, n_rows):
    depth, d, nd = ada_w.shape
    tn = MOD_TILE
    return pl.pallas_call(
        functools.partial(_mod_kernel, n_rows=n_rows),
        out_shape=jax.ShapeDtypeStruct((depth, 8, nd), F32),
        grid=(depth, nd // tn),
        in_specs=[
            pl.BlockSpec((d, 8), lambda l, j: (0, 0)),
            pl.BlockSpec((1, d, tn), lambda l, j: (l, 0, j)),
            pl.BlockSpec((1, 1, tn), lambda l, j: (l, 0, j)),
        ],
        out_specs=pl.BlockSpec((1, 8, tn), lambda l, j: (l, 0, j)),
        compiler_params=_params("arbitrary", "arbitrary"),
        name="adaln_mod",
    )(cvec_t, ada_w, ada_b.reshape(depth, 1, nd))


def _ffn_kernel(x_ref, tail_ref, mod_ref, g_ref, wg_ref, wu_ref, wo_ref, o_ref, h_scr, acc_scr, *, mi, gi, head_tiles):
    def read_x():
        if head_tiles is None:
            return x_ref[...]
        return jnp.where(pl.program_id(0) < head_tiles, x_ref[...], tail_ref[...])

    f = pl.program_id(1)

    @pl.when(f == 0)
    def _():
        h = _rms(read_x(), g_ref[gi:gi + 1, :]) * (1.0 + mod_ref[0, mi + 1:mi + 2, :]) + mod_ref[0, mi:mi + 1, :]
        h_scr[...] = h.astype(BF16)
        acc_scr[...] = jnp.zeros_like(acc_scr)

    h = h_scr[...]
    gate = jnp.dot(h, wg_ref[...], preferred_element_type=F32)
    up = jnp.dot(h, wu_ref[...], preferred_element_type=F32)
    act = (gate * _sigmoid(gate) * up).astype(BF16)
    acc_scr[...] += jnp.dot(act, wo_ref[...], preferred_element_type=F32)

    @pl.when(f == pl.num_programs(1) - 1)
    def _():
        y = _rms(acc_scr[...], g_ref[gi + 1:gi + 2, :])
        o_ref[...] = read_x() + 0.5 * mod_ref[0, mi + 2:mi + 3, :] * y


def _ffn(xs, mod, g, w_in, w_out, which, *, mi, gi, n_tiles, seg_of, tail=None):
    d = xs.shape[1]
    ff = w_out.shape[2]
    tm, tf = TOKEN_TILE, FF_TILE
    nf = ff // tf
    layer, half = which
    if tail is None:
        head_tiles = None
        tokens = [xs, xs]
        token_specs = [pl.BlockSpec((tm, d), lambda i, f: (i, 0)), pl.BlockSpec((8, d), lambda i, f: (0, 0))]
    else:
        head_tiles = xs.shape[0] // tm
        tokens = [xs, tail]
        token_specs = [pl.BlockSpec((tm, d), lambda i, f: (jnp.minimum(i, head_tiles - 1), 0)),
                       pl.BlockSpec((tm, d), lambda i, f: (jnp.maximum(i - head_tiles, 0), 0),
                                    pipeline_mode=pl.Buffered(1))]
    return pl.pallas_call(
        functools.partial(_ffn_kernel, mi=mi, gi=gi, head_tiles=head_tiles),
        out_shape=jax.ShapeDtypeStruct((n_tiles * tm, d), F32),
        grid=(n_tiles, nf),
        in_specs=token_specs + [
            pl.BlockSpec((1, MOD_ROWS, d), lambda i, f: (seg_of(i), 0, 0)),
            pl.BlockSpec((8, d), lambda i, f: (0, 0)),
            pl.BlockSpec((None, None, d, tf), lambda i, f: (layer, half, 0, f)),
            pl.BlockSpec((None, None, d, tf), lambda i, f: (layer, half, 0, nf + f)),
            pl.BlockSpec((None, None, tf, d), lambda i, f: (layer, half, f, 0)),
        ],
        out_specs=pl.BlockSpec((tm, d), lambda i, f: (i, 0)),
        scratch_shapes=[pltpu.VMEM((tm, d), BF16), pltpu.VMEM((tm, d), F32)],
        compiler_params=_params("parallel", "arbitrary"),
        name="ffn_halfstep",
    )(*tokens, mod, g, w_in, w_in, w_out)


def _rope(t, cos, sin, even_quarter):
    partner = jnp.where(even_quarter, pltpu.roll(t, HEAD_DIM - 32, 1), pltpu.roll(t, 32, 1))
    return t * cos + partner * sin


def _qkv_kernel(x_ref, mod_ref, g_ref, w_ref, cos_ref, sin_ref, q_ref, k_ref, v_ref, *, mi, gi):
    h = _rms(x_ref[...], g_ref[gi:gi + 1, :]) * (1.0 + mod_ref[0, mi + 1:mi + 2, :]) + mod_ref[0, mi:mi + 1, :]
    qkv = jnp.dot(h.astype(BF16), w_ref[...], preferred_element_type=F32)
    cos, sin = cos_ref[...], sin_ref[...]
    lane = lax.broadcasted_iota(jnp.int32, cos.shape, 1)
    even_quarter = (lane & 32) == 0
    scale = HEAD_DIM ** -0.5
    nq, nk = N_HEADS * HEAD_DIM, N_KV_HEADS * HEAD_DIM
    for hh in range(N_HEADS):
        sl = slice(hh * HEAD_DIM, (hh + 1) * HEAD_DIM)
        q_ref[:, sl] = (_rope(qkv[:, sl], cos, sin, even_quarter) * scale).astype(BF16)
    for hh in range(N_KV_HEADS):
        sl = slice(hh * HEAD_DIM, (hh + 1) * HEAD_DIM)
        src = slice(nq + hh * HEAD_DIM, nq + (hh + 1) * HEAD_DIM)
        k_ref[:, sl] = _rope(qkv[:, src], cos, sin, even_quarter).astype(BF16)
    v_ref[...] = qkv[:, nq + nk:].astype(BF16)


def _qkv(xs, mod, g, w, cos, sin, *, mi, gi, n_tiles, seg_of, pos_of):
    d = xs.shape[1]
    tm = TOKEN_TILE
    nq, nk = N_HEADS * HEAD_DIM, N_KV_HEADS * HEAD_DIM
    rows = n_tiles * tm
    return pl.pallas_call(
        functools.partial(_qkv_kernel, mi=mi, gi=gi),
        out_shape=(jax.ShapeDtypeStruct((rows, nq), BF16), jax.ShapeDtypeStruct((rows, nk), BF16),
                   jax.ShapeDtypeStruct((rows, nk), BF16)),
        grid=(n_tiles,),
        in_specs=[
            pl.BlockSpec((tm, d), lambda i: (i, 0)),
            pl.BlockSpec((1, MOD_ROWS, d), lambda i: (seg_of(i), 0, 0)),
            pl.BlockSpec((8, d), lambda i: (0, 0)),
            pl.BlockSpec((d, nq + 2 * nk), lambda i: (0, 0), pipeline_mode=pl.Buffered(1)),
            pl.BlockSpec((tm, HEAD_DIM), lambda i: (pos_of(i), 0)),
            pl.BlockSpec((tm, HEAD_DIM), lambda i: (pos_of(i), 0)),
        ],
        out_specs=(pl.BlockSpec((tm, nq), lambda i: (i, 0)), pl.BlockSpec((tm, nk), lambda i: (i, 0)),
                   pl.BlockSpec((tm, nk), lambda i: (i, 0))),
        compiler_params=_params("parallel"),
        name="attn_qkv_rope",
    )(xs, mod, g, w, cos, sin)


def _proj_in_kernel(x_ref, mod_ref, g_ref, w_ref, o_ref, slab_scr, *, mi, gi):
    h = _rms(x_ref[...], g_ref[gi:gi + 1, :]) * (1.0 + mod_ref[0, mi + 1:mi + 2, :]) + mod_ref[0, mi:mi + 1, :]
    u = jnp.dot(h.astype(BF16), w_ref[...], preferred_element_type=F32)
    n_slab, tm, lanes = slab_scr.shape
    pair = GROUPS_PER_STEP * SSM_GROUP_CH
    per_slab = lanes // pair
    for k in range(n_slab):
        slab_scr[k] = u[:, k * lanes:(k + 1) * lanes]
    for t in range(SSM_CHUNK):
        for k in range(n_slab):
            rows = slab_scr[k, pl.ds(t, tm // SSM_CHUNK, stride=SSM_CHUNK), :]
            for jj in range(per_slab):
                o_ref[k * per_slab + jj, :, t * pair:(t + 1) * pair] = rows[:, jj * pair:(jj + 1) * pair].astype(o_ref.dtype)


def _proj_in(xs, mod, g, w, *, mi, gi, n_tiles, seg_of):
    d = xs.shape[1]
    tm = TOKEN_TILE
    pair = GROUPS_PER_STEP * SSM_GROUP_CH
    n_pairs = w.shape[1] // pair
    return pl.pallas_call(
        functools.partial(_proj_in_kernel, mi=mi, gi=gi),
        out_shape=jax.ShapeDtypeStruct((n_pairs, n_tiles * tm // SSM_CHUNK, SSM_CHUNK * pair), BF16),
        grid=(n_tiles,),
        in_specs=[
            pl.BlockSpec((tm, d), lambda i: (i, 0)),
            pl.BlockSpec((1, MOD_ROWS, d), lambda i: (seg_of(i), 0, 0)),
            pl.BlockSpec((8, d), lambda i: (0, 0)),
            pl.BlockSpec(w.shape, lambda i: (0, 0), pipeline_mode=pl.Buffered(1)),
        ],
        out_specs=pl.BlockSpec((n_pairs, tm // SSM_CHUNK, SSM_CHUNK * pair), lambda i: (0, i, 0)),
        scratch_shapes=[pltpu.VMEM((w.shape[1] // LANES, tm, LANES), F32)],
        compiler_params=_params("parallel"),
        name="ssm_in_proj",
    )(xs, mod, g, w)


def _proj_out_kernel(a_ref, x_ref, mod_ref, g_ref, w_ref, o_ref, *, mi, gi):
    out = jnp.dot(a_ref[...], w_ref[...], preferred_element_type=F32)
    o_ref[...] = x_ref[...] + mod_ref[0, mi:mi + 1, :] * _rms(out, g_ref[gi:gi + 1, :])


def _proj_out(a, xs, mod, g, w, *, mi, gi, n_tiles, seg_of):
    d = xs.shape[1]
    tm = TOKEN_TILE
    return pl.pallas_call(
        functools.partial(_proj_out_kernel, mi=mi, gi=gi),
        out_shape=jax.ShapeDtypeStruct((n_tiles * tm, d), F32),
        grid=(n_tiles,),
        in_specs=[
            pl.BlockSpec((tm, a.shape[1]), lambda i: (i, 0)),
            pl.BlockSpec((tm, d), lambda i: (i, 0)),
            pl.BlockSpec((1, MOD_ROWS, d), lambda i: (seg_of(i), 0, 0)),
            pl.BlockSpec((8, d), lambda i: (0, 0)),
            pl.BlockSpec(w.shape, lambda i: (0, 0), pipeline_mode=pl.Buffered(1)),
        ],
        out_specs=pl.BlockSpec((tm, d), lambda i: (i, 0)),
        compiler_params=_params("parallel"),
        name="attn_out_proj",
    )(a, xs, mod, g, w)


def _attn_kernel(sink_ref, q_ref, kp_ref, ko_ref, kn_ref, vp_ref, vo_ref, vn_ref, kc_ref, vc_ref, o_ref,
                 *, n_blk, seq_len):
    n = pl.program_id(1)
    blk = ATT_BLOCK
    c_len = kc_ref.shape[0]
    n_keys = 3 * blk + c_len
    i = lax.broadcasted_iota(jnp.int32, (GQA_GROUP * blk, n_keys), 0) & (blk - 1)
    j = lax.broadcasted_iota(jnp.int32, (GQA_GROUP * blk, n_keys), 1)
    rel = j - i
    kpos = (n - 1) * blk + j
    kpos_end = jnp.where(n < n_blk, seq_len, 0)
    band = (rel >= 0) & (rel <= 2 * blk) & (kpos >= 0) & (kpos < kpos_end)
    mask = band | (j >= 3 * blk)
    row = lax.broadcasted_iota(jnp.int32, (GQA_GROUP * blk, 1), 0)
    for h in range(N_KV_HEADS):
        hs = slice(h * HEAD_DIM, (h + 1) * HEAD_DIM)
        qs = jnp.concatenate(
            [q_ref[:, (h * GQA_GROUP + gq) * HEAD_DIM:(h * GQA_GROUP + gq + 1) * HEAD_DIM] for gq in range(GQA_GROUP)],
            axis=0)
        kb = jnp.concatenate([kp_ref[:, hs], ko_ref[:, hs], kn_ref[:, hs], kc_ref[:, hs]], axis=0)
        vb = jnp.concatenate([vp_ref[:, hs], vo_ref[:, hs], vn_ref[:, hs], vc_ref[:, hs]], axis=0)
        s = lax.dot_general(qs, kb, (((1,), (1,)), ((), ())), preferred_element_type=F32)
        s = jnp.where(mask, s, NEG_INF)
        sink = jnp.full((GQA_GROUP * blk, 1), sink_ref[h * GQA_GROUP], F32)
        for gq in range(1, GQA_GROUP):
            sink = jnp.where(row >= gq * blk, sink_ref[h * GQA_GROUP + gq], sink)
        m = jnp.maximum(jnp.max(s, axis=-1, keepdims=True), sink)
        p = jnp.exp(s - m)
        denom = jnp.sum(p, axis=-1, keepdims=True) + jnp.exp(sink - m)
        o = jnp.dot(p.astype(BF16), vb, preferred_element_type=F32) / denom
        for gq in range(GQA_GROUP):
            col = (h * GQA_GROUP + gq) * HEAD_DIM
            o_ref[:, col:col + HEAD_DIM] = o[gq * blk:(gq + 1) * blk, :].astype(o_ref.dtype)


def _attention(q, k, v, sink, *, batch, seq_len, c_len):
    blk = ATT_BLOCK
    n_blk, c_blk = seq_len // blk, c_len // blk
    nq, nk = q.shape[1], k.shape[1]
    lat_blocks = batch * n_blk
    ctx_block0 = batch * seq_len // c_len

    def q_idx(b, n):
        return jnp.where(n < n_blk, b * n_blk + n, lat_blocks + b * c_blk + (n - n_blk))

    def band_idx(off):
        def idx(b, n, s):
            nn = jnp.clip(n + off, 0, n_blk - 1)
            return (b * n_blk + nn, 0)
        return idx

    kv_spec = [pl.BlockSpec((blk, nk), band_idx(off)) for off in (-1, 0, 1)]
    ctx_spec = pl.BlockSpec((c_len, nk), lambda b, n, s: (ctx_block0 + b, 0))
    grid_spec = pltpu.PrefetchScalarGridSpec(
        num_scalar_prefetch=1,
        grid=(batch, n_blk + c_blk),
        in_specs=[pl.BlockSpec((blk, nq), lambda b, n, s: (q_idx(b, n), 0))] + kv_spec + kv_spec + [ctx_spec, ctx_spec],
        out_specs=pl.BlockSpec((blk, nq), lambda b, n, s: (q_idx(b, n), 0)),
    )
    return pl.pallas_call(
        functools.partial(_attn_kernel, n_blk=n_blk, seq_len=seq_len),
        out_shape=jax.ShapeDtypeStruct(q.shape, BF16),
        grid_spec=grid_spec,
        compiler_params=_params("parallel", "arbitrary"),
        name="window_gqa",
    )(sink, q, k, k, k, v, v, v, k, v)


def _s5_kernel(u_ref, t_ref, ws_ref, wc_ref, al_ref, d_ref, y_ref, s_scr, h_scr, *, batch, n_lat, n_ctx):
    half = GROUPS_PER_STEP * SSM_STATE
    u = u_ref[0]
    s_scr[...] = jnp.dot(u, ws_ref[0], preferred_element_type=F32)
    sub = 8
    a_f = tuple(jnp.broadcast_to(al_ref[0, r:r + 1, :], (sub, half)) for r in (0, 1))
    a_b = tuple(jnp.broadcast_to(al_ref[0, r:r + 1, :], (sub, half)) for r in (2, 3))
    ctx0 = batch * n_lat
    zero = jnp.zeros((sub, half), F32)
    sub_row = lax.broadcasted_iota(jnp.int32, (sub, half), 0)

    def scan_block(h, a, base, col, order, keep):
        h_re, h_im = h
        a_re, a_im = a
        base = pl.multiple_of(base, sub)
        s_re = s_scr[pl.ds(base, sub), col:col + half]
        s_im = s_scr[pl.ds(base, sub), col + half:col + 2 * half]
        e_re, e_im = zero, zero
        for r in order:
            if keep:
                e_re = jnp.where(sub_row == r, h_re, e_re)
                e_im = jnp.where(sub_row == r, h_im, e_im)
            r_re = jnp.broadcast_to(s_re[r:r + 1, :], (sub, half))
            r_im = jnp.broadcast_to(s_im[r:r + 1, :], (sub, half))
            h_re, h_im = a_re * h_re - a_im * h_im + r_re, a_re * h_im + a_im * h_re + r_im
        if keep:
            h_scr[pl.ds(base, sub), col:col + half] = e_re
            h_scr[pl.ds(base, sub), col + half:col + 2 * half] = e_im
        return h_re, h_im

    fwd_order, bwd_order = tuple(range(sub)), tuple(range(sub - 1, -1, -1))

    def make_step(first_row, n_rows, keep):
        def step(m, carry):
            out = []
            for b in range(batch):
                hf, hb = carry[2 * b], carry[2 * b + 1]
                lo = first_row + b * n_rows
                hf = scan_block(hf, a_f, lo + m * sub, 0, fwd_order, keep)
                hb = scan_block(hb, a_b, lo + n_rows - sub - m * sub, 2 * half, bwd_order, keep)
                out += [hf, hb]
            return tuple(out)
        return step

    carry = lax.fori_loop(0, n_ctx // sub, make_step(ctx0, n_ctx, False), ((zero, zero),) * (2 * batch))
    lax.fori_loop(0, n_lat // sub, make_step(0, n_lat, True), carry)

    rows = batch * n_lat
    u_lat = u[:rows]
    y_ref[0] = (jnp.dot(u_lat, t_ref[0], preferred_element_type=F32)
                + jnp.dot(h_scr[...].astype(BF16), wc_ref[0], preferred_element_type=F32)
                + d_ref[0] * u_lat.astype(F32))


def _s5_core(ug, toep, ws, wc, al, d_lanes, *, batch, n_lat, n_ctx):
    n_steps, chunks, width = ug.shape
    assert n_lat % 8 == 0 and n_ctx % 8 == 0
    rows = batch * n_lat
    return pl.pallas_call(
        functools.partial(_s5_kernel, batch=batch, n_lat=n_lat, n_ctx=n_ctx),
        out_shape=jax.ShapeDtypeStruct((n_steps, rows, width), F32),
        grid=(n_steps,),
        in_specs=[
            pl.BlockSpec((1, chunks, width), lambda j: (j, 0, 0)),
            pl.BlockSpec((1,) + toep.shape[1:], lambda j: (j, 0, 0)),
            pl.BlockSpec((1,) + ws.shape[1:], lambda j: (j, 0, 0)),
            pl.BlockSpec((1,) + wc.shape[1:], lambda j: (j, 0, 0)),
            pl.BlockSpec((1,) + al.shape[1:], lambda j: (j, 0, 0)),
            pl.BlockSpec((1, 1, width), lambda j: (j, 0, 0)),
        ],
        out_specs=pl.BlockSpec((1, rows, width), lambda j: (j, 0, 0)),
        scratch_shapes=[pltpu.VMEM((chunks, 4 * GROUPS_PER_STEP * SSM_STATE), F32),
                        pltpu.VMEM((rows, 4 * GROUPS_PER_STEP * SSM_STATE), F32)],
        compiler_params=_params("parallel"),
        name="s5_chunked",
    )(ug, toep, ws, wc, al, d_lanes)


def _cmul(a_re, a_im, b_re, b_im):
    return a_re * b_re - a_im * b_im, a_re * b_im + a_im * b_re


def _zoh(a_re, a_im, log_dt):
    dt = jnp.exp(log_dt)
    mag = jnp.exp(a_re * dt)
    ab_re, ab_im = mag * jnp.cos(a_im * dt), mag * jnp.sin(a_im * dt)
    den = a_re * a_re + a_im * a_im
    x_re, x_im = ab_re - 1.0, ab_im
    return ab_re, ab_im, (x_re * a_re + x_im * a_im) / den, (x_im * a_re - x_re * a_im) / den


def _powers(ab_re, ab_im, n):
    out = [(jnp.ones_like(ab_re), jnp.zeros_like(ab_im))]
    for _ in range(n):
        out.append(_cmul(out[-1][0], out[-1][1], ab_re, ab_im))
    return out


def _s5_op_kernel(row_ref, col_ref, bt_ref, ct_ref, t_ref, ws_ref, wc_ref, al_ref):
    L = SSM_CHUNK
    pair = ct_ref.shape[-1]
    hi = lax.Precision.HIGHEST
    inject, taps, read = [], [], []
    for d in range(2):
        ab_re, ab_im, f_re, f_im = _zoh(*(row_ref[0, 3 * d + r:3 * d + r + 1, :] for r in range(3)))
        pw_row = _powers(ab_re, ab_im, L)
        bb_re, bb_im = _cmul(f_re, f_im, bt_ref[0, d, 0], bt_ref[0, d, 1])
        cb_re, cb_im, _, _ = _zoh(*(col_ref[0, :, 3 * d + r:3 * d + r + 1] for r in range(3)))
        pw_col = _powers(cb_re, cb_im, L)
        ca = [_cmul(ct_ref[0, d, 0], ct_ref[0, d, 1], pr, pi) for pr, pi in pw_col]
        order = range(L - 1, -1, -1) if d == 0 else range(L)
        inject.append([_cmul(pw_row[e][0], pw_row[e][1], bb_re, bb_im) for e in order])
        korder = range(L) if d == 0 else range(L - 1, -1, -1)
        cat_re = jnp.concatenate([ca[k][0] for k in korder], axis=1)
        cat_im = jnp.concatenate([ca[k][1] for k in korder], axis=1)
        taps.append(jnp.dot(bb_re, cat_re, precision=hi, preferred_element_type=F32)
                    - jnp.dot(bb_im, cat_im, precision=hi, preferred_element_type=F32))
        eorder = range(1, L + 1) if d == 0 else range(L, 0, -1)
        read.append((jnp.concatenate([ca[e][0] for e in eorder], axis=1),
                     jnp.concatenate([-ca[e][1] for e in eorder], axis=1)))
        al_ref[0, 2 * d:2 * d + 1, :] = pw_row[L][0]
        al_ref[0, 2 * d + 1:2 * d + 2, :] = pw_row[L][1]
    al_ref[0, 4:, :] = jnp.zeros((4, al_ref.shape[2]), F32)

    for s in range(L):
        ws_ref[0, s * pair:(s + 1) * pair, :] = jnp.concatenate(
            [inject[0][s][0], inject[0][s][1], inject[1][s][0], inject[1][s][1]], axis=1).astype(ws_ref.dtype)
    rows = read[0][0].shape[0]
    for q, blk in enumerate((read[0][0], read[0][1], read[1][0], read[1][1])):
        wc_ref[0, q * rows:(q + 1) * rows, :] = blk.astype(wc_ref.dtype)
    k_f, k_b = taps
    keep = (L - 1) * pair
    by_lag = jnp.concatenate([k_b[:, :keep], k_b[:, keep:] + k_f[:, :pair], k_f[:, pair:]], axis=1)
    for s in range(L):
        t_ref[0, s * pair:(s + 1) * pair, :] = by_lag[:, (L - 1 - s) * pair:(2 * L - 1 - s) * pair].astype(t_ref.dtype)


def _s5_operators(a_re, a_im, log_dt, b_re, b_im, c_re, c_im):
    n_g, n_p = a_re.shape[1], a_re.shape[2]
    gc = b_re.shape[-1]
    n_j, per = n_g // GROUPS_PER_STEP, GROUPS_PER_STEP
    states, pair, width = per * n_p, per * gc, SSM_CHUNK * per * gc
    lam = jnp.stack([a_re, a_im, jnp.broadcast_to(log_dt[..., None], a_re.shape)], axis=1)
    lam = jnp.transpose(lam.reshape(6, n_j, states), (1, 0, 2))
    lam = jnp.pad(lam, ((0, 0), (0, 2), (0, 0)))
    eye = jnp.eye(per, dtype=F32)
    bt = jnp.stack([b_re, b_im], axis=1).reshape(2, 2, n_j, per, n_p, gc)
    bt = jnp.einsum('drjgpc,gh->jdrgchp', bt, eye).reshape(n_j, 2, 2, pair, states)
    ct = jnp.stack([c_re, c_im], axis=1).reshape(2, 2, n_j, per, gc, n_p)
    ct = jnp.einsum('drjgcp,gh->jdrgphc', ct, eye).reshape(n_j, 2, 2, states, pair)
    mat = jax.ShapeDtypeStruct((n_j, width, width), BF16)
    return pl.pallas_call(
        _s5_op_kernel,
        out_shape=(mat, mat, mat, jax.ShapeDtypeStruct((n_j, 8, states), F32)),
        grid=(n_j,),
        in_specs=[
            pl.BlockSpec((1, 8, states), lambda j: (j, 0, 0)),
            pl.BlockSpec((1, states, 8), lambda j: (j, 0, 0)),
            pl.BlockSpec((1, 2, 2, pair, states), lambda j: (j, 0, 0, 0, 0)),
            pl.BlockSpec((1, 2, 2, states, pair), lambda j: (j, 0, 0, 0, 0)),
        ],
        out_specs=(pl.BlockSpec((1, width, width), lambda j: (j, 0, 0)),) * 3
        + (pl.BlockSpec((1, 8, states), lambda j: (j, 0, 0)),),
        compiler_params=_params("parallel"),
        name="s5_operators",
    )(lam, jnp.transpose(lam, (0, 2, 1)), bt, ct)


def _ssm_out_kernel(y_ref, x_ref, mod_ref, g_ref, w_ref, o_ref, slab_scr, act_scr, out_scr, *, mi, gi):
    n_slab, tm, lanes = slab_scr.shape
    pair = GROUPS_PER_STEP * SSM_GROUP_CH
    per_slab = lanes // pair
    for t in range(SSM_CHUNK):
        for k in range(n_slab):
            slab_scr[k, pl.ds(t, tm // SSM_CHUNK, stride=SSM_CHUNK), :] = jnp.concatenate(
                [y_ref[k * per_slab + jj, :, t * pair:(t + 1) * pair] for jj in range(per_slab)], axis=-1)
    for k in range(n_slab):
        act_scr[:, k * lanes:(k + 1) * lanes] = _gelu_tanh(slab_scr[k]).astype(BF16)

    n_c, _, tn = out_scr.shape
    d = n_c * tn
    a = act_scr[...]
    ssq = jnp.zeros((tm, 1), F32)
    for cc in range(n_c):
        val = jnp.dot(a, w_ref[:, cc * tn:(cc + 1) * tn], preferred_element_type=F32)
        gate = jnp.dot(a, w_ref[:, d + cc * tn:d + (cc + 1) * tn], preferred_element_type=F32)
        blk = val * _sigmoid(gate)
        out_scr[cc] = blk
        ssq = ssq + jnp.sum(blk * blk, axis=-1, keepdims=True)
    inv = lax.rsqrt(ssq / d + EPS)
    for cc in range(n_c):
        sl = slice(cc * tn, (cc + 1) * tn)
        o_ref[:, sl] = x_ref[:, sl] + mod_ref[0, mi:mi + 1, sl] * (out_scr[cc] * inv * g_ref[gi:gi + 1, sl])


def _ssm_out(yg, xs, mod, g, w_glu, *, mi, gi, n_rows, seg_of_row):
    d = xs.shape[1]
    tm, tn = GLU_TOKEN_TILE, GLU_TILE
    n_c = d // tn
    n_pairs, _, width = yg.shape
    return pl.pallas_call(
        functools.partial(_ssm_out_kernel, mi=mi, gi=gi),
        out_shape=jax.ShapeDtypeStruct((n_rows, d), F32),
        grid=(n_rows // tm,),
        in_specs=[
            pl.BlockSpec((n_pairs, tm // SSM_CHUNK, width), lambda i: (0, i, 0)),
            pl.BlockSpec((tm, d), lambda i: (i, 0)),
            pl.BlockSpec((1, MOD_ROWS, d), lambda i: (seg_of_row(i * tm), 0, 0)),
            pl.BlockSpec((8, d), lambda i: (0, 0)),
            pl.BlockSpec(w_glu.shape, lambda i: (0, 0), pipeline_mode=pl.Buffered(1)),
        ],
        out_specs=pl.BlockSpec((tm, d), lambda i: (i, 0)),
        scratch_shapes=[pltpu.VMEM((d // LANES, tm, LANES), F32), pltpu.VMEM((tm, d), BF16),
                        pltpu.VMEM((n_c, tm, tn), F32)],
        compiler_params=_params("parallel"),
        name="ssm_glu_out",
    )(yg, xs, mod, g, w_glu)


def _rope_tables(seq_len, extra_rows):
    rows = seq_len // GRID_W
    row = jnp.repeat(jnp.arange(rows, dtype=F32), GRID_W)
    col = jnp.tile(jnp.arange(GRID_W, dtype=F32), rows)
    axis_dim = HEAD_DIM // 2
    inv_freq = ROPE_BASE ** (-jnp.arange(0, axis_dim, 2, dtype=F32) / axis_dim)
    ang_r, ang_c = row[:, None] * inv_freq, col[:, None] * inv_freq
    cos = jnp.concatenate([jnp.cos(ang_r)] * 2 + [jnp.cos(ang_c)] * 2, axis=-1)
    sin = jnp.concatenate([-jnp.sin(ang_r), jnp.sin(ang_r), -jnp.sin(ang_c), jnp.sin(ang_c)], axis=-1)
    cos = jnp.concatenate([cos, jnp.ones((extra_rows, HEAD_DIM), F32)], axis=0)
    sin = jnp.concatenate([sin, jnp.zeros((extra_rows, HEAD_DIM), F32)], axis=0)
    return cos, sin


def kernel(x, c, ctx, c_ctx, ada_w, ada_b, norm_g, ffn_w_in, ffn_w_out, attn_w_in, attn_w_out, attn_sink,
           ssm_w_in, ssm_a_re, ssm_a_im, ssm_log_dt, ssm_b_re, ssm_b_im, ssm_c_re, ssm_c_im, ssm_d, ssm_w_glu):
    batch, seq_len, d = x.shape
    c_len = ctx.shape[1]
    depth = ada_w.shape[0]
    tm = TOKEN_TILE
    n_lat_rows, n_ctx_rows = batch * seq_len, batch * c_len
    assert seq_len % tm == 0 and n_ctx_rows % tm == 0 and n_lat_rows % c_len == 0
    assert seq_len % ATT_BLOCK == 0 and c_len % ATT_BLOCK == 0 and seq_len % GRID_W == 0
    tiles_per_batch = seq_len // tm
    lat_tiles = n_lat_rows // tm
    all_tiles = lat_tiles + n_ctx_rows // tm

    def seg_of(i):
        return jnp.minimum(i // tiles_per_batch, batch)

    def pos_of(i):
        return jnp.where(i < lat_tiles, i % tiles_per_batch, tiles_per_batch + (i - lat_tiles))

    def seg_of_row(r):
        return jnp.minimum(r // seq_len, batch)

    xs, tail = x.reshape(n_lat_rows, d), ctx.reshape(n_ctx_rows, d)
    w_ff_in, w_ff_out = ffn_w_in.astype(BF16), ffn_w_out.astype(BF16)

    n_cond = batch + 1
    cvec = jnp.concatenate([c, c_ctx[None], jnp.zeros((8 - n_cond, d), F32)], axis=0)
    mods = _modulation(cvec.T, ada_w, ada_b, n_cond)
    mods = mods[:, :n_cond].reshape(depth, n_cond, N_MOD, d)
    mods = jnp.pad(mods, ((0, 0), (0, 0), (0, MOD_ROWS - N_MOD), (0, 0)))
    gains = jnp.pad(norm_g, ((0, 0), (0, 8 - norm_g.shape[1]), (0, 0)))

    cos, sin = _rope_tables(seq_len, n_ctx_rows)

    for i in range(depth):
        need_ctx = i < depth - 1
        mod, g = mods[i], gains[i]
        j = i // 2
        xs = _ffn(xs, mod, g, w_ff_in, w_ff_out, (i, 0), mi=0, gi=0, n_tiles=all_tiles, seg_of=seg_of, tail=tail)
        tail = None
        post_tiles = all_tiles if need_ctx else lat_tiles
        if i % 2 == 0:
            q, k, v = _qkv(xs, mod, g, attn_w_in[j].astype(BF16), cos, sin,
                           mi=3, gi=2, n_tiles=all_tiles, seg_of=seg_of, pos_of=pos_of)
            o = _attention(q, k, v, attn_sink[j], batch=batch, seq_len=seq_len, c_len=c_len)
            xs = _proj_out(o, xs, mod, g, attn_w_out[j].astype(BF16),
                           mi=5, gi=3, n_tiles=post_tiles, seg_of=seg_of)
        else:
            if need_ctx:
                raise NotImplementedError("context output of the S5 mixer is only needed when another layer follows")
            ug = _proj_in(xs, mod, g, ssm_w_in[j].astype(BF16), mi=3, gi=2, n_tiles=all_tiles, seg_of=seg_of)
            L = SSM_CHUNK
            pair = GROUPS_PER_STEP * SSM_GROUP_CH
            toep, ws, wc, al = _s5_operators(ssm_a_re[j], ssm_a_im[j], ssm_log_dt[j], ssm_b_re[j], ssm_b_im[j],
                                             ssm_c_re[j], ssm_c_im[j])
            d_lanes = jnp.tile(ssm_d[j].reshape(d // pair, 1, pair), (1, 1, L))
            yg = _s5_core(ug, toep, ws, wc, al, d_lanes, batch=batch, n_lat=seq_len // L, n_ctx=c_len // L)
            xs = _ssm_out(yg, xs, mod, g, ssm_w_glu[j].astype(BF16),
                          mi=5, gi=3, n_rows=post_tiles * tm, seg_of_row=seg_of_row)
        xs = _ffn(xs, mod, g, w_ff_in, w_ff_out, (i, 1), mi=6, gi=4, n_tiles=post_tiles, seg_of=seg_of)
    return xs[:n_lat_rows].reshape(batch, seq_len, d)
```

```python
import functools
import math

import jax
import jax.numpy as jnp
from jax import lax
from jax.experimental import pallas as pl
from jax.experimental.pallas import tpu as pltpu

EPS = 1e-6
NEG_INF = -1e30
N_MOD = 9
GRID_W = 64
N_HEADS = 16
N_KV_HEADS = 4
HEAD_DIM = 128
GQA_GROUP = N_HEADS // N_KV_HEADS
ATT_BLOCK = 128
ROPE_BASE = 10000.0
SSM_GROUP_CH = 16
SSM_STATE = 64
SSM_CHUNK = 16
GROUPS_PER_STEP = 2

TOKEN_TILE = 512
FF_TILE = 512
GLU_TILE = 512
GLU_TOKEN_TILE = 256
MOD_TILE = 1024
MOD_ROWS = 16
VMEM_LIMIT = 56 * 1024 * 1024
LANES = 128

BF16 = jnp.bfloat16
F32 = jnp.float32


def _params(*sem):
    return pltpu.CompilerParams(dimension_semantics=sem, vmem_limit_bytes=VMEM_LIMIT)


def _rms(t, g):
    return t * lax.rsqrt(jnp.mean(t * t, axis=-1, keepdims=True) + EPS) * g


def _sigmoid(t):
    return 1.0 / (1.0 + jnp.exp(-t))


def _gelu_tanh(t):
    return 0.5 * t * (1.0 + jnp.tanh(math.sqrt(2.0 / math.pi) * (t + 0.044715 * (t * t * t))))


def _mod_kernel(ct_ref, w_ref, b_ref, o_ref, *, n_rows):
    w = w_ref[0]
    for r in range(n_rows):
        col = ct_ref[:, r:r + 1]
        col = col * _sigmoid(col)
        o_ref[0, r:r + 1, :] = jnp.sum(w * col, axis=0, keepdims=True) + b_ref[0]
    o_ref[0, n_rows:, :] = jnp.zeros((o_ref.shape[1] - n_rows, o_ref.shape[2]), F32)


def _modulation(cvec_t, ada_w, ada_b, n_rows):
    depth, d, nd = ada_w.shape
    tn = MOD_TILE
    return pl.pallas_call(
        functools.partial(_mod_kernel, n_rows=n_rows),
        out_shape=jax.ShapeDtypeStruct((depth, 8, nd), F32),
        grid=(depth, nd // tn),
        in_specs=[
            pl.BlockSpec((d, 8), lambda l, j: (0, 0)),
            pl.BlockSpec((1, d, tn), lambda l, j: (l, 0, j)),
            pl.BlockSpec((1, 1, tn), lambda l, j: (l, 0, j)),
        ],
        out_specs=pl.BlockSpec((1, 8, tn), lambda l, j: (l, 0, j)),
        compiler_params=_params("arbitrary", "arbitrary"),
        name="adaln_mod",
    )(cvec_t, ada_w, ada_b.reshape(depth, 1, nd))


def _ffn_kernel(x_ref, tail_ref, mod_ref, g_ref, wg_ref, wu_ref, wo_ref, o_ref, h_scr, acc_scr,
                *, mi, gi, head_tiles, n_ff):
    def read_x():
        if head_tiles is None:
            return x_ref[...]
        return jnp.where(pl.program_id(0) < head_tiles, x_ref[...], tail_ref[...])

    def step(first, last):
        if first:
            h = _rms(read_x(), g_ref[gi:gi + 1, :]) * (1.0 + mod_ref[0, mi + 1:mi + 2, :]) + mod_ref[0, mi:mi + 1, :]
            h = h.astype(BF16)
            if not last:
                h_scr[...] = h
        else:
            h = h_scr[...]
        gate = jnp.dot(h, wg_ref[...], preferred_element_type=F32)
        up = jnp.dot(h, wu_ref[...], preferred_element_type=F32)
        act = (gate * _sigmoid(gate) * up).astype(BF16)
        out = jnp.dot(act, wo_ref[...], preferred_element_type=F32)
        if not first:
            out = acc_scr[...] + out
        if last:
            o_ref[...] = read_x() + 0.5 * mod_ref[0, mi + 2:mi + 3, :] * _rms(out, g_ref[gi + 1:gi + 2, :])
        else:
            acc_scr[...] = out

    f = pl.program_id(1)
    if n_ff == 1:
        step(True, True)
    else:
        pl.when(f == 0)(functools.partial(step, True, False))
        if n_ff > 2:
            pl.when((f > 0) & (f < n_ff - 1))(functools.partial(step, False, False))
        pl.when(f == n_ff - 1)(functools.partial(step, False, True))


def _ffn(xs, mod, g, w_in, w_out, which, *, mi, gi, n_tiles, seg_of, tail=None):
    d = xs.shape[1]
    ff = w_out.shape[2]
    tm, tf = TOKEN_TILE, FF_TILE
    nf = ff // tf
    layer, half = which
    if tail is None:
        head_tiles = None
        tokens = [xs, xs]
        token_specs = [pl.BlockSpec((tm, d), lambda i, f: (i, 0)), pl.BlockSpec((8, d), lambda i, f: (0, 0))]
    else:
        head_tiles = xs.shape[0] // tm
        tokens = [xs, tail]
        token_specs = [pl.BlockSpec((tm, d), lambda i, f: (jnp.minimum(i, head_tiles - 1), 0)),
                       pl.BlockSpec((tm, d), lambda i, f: (jnp.maximum(i - head_tiles, 0), 0),
                                    pipeline_mode=pl.Buffered(1))]
    return pl.pallas_call(
        functools.partial(_ffn_kernel, mi=mi, gi=gi, head_tiles=head_tiles, n_ff=nf),
        out_shape=jax.ShapeDtypeStruct((n_tiles * tm, d), F32),
        grid=(n_tiles, nf),
        in_specs=token_specs + [
            pl.BlockSpec((1, MOD_ROWS, d), lambda i, f: (seg_of(i), 0, 0)),
            pl.BlockSpec((8, d), lambda i, f: (0, 0)),
            pl.BlockSpec((None, None, d, tf), lambda i, f: (layer, half, 0, f)),
            pl.BlockSpec((None, None, d, tf), lambda i, f: (layer, half, 0, nf + f)),
            pl.BlockSpec((None, None, tf, d), lambda i, f: (layer, half, f, 0)),
        ],
        out_specs=pl.BlockSpec((tm, d), lambda i, f: (i, 0)),
        scratch_shapes=[pltpu.VMEM((tm, d), BF16), pltpu.VMEM((tm, d), F32)],
        compiler_params=_params("parallel", "arbitrary"),
        name="ffn_halfstep",
    )(*tokens, mod, g, w_in, w_in, w_out)


def _rope(t, cos, sin, even_quarter):
    partner = jnp.where(even_quarter, pltpu.roll(t, HEAD_DIM - 32, 1), pltpu.roll(t, 32, 1))
    return t * cos + partner * sin


def _qkv_kernel(x_ref, mod_ref, g_ref, w_ref, cos_ref, sin_ref, q_ref, k_ref, v_ref, *, mi, gi):
    h = _rms(x_ref[...], g_ref[gi:gi + 1, :]) * (1.0 + mod_ref[0, mi + 1:mi + 2, :]) + mod_ref[0, mi:mi + 1, :]
    qkv = jnp.dot(h.astype(BF16), w_ref[...], preferred_element_type=F32)
    cos, sin = cos_ref[...], sin_ref[...]
    lane = lax.broadcasted_iota(jnp.int32, cos.shape, 1)
    even_quarter = (lane & 32) == 0
    scale = HEAD_DIM ** -0.5
    nq, nk = N_HEADS * HEAD_DIM, N_KV_HEADS * HEAD_DIM
    for hh in range(N_HEADS):
        sl = slice(hh * HEAD_DIM, (hh + 1) * HEAD_DIM)
        q_ref[:, sl] = (_rope(qkv[:, sl], cos, sin, even_quarter) * scale).astype(BF16)
    for hh in range(N_KV_HEADS):
        sl = slice(hh * HEAD_DIM, (hh + 1) * HEAD_DIM)
        src = slice(nq + hh * HEAD_DIM, nq + (hh + 1) * HEAD_DIM)
        k_ref[:, sl] = _rope(qkv[:, src], cos, sin, even_quarter).astype(BF16)
    v_ref[...] = qkv[:, nq + nk:].astype(BF16)


def _qkv(xs, mod, g, w, cos, sin, *, mi, gi, n_tiles, seg_of, pos_of):
    d = xs.shape[1]
    tm = TOKEN_TILE
    nq, nk = N_HEADS * HEAD_DIM, N_KV_HEADS * HEAD_DIM
    rows = n_tiles * tm
    return pl.pallas_call(
        functools.partial(_qkv_kernel, mi=mi, gi=gi),
        out_shape=(jax.ShapeDtypeStruct((rows, nq), BF16), jax.ShapeDtypeStruct((rows, nk), BF16),
                   jax.ShapeDtypeStruct((rows, nk), BF16)),
        grid=(n_tiles,),
        in_specs=[
            pl.BlockSpec((tm, d), lambda i: (i, 0)),
            pl.BlockSpec((1, MOD_ROWS, d), lambda i: (seg_of(i), 0, 0)),
            pl.BlockSpec((8, d), lambda i: (0, 0)),
            pl.BlockSpec((d, nq + 2 * nk), lambda i: (0, 0), pipeline_mode=pl.Buffered(1)),
            pl.BlockSpec((tm, HEAD_DIM), lambda i: (pos_of(i), 0)),
            pl.BlockSpec((tm, HEAD_DIM), lambda i: (pos_of(i), 0)),
        ],
        out_specs=(pl.BlockSpec((tm, nq), lambda i: (i, 0)), pl.BlockSpec((tm, nk), lambda i: (i, 0)),
                   pl.BlockSpec((tm, nk), lambda i: (i, 0))),
        compiler_params=_params("parallel"),
        name="attn_qkv_rope",
    )(xs, mod, g, w, cos, sin)


def _proj_in_kernel(x_ref, mod_ref, g_ref, w_ref, o_ref, slab_scr, *, mi, gi):
    h = _rms(x_ref[...], g_ref[gi:gi + 1, :]) * (1.0 + mod_ref[0, mi + 1:mi + 2, :]) + mod_ref[0, mi:mi + 1, :]
    u = jnp.dot(h.astype(BF16), w_ref[...], preferred_element_type=F32)
    n_slab, tm, lanes = slab_scr.shape
    pair = GROUPS_PER_STEP * SSM_GROUP_CH
    per_slab = lanes // pair
    for k in range(n_slab):
        slab_scr[k] = u[:, k * lanes:(k + 1) * lanes]
    for t in range(SSM_CHUNK):
        for k in range(n_slab):
            rows = slab_scr[k, pl.ds(t, tm // SSM_CHUNK, stride=SSM_CHUNK), :]
            for jj in range(per_slab):
                o_ref[k * per_slab + jj, :, t * pair:(t + 1) * pair] = rows[:, jj * pair:(jj + 1) * pair].astype(o_ref.dtype)


def _proj_in(xs, mod, g, w, *, mi, gi, n_tiles, seg_of):
    d = xs.shape[1]
    tm = TOKEN_TILE
    pair = GROUPS_PER_STEP * SSM_GROUP_CH
    n_pairs = w.shape[1] // pair
    return pl.pallas_call(
        functools.partial(_proj_in_kernel, mi=mi, gi=gi),
        out_shape=jax.ShapeDtypeStruct((n_pairs, n_tiles * tm // SSM_CHUNK, SSM_CHUNK * pair), BF16),
        grid=(n_tiles,),
        in_specs=[
            pl.BlockSpec((tm, d), lambda i: (i, 0)),
            pl.BlockSpec((1, MOD_ROWS, d), lambda i: (seg_of(i), 0, 0)),
            pl.BlockSpec((8, d), lambda i: (0, 0)),
            pl.BlockSpec(w.shape, lambda i: (0, 0), pipeline_mode=pl.Buffered(1)),
        ],
        out_specs=pl.BlockSpec((n_pairs, tm // SSM_CHUNK, SSM_CHUNK * pair), lambda i: (0, i, 0)),
        scratch_shapes=[pltpu.VMEM((w.shape[1] // LANES, tm, LANES), F32)],
        compiler_params=_params("parallel"),
        name="ssm_in_proj",
    )(xs, mod, g, w)


def _proj_out_kernel(a_ref, x_ref, mod_ref, g_ref, w_ref, o_ref, *, mi, gi):
    out = jnp.dot(a_ref[...], w_ref[...], preferred_element_type=F32)
    o_ref[...] = x_ref[...] + mod_ref[0, mi:mi + 1, :] * _rms(out, g_ref[gi:gi + 1, :])


def _proj_out(a, xs, mod, g, w, *, mi, gi, n_tiles, seg_of):
    d = xs.shape[1]
    tm = TOKEN_TILE
    return pl.pallas_call(
        functools.partial(_proj_out_kernel, mi=mi, gi=gi),
        out_shape=jax.ShapeDtypeStruct((n_tiles * tm, d), F32),
        grid=(n_tiles,),
        in_specs=[
            pl.BlockSpec((tm, a.shape[1]), lambda i: (i, 0)),
            pl.BlockSpec((tm, d), lambda i: (i, 0)),
            pl.BlockSpec((1, MOD_ROWS, d), lambda i: (seg_of(i), 0, 0)),
            pl.BlockSpec((8, d), lambda i: (0, 0)),
            pl.BlockSpec(w.shape, lambda i: (0, 0), pipeline_mode=pl.Buffered(1)),
        ],
        out_specs=pl.BlockSpec((tm, d), lambda i: (i, 0)),
        compiler_params=_params("parallel"),
        name="attn_out_proj",
    )(a, xs, mod, g, w)


def _attn_kernel(sink_ref, q_ref, kp_ref, ko_ref, kn_ref, vp_ref, vo_ref, vn_ref, kc_ref, vc_ref, o_ref,
                 *, n_blk, seq_len):
    n = pl.program_id(1)
    blk = ATT_BLOCK
    c_len = kc_ref.shape[0]
    n_keys = 3 * blk + c_len
    i = lax.broadcasted_iota(jnp.int32, (GQA_GROUP * blk, n_keys), 0) & (blk - 1)
    j = lax.broadcasted_iota(jnp.int32, (GQA_GROUP * blk, n_keys), 1)
    rel = j - i
    kpos = (n - 1) * blk + j
    kpos_end = jnp.where(n < n_blk, seq_len, 0)
    band = (rel >= 0) & (rel <= 2 * blk) & (kpos >= 0) & (kpos < kpos_end)
    mask = band | (j >= 3 * blk)
    row = lax.broadcasted_iota(jnp.int32, (GQA_GROUP * blk, 1), 0)
    for h in range(N_KV_HEADS):
        hs = slice(h * HEAD_DIM, (h + 1) * HEAD_DIM)
        qs = jnp.concatenate(
            [q_ref[:, (h * GQA_GROUP + gq) * HEAD_DIM:(h * GQA_GROUP + gq + 1) * HEAD_DIM] for gq in range(GQA_GROUP)],
            axis=0)
        kb = jnp.concatenate([kp_ref[:, hs], ko_ref[:, hs], kn_ref[:, hs], kc_ref[:, hs]], axis=0)
        vb = jnp.concatenate([vp_ref[:, hs], vo_ref[:, hs], vn_ref[:, hs], vc_ref[:, hs]], axis=0)
        s = lax.dot_general(qs, kb, (((1,), (1,)), ((), ())), preferred_element_type=F32)
        s = jnp.where(mask, s, NEG_INF)
        sink = jnp.full((GQA_GROUP * blk, 1), sink_ref[h * GQA_GROUP], F32)
        for gq in range(1, GQA_GROUP):
            sink = jnp.where(row >= gq * blk, sink_ref[h * GQA_GROUP + gq], sink)
        m = jnp.maximum(jnp.max(s, axis=-1, keepdims=True), sink)
        p = jnp.exp(s - m)
        denom = jnp.sum(p, axis=-1, keepdims=True) + jnp.exp(sink - m)
        o = jnp.dot(p.astype(BF16), vb, preferred_element_type=F32) / denom
        for gq in range(GQA_GROUP):
            col = (h * GQA_GROUP + gq) * HEAD_DIM
            o_ref[:, col:col + HEAD_DIM] = o[gq * blk:(gq + 1) * blk, :].astype(o_ref.dtype)


def _attention(q, k, v, sink, *, batch, seq_len, c_len):
    blk = ATT_BLOCK
    n_blk, c_blk = seq_len // blk, c_len // blk
    nq, nk = q.shape[1], k.shape[1]
    lat_blocks = batch * n_blk
    ctx_block0 = batch * seq_len // c_len

    def q_idx(b, n):
        return jnp.where(n < n_blk, b * n_blk + n, lat_blocks + b * c_blk + (n - n_blk))

    def band_idx(off):
        def idx(b, n, s):
            nn = jnp.clip(n + off, 0, n_blk - 1)
            return (b * n_blk + nn, 0)
        return idx

    kv_spec = [pl.BlockSpec((blk, nk), band_idx(off)) for off in (-1, 0, 1)]
    ctx_spec = pl.BlockSpec((c_len, nk), lambda b, n, s: (ctx_block0 + b, 0))
    grid_spec = pltpu.PrefetchScalarGridSpec(
        num_scalar_prefetch=1,
        grid=(batch, n_blk + c_blk),
        in_specs=[pl.BlockSpec((blk, nq), lambda b, n, s: (q_idx(b, n), 0))] + kv_spec + kv_spec + [ctx_spec, ctx_spec],
        out_specs=pl.BlockSpec((blk, nq), lambda b, n, s: (q_idx(b, n), 0)),
    )
    return pl.pallas_call(
        functools.partial(_attn_kernel, n_blk=n_blk, seq_len=seq_len),
        out_shape=jax.ShapeDtypeStruct(q.shape, BF16),
        grid_spec=grid_spec,
        compiler_params=_params("parallel", "arbitrary"),
        name="window_gqa",
    )(sink, q, k, k, k, v, v, v, k, v)


def _s5_kernel(u_ref, t_ref, ws_ref, wc_ref, al_ref, d_ref, y_ref, s_scr, h_scr, *, batch, n_lat, n_ctx):
    half = GROUPS_PER_STEP * SSM_STATE
    u = u_ref[0]
    s_scr[...] = jnp.dot(u, ws_ref[0], preferred_element_type=F32)
    sub = 8
    a_f = tuple(jnp.broadcast_to(al_ref[0, r:r + 1, :], (sub, half)) for r in (0, 1))
    a_b = tuple(jnp.broadcast_to(al_ref[0, r:r + 1, :], (sub, half)) for r in (2, 3))
    ctx0 = batch * n_lat
    zero = jnp.zeros((sub, half), F32)
    sub_row = lax.broadcasted_iota(jnp.int32, (sub, half), 0)

    def scan_block(h, a, base, col, order, keep):
        h_re, h_im = h
        a_re, a_im = a
        base = pl.multiple_of(base, sub)
        s_re = s_scr[pl.ds(base, sub), col:col + half]
        s_im = s_scr[pl.ds(base, sub), col + half:col + 2 * half]
        e_re, e_im = zero, zero
        for r in order:
            if keep:
                e_re = jnp.where(sub_row == r, h_re, e_re)
                e_im = jnp.where(sub_row == r, h_im, e_im)
            r_re = jnp.broadcast_to(s_re[r:r + 1, :], (sub, half))
            r_im = jnp.broadcast_to(s_im[r:r + 1, :], (sub, half))
            h_re, h_im = a_re * h_re - a_im * h_im + r_re, a_re * h_im + a_im * h_re + r_im
        if keep:
            h_scr[pl.ds(base, sub), col:col + half] = e_re
            h_scr[pl.ds(base, sub), col + half:col + 2 * half] = e_im
        return h_re, h_im

    fwd_order, bwd_order = tuple(range(sub)), tuple(range(sub - 1, -1, -1))

    def make_step(first_row, n_rows, keep):
        def step(m, carry):
            out = []
            for b in range(batch):
                hf, hb = carry[2 * b], carry[2 * b + 1]
                lo = first_row + b * n_rows
                hf = scan_block(hf, a_f, lo + m * sub, 0, fwd_order, keep)
                hb = scan_block(hb, a_b, lo + n_rows - sub - m * sub, 2 * half, bwd_order, keep)
                out += [hf, hb]
            return tuple(out)
        return step

    carry = lax.fori_loop(0, n_ctx // sub, make_step(ctx0, n_ctx, False), ((zero, zero),) * (2 * batch))
    lax.fori_loop(0, n_lat // sub, make_step(0, n_lat, True), carry)

    rows = batch * n_lat
    u_lat = u[:rows]
    y_ref[0] = (jnp.dot(u_lat, t_ref[0], preferred_element_type=F32)
                + jnp.dot(h_scr[...].astype(BF16), wc_ref[0], preferred_element_type=F32)
                + d_ref[0] * u_lat.astype(F32))


def _s5_core(ug, toep, ws, wc, al, d_lanes, *, batch, n_lat, n_ctx):
    n_steps, chunks, width = ug.shape
    assert n_lat % 8 == 0 and n_ctx % 8 == 0
    rows = batch * n_lat
    return pl.pallas_call(
        functools.partial(_s5_kernel, batch=batch, n_lat=n_lat, n_ctx=n_ctx),
        out_shape=jax.ShapeDtypeStruct((n_steps, rows, width), F32),
        grid=(n_steps,),
        in_specs=[
            pl.BlockSpec((1, chunks, width), lambda j: (j, 0, 0)),
            pl.BlockSpec((1,) + toep.shape[1:], lambda j: (j, 0, 0)),
            pl.BlockSpec((1,) + ws.shape[1:], lambda j: (j, 0, 0)),
            pl.BlockSpec((1,) + wc.shape[1:], lambda j: (j, 0, 0)),
            pl.BlockSpec((1,) + al.shape[1:], lambda j: (j, 0, 0)),
            pl.BlockSpec((1, 1, width), lambda j: (j, 0, 0)),
        ],
        out_specs=pl.BlockSpec((1, rows, width), lambda j: (j, 0, 0)),
        scratch_shapes=[pltpu.VMEM((chunks, 4 * GROUPS_PER_STEP * SSM_STATE), F32),
                        pltpu.VMEM((rows, 4 * GROUPS_PER_STEP * SSM_STATE), F32)],
        compiler_params=_params("parallel"),
        name="s5_chunked",
    )(ug, toep, ws, wc, al, d_lanes)


def _cmul(a_re, a_im, b_re, b_im):
    return a_re * b_re - a_im * b_im, a_re * b_im + a_im * b_re


def _zoh(a_re, a_im, log_dt):
    dt = jnp.exp(log_dt)
    mag = jnp.exp(a_re * dt)
    ab_re, ab_im = mag * jnp.cos(a_im * dt), mag * jnp.sin(a_im * dt)
    den = a_re * a_re + a_im * a_im
    x_re, x_im = ab_re - 1.0, ab_im
    return ab_re, ab_im, (x_re * a_re + x_im * a_im) / den, (x_im * a_re - x_re * a_im) / den


def _powers(ab_re, ab_im, n):
    out = [(jnp.ones_like(ab_re), jnp.zeros_like(ab_im))]
    for _ in range(n):
        out.append(_cmul(out[-1][0], out[-1][1], ab_re, ab_im))
    return out


def _s5_op_kernel(row_ref, col_ref, bt_ref, ct_ref, t_ref, ws_ref, wc_ref, al_ref):
    L = SSM_CHUNK
    pair = ct_ref.shape[-1]
    hi = lax.Precision.HIGHEST
    inject, taps, read = [], [], []
    for d in range(2):
        ab_re, ab_im, f_re, f_im = _zoh(*(row_ref[0, 3 * d + r:3 * d + r + 1, :] for r in range(3)))
        pw_row = _powers(ab_re, ab_im, L)
        bb_re, bb_im = _cmul(f_re, f_im, bt_ref[0, d, 0], bt_ref[0, d, 1])
        cb_re, cb_im, _, _ = _zoh(*(col_ref[0, :, 3 * d + r:3 * d + r + 1] for r in range(3)))
        pw_col = _powers(cb_re, cb_im, L)
        ca = [_cmul(ct_ref[0, d, 0], ct_ref[0, d, 1], pr, pi) for pr, pi in pw_col]
        order = range(L - 1, -1, -1) if d == 0 else range(L)
        inject.append([_cmul(pw_row[e][0], pw_row[e][1], bb_re, bb_im) for e in order])
        korder = range(L) if d == 0 else range(L - 1, -1, -1)
        cat_re = jnp.concatenate([ca[k][0] for k in korder], axis=1)
        cat_im = jnp.concatenate([ca[k][1] for k in korder], axis=1)
        taps.append(jnp.dot(bb_re, cat_re, precision=hi, preferred_element_type=F32)
                    - jnp.dot(bb_im, cat_im, precision=hi, preferred_element_type=F32))
        eorder = range(1, L + 1) if d == 0 else range(L, 0, -1)
        read.append((jnp.concatenate([ca[e][0] for e in eorder], axis=1),
                     jnp.concatenate([-ca[e][1] for e in eorder], axis=1)))
        al_ref[0, 2 * d:2 * d + 1, :] = pw_row[L][0]
        al_ref[0, 2 * d + 1:2 * d + 2, :] = pw_row[L][1]
    al_ref[0, 4:, :] = jnp.zeros((4, al_ref.shape[2]), F32)

    for s in range(L):
        ws_ref[0, s * pair:(s + 1) * pair, :] = jnp.concatenate(
            [inject[0][s][0], inject[0][s][1], inject[1][s][0], inject[1][s][1]], axis=1).astype(ws_ref.dtype)
    rows = read[0][0].shape[0]
    for q, blk in enumerate((read[0][0], read[0][1], read[1][0], read[1][1])):
        wc_ref[0, q * rows:(q + 1) * rows, :] = blk.astype(wc_ref.dtype)
    k_f, k_b = taps
    keep = (L - 1) * pair
    by_lag = jnp.concatenate([k_b[:, :keep], k_b[:, keep:] + k_f[:, :pair], k_f[:, pair:]], axis=1)
    for s in range(L):
        t_ref[0, s * pair:(s + 1) * pair, :] = by_lag[:, (L - 1 - s) * pair:(2 * L - 1 - s) * pair].astype(t_ref.dtype)


def _s5_operators(a_re, a_im, log_dt, b_re, b_im, c_re, c_im):
    n_g, n_p = a_re.shape[1], a_re.shape[2]
    gc = b_re.shape[-1]
    n_j, per = n_g // GROUPS_PER_STEP, GROUPS_PER_STEP
    states, pair, width = per * n_p, per * gc, SSM_CHUNK * per * gc
    lam = jnp.stack([a_re, a_im, jnp.broadcast_to(log_dt[..., None], a_re.shape)], axis=1)
    lam = jnp.transpose(lam.reshape(6, n_j, states), (1, 0, 2))
    lam = jnp.pad(lam, ((0, 0), (0, 2), (0, 0)))
    eye = jnp.eye(per, dtype=F32)
    bt = jnp.stack([b_re, b_im], axis=1).reshape(2, 2, n_j, per, n_p, gc)
    bt = jnp.einsum('drjgpc,gh->jdrgchp', bt, eye).reshape(n_j, 2, 2, pair, states)
    ct = jnp.stack([c_re, c_im], axis=1).reshape(2, 2, n_j, per, gc, n_p)
    ct = jnp.einsum('drjgcp,gh->jdrgphc', ct, eye).reshape(n_j, 2, 2, states, pair)
    mat = jax.ShapeDtypeStruct((n_j, width, width), BF16)
    return pl.pallas_call(
        _s5_op_kernel,
        out_shape=(mat, mat, mat, jax.ShapeDtypeStruct((n_j, 8, states), F32)),
        grid=(n_j,),
        in_specs=[
            pl.BlockSpec((1, 8, states), lambda j: (j, 0, 0)),
            pl.BlockSpec((1, states, 8), lambda j: (j, 0, 0)),
            pl.BlockSpec((1, 2, 2, pair, states), lambda j: (j, 0, 0, 0, 0)),
            pl.BlockSpec((1, 2, 2, states, pair), lambda j: (j, 0, 0, 0, 0)),
        ],
        out_specs=(pl.BlockSpec((1, width, width), lambda j: (j, 0, 0)),) * 3
        + (pl.BlockSpec((1, 8, states), lambda j: (j, 0, 0)),),
        compiler_params=_params("parallel"),
        name="s5_operators",
    )(lam, jnp.transpose(lam, (0, 2, 1)), bt, ct)


def _ssm_out_kernel(y_ref, x_ref, mod_ref, g_ref, w_ref, o_ref, slab_scr, act_even, act_odd, out_scr, *, mi, gi):
    i = pl.program_id(0)

    @pl.when(i == 0)
    def _():
        act_odd[...] = jnp.zeros_like(act_odd)

    @pl.when(i % 2 == 0)
    def _():
        _ssm_out_step(y_ref, x_ref, mod_ref, g_ref, w_ref, o_ref, slab_scr, act_even, act_odd, out_scr, mi=mi, gi=gi)

    @pl.when(i % 2 == 1)
    def _():
        _ssm_out_step(y_ref, x_ref, mod_ref, g_ref, w_ref, o_ref, slab_scr, act_odd, act_even, out_scr, mi=mi, gi=gi)


def _ssm_out_step(y_ref, x_ref, mod_ref, g_ref, w_ref, o_ref, slab_scr, act_new, act_old, out_scr, *, mi, gi):
    n_slab, tm, lanes = slab_scr.shape
    pair = GROUPS_PER_STEP * SSM_GROUP_CH
    per_slab = lanes // pair
    for t in range(SSM_CHUNK):
        for k in range(n_slab):
            slab_scr[k, pl.ds(t, tm // SSM_CHUNK, stride=SSM_CHUNK), :] = jnp.concatenate(
                [y_ref[k * per_slab + jj, :, t * pair:(t + 1) * pair] for jj in range(per_slab)], axis=-1)
    for k in range(n_slab):
        act_new[:, k * lanes:(k + 1) * lanes] = _gelu_tanh(slab_scr[k]).astype(BF16)

    n_c, _, tn = out_scr.shape
    d = n_c * tn
    a = act_old[...]
    ssq = jnp.zeros((tm, 1), F32)
    for cc in range(n_c):
        val = jnp.dot(a, w_ref[:, cc * tn:(cc + 1) * tn], preferred_element_type=F32)
        gate = jnp.dot(a, w_ref[:, d + cc * tn:d + (cc + 1) * tn], preferred_element_type=F32)
        blk = val * _sigmoid(gate)
        out_scr[cc] = blk
        ssq = ssq + jnp.sum(blk * blk, axis=-1, keepdims=True)
    inv = lax.rsqrt(ssq / d + EPS)
    for cc in range(n_c):
        sl = slice(cc * tn, (cc + 1) * tn)
        o_ref[:, sl] = x_ref[:, sl] + mod_ref[0, mi:mi + 1, sl] * (out_scr[cc] * inv * g_ref[gi:gi + 1, sl])


def _ssm_out(yg, xs, mod, g, w_glu, *, mi, gi, n_rows, seg_of_row):
    d = xs.shape[1]
    tm, tn = GLU_TOKEN_TILE, GLU_TILE
    n_c = d // tn
    n_pairs, _, width = yg.shape
    n_tiles = n_rows // tm

    def prev(i):
        return jnp.maximum(i - 1, 0)

    return pl.pallas_call(
        functools.partial(_ssm_out_kernel, mi=mi, gi=gi),
        out_shape=jax.ShapeDtypeStruct((n_rows, d), F32),
        grid=(n_tiles + 1,),
        in_specs=[
            pl.BlockSpec((n_pairs, tm // SSM_CHUNK, width), lambda i: (0, jnp.minimum(i, n_tiles - 1), 0)),
            pl.BlockSpec((tm, d), lambda i: (prev(i), 0)),
            pl.BlockSpec((1, MOD_ROWS, d), lambda i: (seg_of_row(prev(i) * tm), 0, 0)),
            pl.BlockSpec((8, d), lambda i: (0, 0)),
            pl.BlockSpec(w_glu.shape, lambda i: (0, 0), pipeline_mode=pl.Buffered(1)),
        ],
        out_specs=pl.BlockSpec((tm, d), lambda i: (prev(i), 0)),
        scratch_shapes=[pltpu.VMEM((d // LANES, tm, LANES), F32), pltpu.VMEM((tm, d), BF16),
                        pltpu.VMEM((tm, d), BF16), pltpu.VMEM((n_c, tm, tn), F32)],
        compiler_params=_params("arbitrary"),
        name="ssm_glu_out",
    )(yg, xs, mod, g, w_glu)


def _rope_tables(seq_len, extra_rows):
    rows = seq_len // GRID_W
    row = jnp.repeat(jnp.arange(rows, dtype=F32), GRID_W)
    col = jnp.tile(jnp.arange(GRID_W, dtype=F32), rows)
    axis_dim = HEAD_DIM // 2
    inv_freq = ROPE_BASE ** (-jnp.arange(0, axis_dim, 2, dtype=F32) / axis_dim)
    ang_r, ang_c = row[:, None] * inv_freq, col[:, None] * inv_freq
    cos = jnp.concatenate([jnp.cos(ang_r)] * 2 + [jnp.cos(ang_c)] * 2, axis=-1)
    sin = jnp.concatenate([-jnp.sin(ang_r), jnp.sin(ang_r), -jnp.sin(ang_c), jnp.sin(ang_c)], axis=-1)
    cos = jnp.concatenate([cos, jnp.ones((extra_rows, HEAD_DIM), F32)], axis=0)
    sin = jnp.concatenate([sin, jnp.zeros((extra_rows, HEAD_DIM), F32)], axis=0)
    return cos, sin


def kernel(x, c, ctx, c_ctx, ada_w, ada_b, norm_g, ffn_w_in, ffn_w_out, attn_w_in, attn_w_out, attn_sink,
           ssm_w_in, ssm_a_re, ssm_a_im, ssm_log_dt, ssm_b_re, ssm_b_im, ssm_c_re, ssm_c_im, ssm_d, ssm_w_glu):
    batch, seq_len, d = x.shape
    c_len = ctx.shape[1]
    depth = ada_w.shape[0]
    tm = TOKEN_TILE
    n_lat_rows, n_ctx_rows = batch * seq_len, batch * c_len
    assert seq_len % tm == 0 and n_ctx_rows % tm == 0 and n_lat_rows % c_len == 0
    assert seq_len % ATT_BLOCK == 0 and c_len % ATT_BLOCK == 0 and seq_len % GRID_W == 0
    tiles_per_batch = seq_len // tm
    lat_tiles = n_lat_rows // tm
    all_tiles = lat_tiles + n_ctx_rows // tm

    def seg_of(i):
        return jnp.minimum(i // tiles_per_batch, batch)

    def pos_of(i):
        return jnp.where(i < lat_tiles, i % tiles_per_batch, tiles_per_batch + (i - lat_tiles))

    def seg_of_row(r):
        return jnp.minimum(r // seq_len, batch)

    xs, tail = x.reshape(n_lat_rows, d), ctx.reshape(n_ctx_rows, d)
    w_ff_in, w_ff_out = ffn_w_in.astype(BF16), ffn_w_out.astype(BF16)

    n_cond = batch + 1
    cvec = jnp.concatenate([c, c_ctx[None], jnp.zeros((8 - n_cond, d), F32)], axis=0)
    mods = _modulation(cvec.T, ada_w, ada_b, n_cond)
    mods = mods[:, :n_cond].reshape(depth, n_cond, N_MOD, d)
    mods = jnp.pad(mods, ((0, 0), (0, 0), (0, MOD_ROWS - N_MOD), (0, 0)))
    gains = jnp.pad(norm_g, ((0, 0), (0, 8 - norm_g.shape[1]), (0, 0)))

    cos, sin = _rope_tables(seq_len, n_ctx_rows)

    for i in range(depth):
        need_ctx = i < depth - 1
        mod, g = mods[i], gains[i]
        j = i // 2
        xs = _ffn(xs, mod, g, w_ff_in, w_ff_out, (i, 0), mi=0, gi=0, n_tiles=all_tiles, seg_of=seg_of, tail=tail)
        tail = None
        post_tiles = all_tiles if need_ctx else lat_tiles
        if i % 2 == 0:
            q, k, v = _qkv(xs, mod, g, attn_w_in[j].astype(BF16), cos, sin,
                           mi=3, gi=2, n_tiles=all_tiles, seg_of=seg_of, pos_of=pos_of)
            o = _attention(q, k, v, attn_sink[j], batch=batch, seq_len=seq_len, c_len=c_len)
            xs = _proj_out(o, xs, mod, g, attn_w_out[j].astype(BF16),
                           mi=5, gi=3, n_tiles=post_tiles, seg_of=seg_of)
        else:
            if need_ctx:
                raise NotImplementedError("context output of the S5 mixer is only needed when another layer follows")
            ug = _proj_in(xs, mod, g, ssm_w_in[j].astype(BF16), mi=3, gi=2, n_tiles=all_tiles, seg_of=seg_of)
            L = SSM_CHUNK
            pair = GROUPS_PER_STEP * SSM_GROUP_CH
            toep, ws, wc, al = _s5_operators(ssm_a_re[j], ssm_a_im[j], ssm_log_dt[j], ssm_b_re[j], ssm_b_im[j],
                                             ssm_c_re[j], ssm_c_im[j])
            d_lanes = jnp.tile(ssm_d[j].reshape(d // pair, 1, pair), (1, 1, L))
            yg = _s5_core(ug, toep, ws, wc, al, d_lanes, batch=batch, n_lat=seq_len // L, n_ctx=c_len // L)
            xs = _ssm_out(yg, xs, mod, g, ssm_w_glu[j].astype(BF16),
                          mi=5, gi=3, n_rows=post_tiles * tm, seg_of_row=seg_of_row)
        xs = _ffn(xs, mod, g, w_ff_in, w_ff_out, (i, 1), mi=6, gi=4, n_tiles=post_tiles, seg_of=seg_of)
    return xs[:n_lat_rows].reshape(batch, seq_len, d)
```

```python
import functools
import math

import jax
import jax.numpy as jnp
from jax import lax
from jax.experimental import pallas as pl
from jax.experimental.pallas import tpu as pltpu

EPS = 1e-6
NEG_INF = -1e30
N_MOD = 9
GRID_W = 64
N_HEADS = 16
N_KV_HEADS = 4
HEAD_DIM = 128
GQA_GROUP = N_HEADS // N_KV_HEADS
ATT_BLOCK = 128
ROPE_BASE = 10000.0
SSM_GROUP_CH = 16
SSM_STATE = 64
SSM_CHUNK = 16
GROUPS_PER_STEP = 2

TOKEN_TILE = 512
FF_TILE = 512
GLU_TILE = 512
GLU_TOKEN_TILE = 256
MOD_TILE = 1024
MOD_ROWS = 16
VMEM_LIMIT = 56 * 1024 * 1024
LANES = 128

BF16 = jnp.bfloat16
F32 = jnp.float32


def _params(*sem):
    return pltpu.CompilerParams(dimension_semantics=sem, vmem_limit_bytes=VMEM_LIMIT)


def _rms(t, g):
    return t * lax.rsqrt(jnp.mean(t * t, axis=-1, keepdims=True) + EPS) * g


def _sigmoid(t):
    return 1.0 / (1.0 + jnp.exp(-t))


def _gelu_tanh(t):
    return 0.5 * t * (1.0 + jnp.tanh(math.sqrt(2.0 / math.pi) * (t + 0.044715 * (t * t * t))))


def _mod_kernel(ct_ref, w_ref, b_ref, o_ref, *, n_rows):
    w = w_ref[0]
    for r in range(n_rows):
        col = ct_ref[:, r:r + 1]
        col = col * _sigmoid(col)
        o_ref[0, r:r + 1, :] = jnp.sum(w * col, axis=0, keepdims=True) + b_ref[0]
    o_ref[0, n_rows:, :] = jnp.zeros((o_ref.shape[1] - n_rows, o_ref.shape[2]), F32)


def _modulation(cvec_t, ada_w, ada_b, n_rows):
    depth, d, nd = ada_w.shape
    tn = MOD_TILE
    return pl.pallas_call(
        functools.partial(_mod_kernel, n_rows=n_rows),
        out_shape=jax.ShapeDtypeStruct((depth, 8, nd), F32),
        grid=(depth, nd // tn),
        in_specs=[
            pl.BlockSpec((d, 8), lambda l, j: (0, 0)),
            pl.BlockSpec((1, d, tn), lambda l, j: (l, 0, j)),
            pl.BlockSpec((1, 1, tn), lambda l, j: (l, 0, j)),
        ],
        out_specs=pl.BlockSpec((1, 8, tn), lambda l, j: (l, 0, j)),
        compiler_params=_params("arbitrary", "arbitrary"),
        name="adaln_mod",
    )(cvec_t, ada_w, ada_b.reshape(depth, 1, nd))


def _ffn_kernel(x_ref, tail_ref, mod_ref, g_ref, wgu_ref, wo_ref, o_ref, h_scr, acc_scr,
                *, mi, gi, head_tiles, n_ff):
    def read_x():
        if head_tiles is None:
            return x_ref[...]
        return jnp.where(pl.program_id(0) < head_tiles, x_ref[...], tail_ref[...])

    def step(first, last):
        if first:
            h = _rms(read_x(), g_ref[gi:gi + 1, :]) * (1.0 + mod_ref[0, mi + 1:mi + 2, :]) + mod_ref[0, mi:mi + 1, :]
            h = h.astype(BF16)
            if not last:
                h_scr[...] = h
        else:
            h = h_scr[...]
        gate = jnp.dot(h, wgu_ref[0], preferred_element_type=F32)
        up = jnp.dot(h, wgu_ref[1], preferred_element_type=F32)
        act = (gate * _sigmoid(gate) * up).astype(BF16)
        out = jnp.dot(act, wo_ref[...], preferred_element_type=F32)
        if not first:
            out = acc_scr[...] + out
        if last:
            o_ref[...] = read_x() + 0.5 * mod_ref[0, mi + 2:mi + 3, :] * _rms(out, g_ref[gi + 1:gi + 2, :])
        else:
            acc_scr[...] = out

    f = pl.program_id(1)
    if n_ff == 1:
        step(True, True)
    else:
        pl.when(f == 0)(functools.partial(step, True, False))
        if n_ff > 2:
            pl.when((f > 0) & (f < n_ff - 1))(functools.partial(step, False, False))
        pl.when(f == n_ff - 1)(functools.partial(step, False, True))


def _ffn(xs, mod, g, w_in, w_out, which, *, mi, gi, n_tiles, seg_of, tail=None):
    d = xs.shape[1]
    ff = w_out.shape[2]
    tm, tf = TOKEN_TILE, FF_TILE
    nf = ff // tf
    layer, half = which
    if tail is None:
        head_tiles = None
        tokens = [xs, xs]
        token_specs = [pl.BlockSpec((tm, d), lambda i, f: (i, 0)), pl.BlockSpec((8, d), lambda i, f: (0, 0))]
    else:
        head_tiles = xs.shape[0] // tm
        tokens = [xs, tail]
        token_specs = [pl.BlockSpec((tm, d), lambda i, f: (jnp.minimum(i, head_tiles - 1), 0)),
                       pl.BlockSpec((tm, d), lambda i, f: (jnp.maximum(i - head_tiles, 0), 0),
                                    pipeline_mode=pl.Buffered(1))]
    return pl.pallas_call(
        functools.partial(_ffn_kernel, mi=mi, gi=gi, head_tiles=head_tiles, n_ff=nf),
        out_shape=jax.ShapeDtypeStruct((n_tiles * tm, d), F32),
        grid=(n_tiles, nf),
        in_specs=token_specs + [
            pl.BlockSpec((1, MOD_ROWS, d), lambda i, f: (seg_of(i), 0, 0)),
            pl.BlockSpec((8, d), lambda i, f: (0, 0)),
            pl.BlockSpec((None, None, None, 2, d, tf), lambda i, f: (layer, half, f, 0, 0, 0)),
            pl.BlockSpec((None, None, tf, d), lambda i, f: (layer, half, f, 0)),
        ],
        out_specs=pl.BlockSpec((tm, d), lambda i, f: (i, 0)),
        scratch_shapes=[pltpu.VMEM((tm, d), BF16), pltpu.VMEM((tm, d), F32)],
        compiler_params=_params("parallel", "arbitrary"),
        name="ffn_halfstep",
    )(*tokens, mod, g, w_in, w_out)


def _rope(t, cos, sin, even_quarter):
    partner = jnp.where(even_quarter, pltpu.roll(t, HEAD_DIM - 32, 1), pltpu.roll(t, 32, 1))
    return t * cos + partner * sin


def _qkv_kernel(x_ref, mod_ref, g_ref, w_ref, cos_ref, sin_ref, q_ref, k_ref, v_ref, *, mi, gi):
    h = _rms(x_ref[...], g_ref[gi:gi + 1, :]) * (1.0 + mod_ref[0, mi + 1:mi + 2, :]) + mod_ref[0, mi:mi + 1, :]
    qkv = jnp.dot(h.astype(BF16), w_ref[...], preferred_element_type=F32)
    cos, sin = cos_ref[...], sin_ref[...]
    lane = lax.broadcasted_iota(jnp.int32, cos.shape, 1)
    even_quarter = (lane & 32) == 0
    scale = HEAD_DIM ** -0.5
    nq, nk = N_HEADS * HEAD_DIM, N_KV_HEADS * HEAD_DIM
    for hh in range(N_HEADS):
        sl = slice(hh * HEAD_DIM, (hh + 1) * HEAD_DIM)
        q_ref[:, sl] = (_rope(qkv[:, sl], cos, sin, even_quarter) * scale).astype(BF16)
    for hh in range(N_KV_HEADS):
        sl = slice(hh * HEAD_DIM, (hh + 1) * HEAD_DIM)
        src = slice(nq + hh * HEAD_DIM, nq + (hh + 1) * HEAD_DIM)
        k_ref[:, sl] = _rope(qkv[:, src], cos, sin, even_quarter).astype(BF16)
    v_ref[...] = qkv[:, nq + nk:].astype(BF16)


def _qkv(xs, mod, g, w, cos, sin, *, mi, gi, n_tiles, seg_of, pos_of):
    d = xs.shape[1]
    tm = TOKEN_TILE
    nq, nk = N_HEADS * HEAD_DIM, N_KV_HEADS * HEAD_DIM
    rows = n_tiles * tm
    return pl.pallas_call(
        functools.partial(_qkv_kernel, mi=mi, gi=gi),
        out_shape=(jax.ShapeDtypeStruct((rows, nq), BF16), jax.ShapeDtypeStruct((rows, nk), BF16),
                   jax.ShapeDtypeStruct((rows, nk), BF16)),
        grid=(n_tiles,),
        in_specs=[
            pl.BlockSpec((tm, d), lambda i: (i, 0)),
            pl.BlockSpec((1, MOD_ROWS, d), lambda i: (seg_of(i), 0, 0)),
            pl.BlockSpec((8, d), lambda i: (0, 0)),
            pl.BlockSpec((d, nq + 2 * nk), lambda i: (0, 0), pipeline_mode=pl.Buffered(1)),
            pl.BlockSpec((tm, HEAD_DIM), lambda i: (pos_of(i), 0)),
            pl.BlockSpec((tm, HEAD_DIM), lambda i: (pos_of(i), 0)),
        ],
        out_specs=(pl.BlockSpec((tm, nq), lambda i: (i, 0)), pl.BlockSpec((tm, nk), lambda i: (i, 0)),
                   pl.BlockSpec((tm, nk), lambda i: (i, 0))),
        compiler_params=_params("parallel"),
        name="attn_qkv_rope",
    )(xs, mod, g, w, cos, sin)


def _proj_in_kernel(x_ref, mod_ref, g_ref, w_ref, o_ref, slab_scr, *, mi, gi):
    h = _rms(x_ref[...], g_ref[gi:gi + 1, :]) * (1.0 + mod_ref[0, mi + 1:mi + 2, :]) + mod_ref[0, mi:mi + 1, :]
    u = jnp.dot(h.astype(BF16), w_ref[...], preferred_element_type=F32)
    n_slab, tm, lanes = slab_scr.shape
    pair = GROUPS_PER_STEP * SSM_GROUP_CH
    per_slab = lanes // pair
    for k in range(n_slab):
        slab_scr[k] = u[:, k * lanes:(k + 1) * lanes]
    for t in range(SSM_CHUNK):
        for k in range(n_slab):
            rows = slab_scr[k, pl.ds(t, tm // SSM_CHUNK, stride=SSM_CHUNK), :]
            for jj in range(per_slab):
                o_ref[k * per_slab + jj, :, t * pair:(t + 1) * pair] = rows[:, jj * pair:(jj + 1) * pair].astype(o_ref.dtype)


def _proj_in(xs, mod, g, w, *, mi, gi, n_tiles, seg_of):
    d = xs.shape[1]
    tm = TOKEN_TILE
    pair = GROUPS_PER_STEP * SSM_GROUP_CH
    n_pairs = w.shape[1] // pair
    return pl.pallas_call(
        functools.partial(_proj_in_kernel, mi=mi, gi=gi),
        out_shape=jax.ShapeDtypeStruct((n_pairs, n_tiles * tm // SSM_CHUNK, SSM_CHUNK * pair), BF16),
        grid=(n_tiles,),
        in_specs=[
            pl.BlockSpec((tm, d), lambda i: (i, 0)),
            pl.BlockSpec((1, MOD_ROWS, d), lambda i: (seg_of(i), 0, 0)),
            pl.BlockSpec((8, d), lambda i: (0, 0)),
            pl.BlockSpec(w.shape, lambda i: (0, 0), pipeline_mode=pl.Buffered(1)),
        ],
        out_specs=pl.BlockSpec((n_pairs, tm // SSM_CHUNK, SSM_CHUNK * pair), lambda i: (0, i, 0)),
        scratch_shapes=[pltpu.VMEM((w.shape[1] // LANES, tm, LANES), F32)],
        compiler_params=_params("parallel"),
        name="ssm_in_proj",
    )(xs, mod, g, w)


def _proj_out_kernel(a_ref, x_ref, mod_ref, g_ref, w_ref, o_ref, *, mi, gi):
    out = jnp.dot(a_ref[...], w_ref[...], preferred_element_type=F32)
    o_ref[...] = x_ref[...] + mod_ref[0, mi:mi + 1, :] * _rms(out, g_ref[gi:gi + 1, :])


def _proj_out(a, xs, mod, g, w, *, mi, gi, n_tiles, seg_of):
    d = xs.shape[1]
    tm = TOKEN_TILE
    return pl.pallas_call(
        functools.partial(_proj_out_kernel, mi=mi, gi=gi),
        out_shape=jax.ShapeDtypeStruct((n_tiles * tm, d), F32),
        grid=(n_tiles,),
        in_specs=[
            pl.BlockSpec((tm, a.shape[1]), lambda i: (i, 0)),
            pl.BlockSpec((tm, d), lambda i: (i, 0)),
            pl.BlockSpec((1, MOD_ROWS, d), lambda i: (seg_of(i), 0, 0)),
            pl.BlockSpec((8, d), lambda i: (0, 0)),
            pl.BlockSpec(w.shape, lambda i: (0, 0), pipeline_mode=pl.Buffered(1)),
        ],
        out_specs=pl.BlockSpec((tm, d), lambda i: (i, 0)),
        compiler_params=_params("parallel"),
        name="attn_out_proj",
    )(a, xs, mod, g, w)


def _attn_kernel(sink_ref, q_ref, kp_ref, ko_ref, kn_ref, vp_ref, vo_ref, vn_ref, kc_ref, vc_ref, o_ref,
                 *, n_blk, seq_len):
    n = pl.program_id(1)
    blk = ATT_BLOCK
    c_len = kc_ref.shape[0]
    n_keys = 3 * blk + c_len
    i = lax.broadcasted_iota(jnp.int32, (GQA_GROUP * blk, n_keys), 0) & (blk - 1)
    j = lax.broadcasted_iota(jnp.int32, (GQA_GROUP * blk, n_keys), 1)
    rel = j - i
    kpos = (n - 1) * blk + j
    kpos_end = jnp.where(n < n_blk, seq_len, 0)
    band = (rel >= 0) & (rel <= 2 * blk) & (kpos >= 0) & (kpos < kpos_end)
    mask = band | (j >= 3 * blk)
    row = lax.broadcasted_iota(jnp.int32, (GQA_GROUP * blk, 1), 0)
    for h in range(N_KV_HEADS):
        hs = slice(h * HEAD_DIM, (h + 1) * HEAD_DIM)
        qs = jnp.concatenate(
            [q_ref[:, (h * GQA_GROUP + gq) * HEAD_DIM:(h * GQA_GROUP + gq + 1) * HEAD_DIM] for gq in range(GQA_GROUP)],
            axis=0)
        kb = jnp.concatenate([kp_ref[:, hs], ko_ref[:, hs], kn_ref[:, hs], kc_ref[:, hs]], axis=0)
        vb = jnp.concatenate([vp_ref[:, hs], vo_ref[:, hs], vn_ref[:, hs], vc_ref[:, hs]], axis=0)
        s = lax.dot_general(qs, kb, (((1,), (1,)), ((), ())), preferred_element_type=F32)
        s = jnp.where(mask, s, NEG_INF)
        sink = jnp.full((GQA_GROUP * blk, 1), sink_ref[h * GQA_GROUP], F32)
        for gq in range(1, GQA_GROUP):
            sink = jnp.where(row >= gq * blk, sink_ref[h * GQA_GROUP + gq], sink)
        m = jnp.maximum(jnp.max(s, axis=-1, keepdims=True), sink)
        p = jnp.exp(s - m)
        denom = jnp.sum(p, axis=-1, keepdims=True) + jnp.exp(sink - m)
        o = jnp.dot(p.astype(BF16), vb, preferred_element_type=F32) / denom
        for gq in range(GQA_GROUP):
            col = (h * GQA_GROUP + gq) * HEAD_DIM
            o_ref[:, col:col + HEAD_DIM] = o[gq * blk:(gq + 1) * blk, :].astype(o_ref.dtype)


def _attention(q, k, v, sink, *, batch, seq_len, c_len):
    blk = ATT_BLOCK
    n_blk, c_blk = seq_len // blk, c_len // blk
    nq, nk = q.shape[1], k.shape[1]
    lat_blocks = batch * n_blk
    ctx_block0 = batch * seq_len // c_len

    def q_idx(b, n):
        return jnp.where(n < n_blk, b * n_blk + n, lat_blocks + b * c_blk + (n - n_blk))

    def band_idx(off):
        def idx(b, n, s):
            nn = jnp.clip(n + off, 0, n_blk - 1)
            return (b * n_blk + nn, 0)
        return idx

    kv_spec = [pl.BlockSpec((blk, nk), band_idx(off)) for off in (-1, 0, 1)]
    ctx_spec = pl.BlockSpec((c_len, nk), lambda b, n, s: (ctx_block0 + b, 0))
    grid_spec = pltpu.PrefetchScalarGridSpec(
        num_scalar_prefetch=1,
        grid=(batch, n_blk + c_blk),
        in_specs=[pl.BlockSpec((blk, nq), lambda b, n, s: (q_idx(b, n), 0))] + kv_spec + kv_spec + [ctx_spec, ctx_spec],
        out_specs=pl.BlockSpec((blk, nq), lambda b, n, s: (q_idx(b, n), 0)),
    )
    return pl.pallas_call(
        functools.partial(_attn_kernel, n_blk=n_blk, seq_len=seq_len),
        out_shape=jax.ShapeDtypeStruct(q.shape, BF16),
        grid_spec=grid_spec,
        compiler_params=_params("parallel", "arbitrary"),
        name="window_gqa",
    )(sink, q, k, k, k, v, v, v, k, v)


def _s5_kernel(u_ref, t_ref, ws_ref, wc_ref, al_ref, d_ref, y_ref, s_scr, h_scr, *, batch, n_lat, n_ctx):
    half = GROUPS_PER_STEP * SSM_STATE
    u = u_ref[0]
    s_scr[...] = jnp.dot(u, ws_ref[0], preferred_element_type=F32)
    sub = 8
    ctx0 = batch * n_lat
    zero = jnp.zeros((sub, half), F32)
    sub_row = lax.broadcasted_iota(jnp.int32, (sub, half), 0)

    def rep(v):
        return jnp.broadcast_to(v, (sub, half))

    def tables(row, fwd):
        pw = _powers(al_ref[0, row:row + 1, :], al_ref[0, row + 1:row + 2, :], sub)
        levels = []
        for k in (1, 2, 4):
            ok = (sub_row >= k) if fwd else (sub_row <= sub - 1 - k)
            levels.append((k if fwd else sub - k, jnp.where(ok, rep(pw[k][0]), 0.0), jnp.where(ok, rep(pw[k][1]), 0.0)))
        c_re, c_im = zero, zero
        for r in range(sub):
            e = r if fwd else sub - 1 - r
            c_re = jnp.where(sub_row == r, rep(pw[e][0]), c_re)
            c_im = jnp.where(sub_row == r, rep(pw[e][1]), c_im)
        inner = (sub_row >= 1) if fwd else (sub_row <= sub - 2)
        return dict(levels=levels, carry_w=(c_re, c_im), block_w=(rep(pw[sub][0]), rep(pw[sub][1])),
                    shift=1 if fwd else sub - 1, inner=inner, last=sub - 1 if fwd else 0)

    tbl_f, tbl_b = tables(0, True), tables(2, False)

    def scan_block(h, tbl, base, col, keep):
        h_re, h_im = h
        base = pl.multiple_of(base, sub)
        x_re = s_scr[pl.ds(base, sub), col:col + half]
        x_im = s_scr[pl.ds(base, sub), col + half:col + 2 * half]
        for shift, c_re, c_im in tbl["levels"]:
            r_re, r_im = pltpu.roll(x_re, shift, 0), pltpu.roll(x_im, shift, 0)
            x_re, x_im = x_re + c_re * r_re - c_im * r_im, x_im + c_re * r_im + c_im * r_re
        if keep:
            w_re, w_im = tbl["carry_w"]
            p_re = jnp.where(tbl["inner"], pltpu.roll(x_re, tbl["shift"], 0), 0.0)
            p_im = jnp.where(tbl["inner"], pltpu.roll(x_im, tbl["shift"], 0), 0.0)
            h_scr[pl.ds(base, sub), col:col + half] = w_re * h_re - w_im * h_im + p_re
            h_scr[pl.ds(base, sub), col + half:col + 2 * half] = w_re * h_im + w_im * h_re + p_im
        b_re, b_im = tbl["block_w"]
        last = tbl["last"]
        t_re, t_im = rep(x_re[last:last + 1, :]), rep(x_im[last:last + 1, :])
        return b_re * h_re - b_im * h_im + t_re, b_re * h_im + b_im * h_re + t_im

    def make_step(first_row, n_rows, keep):
        def step(m, carry):
            out = []
            for b in range(batch):
                hf, hb = carry[2 * b], carry[2 * b + 1]
                lo = first_row + b * n_rows
                hf = scan_block(hf, tbl_f, lo + m * sub, 0, keep)
                hb = scan_block(hb, tbl_b, lo + n_rows - sub - m * sub, 2 * half, keep)
                out += [hf, hb]
            return tuple(out)
        return step

    carry = lax.fori_loop(0, n_ctx // sub, make_step(ctx0, n_ctx, False), ((zero, zero),) * (2 * batch))
    lax.fori_loop(0, n_lat // sub, make_step(0, n_lat, True), carry)

    rows = batch * n_lat
    u_lat = u[:rows]
    y_ref[0] = (jnp.dot(u_lat, t_ref[0], preferred_element_type=F32)
                + jnp.dot(h_scr[...].astype(BF16), wc_ref[0], preferred_element_type=F32)
                + d_ref[0] * u_lat.astype(F32))


def _s5_core(ug, toep, ws, wc, al, d_lanes, *, batch, n_lat, n_ctx):
    n_steps, chunks, width = ug.shape
    assert n_lat % 8 == 0 and n_ctx % 8 == 0
    rows = batch * n_lat
    return pl.pallas_call(
        functools.partial(_s5_kernel, batch=batch, n_lat=n_lat, n_ctx=n_ctx),
        out_shape=jax.ShapeDtypeStruct((n_steps, rows, width), F32),
        grid=(n_steps,),
        in_specs=[
            pl.BlockSpec((1, chunks, width), lambda j: (j, 0, 0)),
            pl.BlockSpec((1,) + toep.shape[1:], lambda j: (j, 0, 0)),
            pl.BlockSpec((1,) + ws.shape[1:], lambda j: (j, 0, 0)),
            pl.BlockSpec((1,) + wc.shape[1:], lambda j: (j, 0, 0)),
            pl.BlockSpec((1,) + al.shape[1:], lambda j: (j, 0, 0)),
            pl.BlockSpec((1, 1, width), lambda j: (j, 0, 0)),
        ],
        out_specs=pl.BlockSpec((1, rows, width), lambda j: (j, 0, 0)),
        scratch_shapes=[pltpu.VMEM((chunks, 4 * GROUPS_PER_STEP * SSM_STATE), F32),
                        pltpu.VMEM((rows, 4 * GROUPS_PER_STEP * SSM_STATE), F32)],
        compiler_params=_params("parallel"),
        name="s5_chunked",
    )(ug, toep, ws, wc, al, d_lanes)


def _cmul(a_re, a_im, b_re, b_im):
    return a_re * b_re - a_im * b_im, a_re * b_im + a_im * b_re


def _zoh(a_re, a_im, log_dt):
    dt = jnp.exp(log_dt)
    mag = jnp.exp(a_re * dt)
    ab_re, ab_im = mag * jnp.cos(a_im * dt), mag * jnp.sin(a_im * dt)
    den = a_re * a_re + a_im * a_im
    x_re, x_im = ab_re - 1.0, ab_im
    return ab_re, ab_im, (x_re * a_re + x_im * a_im) / den, (x_im * a_re - x_re * a_im) / den


def _powers(ab_re, ab_im, n):
    out = [(jnp.ones_like(ab_re), jnp.zeros_like(ab_im))]
    for _ in range(n):
        out.append(_cmul(out[-1][0], out[-1][1], ab_re, ab_im))
    return out


def _s5_op_kernel(row_ref, col_ref, bt_ref, ct_ref, t_ref, ws_ref, wc_ref, al_ref):
    L = SSM_CHUNK
    pair = ct_ref.shape[-1]
    hi = lax.Precision.HIGHEST
    inject, taps, read = [], [], []
    for d in range(2):
        ab_re, ab_im, f_re, f_im = _zoh(*(row_ref[0, 3 * d + r:3 * d + r + 1, :] for r in range(3)))
        pw_row = _powers(ab_re, ab_im, L)
        bb_re, bb_im = _cmul(f_re, f_im, bt_ref[0, d, 0], bt_ref[0, d, 1])
        cb_re, cb_im, _, _ = _zoh(*(col_ref[0, :, 3 * d + r:3 * d + r + 1] for r in range(3)))
        pw_col = _powers(cb_re, cb_im, L)
        ca = [_cmul(ct_ref[0, d, 0], ct_ref[0, d, 1], pr, pi) for pr, pi in pw_col]
        order = range(L - 1, -1, -1) if d == 0 else range(L)
        inject.append([_cmul(pw_row[e][0], pw_row[e][1], bb_re, bb_im) for e in order])
        korder = range(L) if d == 0 else range(L - 1, -1, -1)
        cat_re = jnp.concatenate([ca[k][0] for k in korder], axis=1)
        cat_im = jnp.concatenate([ca[k][1] for k in korder], axis=1)
        taps.append(jnp.dot(bb_re, cat_re, precision=hi, preferred_element_type=F32)
                    - jnp.dot(bb_im, cat_im, precision=hi, preferred_element_type=F32))
        eorder = range(1, L + 1) if d == 0 else range(L, 0, -1)
        read.append((jnp.concatenate([ca[e][0] for e in eorder], axis=1),
                     jnp.concatenate([-ca[e][1] for e in eorder], axis=1)))
        al_ref[0, 2 * d:2 * d + 1, :] = pw_row[L][0]
        al_ref[0, 2 * d + 1:2 * d + 2, :] = pw_row[L][1]
    al_ref[0, 4:, :] = jnp.zeros((4, al_ref.shape[2]), F32)

    for s in range(L):
        ws_ref[0, s * pair:(s + 1) * pair, :] = jnp.concatenate(
            [inject[0][s][0], inject[0][s][1], inject[1][s][0], inject[1][s][1]], axis=1).astype(ws_ref.dtype)
    rows = read[0][0].shape[0]
    for q, blk in enumerate((read[0][0], read[0][1], read[1][0], read[1][1])):
        wc_ref[0, q * rows:(q + 1) * rows, :] = blk.astype(wc_ref.dtype)
    k_f, k_b = taps
    keep = (L - 1) * pair
    by_lag = jnp.concatenate([k_b[:, :keep], k_b[:, keep:] + k_f[:, :pair], k_f[:, pair:]], axis=1)
    for s in range(L):
        t_ref[0, s * pair:(s + 1) * pair, :] = by_lag[:, (L - 1 - s) * pair:(2 * L - 1 - s) * pair].astype(t_ref.dtype)


def _s5_operators(a_re, a_im, log_dt, b_re, b_im, c_re, c_im):
    n_g, n_p = a_re.shape[1], a_re.shape[2]
    gc = b_re.shape[-1]
    n_j, per = n_g // GROUPS_PER_STEP, GROUPS_PER_STEP
    states, pair, width = per * n_p, per * gc, SSM_CHUNK * per * gc
    lam = jnp.stack([a_re, a_im, jnp.broadcast_to(log_dt[..., None], a_re.shape)], axis=1)
    lam = jnp.transpose(lam.reshape(6, n_j, states), (1, 0, 2))
    lam = jnp.pad(lam, ((0, 0), (0, 2), (0, 0)))
    eye = jnp.eye(per, dtype=F32)
    bt = jnp.stack([b_re, b_im], axis=1).reshape(2, 2, n_j, per, n_p, gc)
    bt = jnp.einsum('drjgpc,gh->jdrgchp', bt, eye).reshape(n_j, 2, 2, pair, states)
    ct = jnp.stack([c_re, c_im], axis=1).reshape(2, 2, n_j, per, gc, n_p)
    ct = jnp.einsum('drjgcp,gh->jdrgphc', ct, eye).reshape(n_j, 2, 2, states, pair)
    mat = jax.ShapeDtypeStruct((n_j, width, width), BF16)
    return pl.pallas_call(
        _s5_op_kernel,
        out_shape=(mat, mat, mat, jax.ShapeDtypeStruct((n_j, 8, states), F32)),
        grid=(n_j,),
        in_specs=[
            pl.BlockSpec((1, 8, states), lambda j: (j, 0, 0)),
            pl.BlockSpec((1, states, 8), lambda j: (j, 0, 0)),
            pl.BlockSpec((1, 2, 2, pair, states), lambda j: (j, 0, 0, 0, 0)),
            pl.BlockSpec((1, 2, 2, states, pair), lambda j: (j, 0, 0, 0, 0)),
        ],
        out_specs=(pl.BlockSpec((1, width, width), lambda j: (j, 0, 0)),) * 3
        + (pl.BlockSpec((1, 8, states), lambda j: (j, 0, 0)),),
        compiler_params=_params("parallel"),
        name="s5_operators",
    )(lam, jnp.transpose(lam, (0, 2, 1)), bt, ct)


def _ssm_out_kernel(y_ref, x_ref, mod_ref, g_ref, w_ref, o_ref, slab_scr, act_even, act_odd, out_scr, *, mi, gi):
    i = pl.program_id(0)

    @pl.when(i == 0)
    def _():
        act_odd[...] = jnp.zeros_like(act_odd)

    @pl.when(i % 2 == 0)
    def _():
        _ssm_out_step(y_ref, x_ref, mod_ref, g_ref, w_ref, o_ref, slab_scr, act_even, act_odd, out_scr, mi=mi, gi=gi)

    @pl.when(i % 2 == 1)
    def _():
        _ssm_out_step(y_ref, x_ref, mod_ref, g_ref, w_ref, o_ref, slab_scr, act_odd, act_even, out_scr, mi=mi, gi=gi)


def _ssm_out_step(y_ref, x_ref, mod_ref, g_ref, w_ref, o_ref, slab_scr, act_new, act_old, out_scr, *, mi, gi):
    n_slab, tm, lanes = slab_scr.shape
    pair = GROUPS_PER_STEP * SSM_GROUP_CH
    per_slab = lanes // pair
    for t in range(SSM_CHUNK):
        for k in range(n_slab):
            slab_scr[k, pl.ds(t, tm // SSM_CHUNK, stride=SSM_CHUNK), :] = jnp.concatenate(
                [y_ref[k * per_slab + jj, :, t * pair:(t + 1) * pair] for jj in range(per_slab)], axis=-1)
    for k in range(n_slab):
        act_new[:, k * lanes:(k + 1) * lanes] = _gelu_tanh(slab_scr[k]).astype(BF16)

    n_c, _, tn = out_scr.shape
    d = n_c * tn
    a = act_old[...]
    ssq = jnp.zeros((tm, 1), F32)
    for cc in range(n_c):
        val = jnp.dot(a, w_ref[:, cc * tn:(cc + 1) * tn], preferred_element_type=F32)
        gate = jnp.dot(a, w_ref[:, d + cc * tn:d + (cc + 1) * tn], preferred_element_type=F32)
        blk = val * _sigmoid(gate)
        out_scr[cc] = blk
        ssq = ssq + jnp.sum(blk * blk, axis=-1, keepdims=True)
    inv = lax.rsqrt(ssq / d + EPS)
    for cc in range(n_c):
        sl = slice(cc * tn, (cc + 1) * tn)
        o_ref[:, sl] = x_ref[:, sl] + mod_ref[0, mi:mi + 1, sl] * (out_scr[cc] * inv * g_ref[gi:gi + 1, sl])


def _ssm_out(yg, xs, mod, g, w_glu, *, mi, gi, n_rows, seg_of_row):
    d = xs.shape[1]
    tm, tn = GLU_TOKEN_TILE, GLU_TILE
    n_c = d // tn
    n_pairs, _, width = yg.shape
    n_tiles = n_rows // tm

    def prev(i):
        return jnp.maximum(i - 1, 0)

    return pl.pallas_call(
        functools.partial(_ssm_out_kernel, mi=mi, gi=gi),
        out_shape=jax.ShapeDtypeStruct((n_rows, d), F32),
        grid=(n_tiles + 1,),
        in_specs=[
            pl.BlockSpec((n_pairs, tm // SSM_CHUNK, width), lambda i: (0, jnp.minimum(i, n_tiles - 1), 0)),
            pl.BlockSpec((tm, d), lambda i: (prev(i), 0)),
            pl.BlockSpec((1, MOD_ROWS, d), lambda i: (seg_of_row(prev(i) * tm), 0, 0)),
            pl.BlockSpec((8, d), lambda i: (0, 0)),
            pl.BlockSpec(w_glu.shape, lambda i: (0, 0), pipeline_mode=pl.Buffered(1)),
        ],
        out_specs=pl.BlockSpec((tm, d), lambda i: (prev(i), 0)),
        scratch_shapes=[pltpu.VMEM((d // LANES, tm, LANES), F32), pltpu.VMEM((tm, d), BF16),
                        pltpu.VMEM((tm, d), BF16), pltpu.VMEM((n_c, tm, tn), F32)],
        compiler_params=_params("arbitrary"),
        name="ssm_glu_out",
    )(yg, xs, mod, g, w_glu)


def _rope_tables(seq_len, extra_rows):
    rows = seq_len // GRID_W
    row = jnp.repeat(jnp.arange(rows, dtype=F32), GRID_W)
    col = jnp.tile(jnp.arange(GRID_W, dtype=F32), rows)
    axis_dim = HEAD_DIM // 2
    inv_freq = ROPE_BASE ** (-jnp.arange(0, axis_dim, 2, dtype=F32) / axis_dim)
    ang_r, ang_c = row[:, None] * inv_freq, col[:, None] * inv_freq
    cos = jnp.concatenate([jnp.cos(ang_r)] * 2 + [jnp.cos(ang_c)] * 2, axis=-1)
    sin = jnp.concatenate([-jnp.sin(ang_r), jnp.sin(ang_r), -jnp.sin(ang_c), jnp.sin(ang_c)], axis=-1)
    cos = jnp.concatenate([cos, jnp.ones((extra_rows, HEAD_DIM), F32)], axis=0)
    sin = jnp.concatenate([sin, jnp.zeros((extra_rows, HEAD_DIM), F32)], axis=0)
    return cos, sin


def kernel(x, c, ctx, c_ctx, ada_w, ada_b, norm_g, ffn_w_in, ffn_w_out, attn_w_in, attn_w_out, attn_sink,
           ssm_w_in, ssm_a_re, ssm_a_im, ssm_log_dt, ssm_b_re, ssm_b_im, ssm_c_re, ssm_c_im, ssm_d, ssm_w_glu):
    batch, seq_len, d = x.shape
    c_len = ctx.shape[1]
    depth = ada_w.shape[0]
    tm = TOKEN_TILE
    n_lat_rows, n_ctx_rows = batch * seq_len, batch * c_len
    assert seq_len % tm == 0 and n_ctx_rows % tm == 0 and n_lat_rows % c_len == 0
    assert seq_len % ATT_BLOCK == 0 and c_len % ATT_BLOCK == 0 and seq_len % GRID_W == 0
    tiles_per_batch = seq_len // tm
    lat_tiles = n_lat_rows // tm
    all_tiles = lat_tiles + n_ctx_rows // tm

    def seg_of(i):
        return jnp.minimum(i // tiles_per_batch, batch)

    def pos_of(i):
        return jnp.where(i < lat_tiles, i % tiles_per_batch, tiles_per_batch + (i - lat_tiles))

    def seg_of_row(r):
        return jnp.minimum(r // seq_len, batch)

    xs, tail = x.reshape(n_lat_rows, d), ctx.reshape(n_ctx_rows, d)
    n_ff = ffn_w_out.shape[2] // FF_TILE
    w_ff_in = jnp.transpose(ffn_w_in.astype(BF16).reshape(depth, 2, d, 2, n_ff, FF_TILE), (0, 1, 4, 3, 2, 5))
    w_ff_out = ffn_w_out.astype(BF16)

    n_cond = batch + 1
    cvec = jnp.concatenate([c, c_ctx[None], jnp.zeros((8 - n_cond, d), F32)], axis=0)
    mods = _modulation(cvec.T, ada_w, ada_b, n_cond)
    mods = mods[:, :n_cond].reshape(depth, n_cond, N_MOD, d)
    mods = jnp.pad(mods, ((0, 0), (0, 0), (0, MOD_ROWS - N_MOD), (0, 0)))
    gains = jnp.pad(norm_g, ((0, 0), (0, 8 - norm_g.shape[1]), (0, 0)))

    cos, sin = _rope_tables(seq_len, n_ctx_rows)

    for i in range(depth):
        need_ctx = i < depth - 1
        mod, g = mods[i], gains[i]
        j = i // 2
        xs = _ffn(xs, mod, g, w_ff_in, w_ff_out, (i, 0), mi=0, gi=0, n_tiles=all_tiles, seg_of=seg_of, tail=tail)
        tail = None
        post_tiles = all_tiles if need_ctx else lat_tiles
        if i % 2 == 0:
            q, k, v = _qkv(xs, mod, g, attn_w_in[j].astype(BF16), cos, sin,
                           mi=3, gi=2, n_tiles=all_tiles, seg_of=seg_of, pos_of=pos_of)
            o = _attention(q, k, v, attn_sink[j], batch=batch, seq_len=seq_len, c_len=c_len)
            xs = _proj_out(o, xs, mod, g, attn_w_out[j].astype(BF16),
                           mi=5, gi=3, n_tiles=post_tiles, seg_of=seg_of)
        else:
            if need_ctx:
                raise NotImplementedError("context output of the S5 mixer is only needed when another layer follows")
            ug = _proj_in(xs, mod, g, ssm_w_in[j].astype(BF16), mi=3, gi=2, n_tiles=all_tiles, seg_of=seg_of)
            L = SSM_CHUNK
            pair = GROUPS_PER_STEP * SSM_GROUP_CH
            toep, ws, wc, al = _s5_operators(ssm_a_re[j], ssm_a_im[j], ssm_log_dt[j], ssm_b_re[j], ssm_b_im[j],
                                             ssm_c_re[j], ssm_c_im[j])
            d_lanes = jnp.tile(ssm_d[j].reshape(d // pair, 1, pair), (1, 1, L))
            yg = _s5_core(ug, toep, ws, wc, al, d_lanes, batch=batch, n_lat=seq_len // L, n_ctx=c_len // L)
            xs = _ssm_out(yg, xs, mod, g, ssm_w_glu[j].astype(BF16),
                          mi=5, gi=3, n_rows=post_tiles * tm, seg_of_row=seg_of_row)
        xs = _ffn(xs, mod, g, w_ff_in, w_ff_out, (i, 1), mi=6, gi=4, n_tiles=post_tiles, seg_of=seg_of)
    return xs[:n_lat_rows].reshape(batch, seq_len, d)
```

```python
import functools
import math

import jax
import jax.numpy as jnp
from jax import lax
from jax.experimental import pallas as pl
from jax.experimental.pallas import tpu as pltpu

EPS = 1e-6
NEG_INF = -1e30
N_MOD = 9
GRID_W = 64
N_HEADS = 16
N_KV_HEADS = 4
HEAD_DIM = 128
GQA_GROUP = N_HEADS // N_KV_HEADS
ATT_BLOCK = 128
ROPE_BASE = 10000.0
SSM_GROUP_CH = 16
SSM_STATE = 64
SSM_CHUNK = 16
GROUPS_PER_STEP = 2

TOKEN_TILE = 512
FF_TILE = 512
GLU_TILE = 512
GLU_TOKEN_TILE = 256
MOD_TILE = 1024
MOD_ROWS = 16
VMEM_LIMIT = 56 * 1024 * 1024
LANES = 128

BF16 = jnp.bfloat16
F32 = jnp.float32


def _params(*sem):
    return pltpu.CompilerParams(dimension_semantics=sem, vmem_limit_bytes=VMEM_LIMIT)


def _rms(t, g):
    return t * lax.rsqrt(jnp.mean(t * t, axis=-1, keepdims=True) + EPS) * g


def _sigmoid(t):
    return 1.0 / (1.0 + jnp.exp(-t))


def _gelu_tanh(t):
    return 0.5 * t * (1.0 + jnp.tanh(math.sqrt(2.0 / math.pi) * (t + 0.044715 * (t * t * t))))


def _mod_kernel(ct_ref, w_ref, b_ref, o_ref, *, n_rows):
    w = w_ref[0]
    for r in range(n_rows):
        col = ct_ref[:, r:r + 1]
        col = col * _sigmoid(col)
        o_ref[0, r:r + 1, :] = jnp.sum(w * col, axis=0, keepdims=True) + b_ref[0]
    o_ref[0, n_rows:, :] = jnp.zeros((o_ref.shape[1] - n_rows, o_ref.shape[2]), F32)


def _modulation(cvec_t, ada_w, ada_b, n_rows):
    depth, d, nd = ada_w.shape
    tn = MOD_TILE
    return pl.pallas_call(
        functools.partial(_mod_kernel, n_rows=n_rows),
        out_shape=jax.ShapeDtypeStruct((depth, 8, nd), F32),
        grid=(depth, nd // tn),
        in_specs=[
            pl.BlockSpec((d, 8), lambda l, j: (0, 0)),
            pl.BlockSpec((1, d, tn), lambda l, j: (l, 0, j)),
            pl.BlockSpec((1, 1, tn), lambda l, j: (l, 0, j)),
        ],
        out_specs=pl.BlockSpec((1, 8, tn), lambda l, j: (l, 0, j)),
        compiler_params=_params("arbitrary", "arbitrary"),
        name="adaln_mod",
    )(cvec_t, ada_w, ada_b.reshape(depth, 1, nd))


def _ffn_kernel(x_ref, tail_ref, mod_ref, g_ref, wg_ref, wu_ref, wo_ref, o_ref, h_scr, acc_scr,
                *, mi, gi, head_tiles, n_ff):
    def read_x():
        if head_tiles is None:
            return x_ref[...]
        return jnp.where(pl.program_id(0) < head_tiles, x_ref[...], tail_ref[...])

    def step(first, last):
        if first:
            h = _rms(read_x(), g_ref[gi:gi + 1, :]) * (1.0 + mod_ref[0, mi + 1:mi + 2, :]) + mod_ref[0, mi:mi + 1, :]
            h = h.astype(BF16)
            if not last:
                h_scr[...] = h
        else:
            h = h_scr[...]
        gate = jnp.dot(h, wg_ref[...], preferred_element_type=F32)
        up = jnp.dot(h, wu_ref[...], preferred_element_type=F32)
        act = (gate * _sigmoid(gate) * up).astype(BF16)
        out = jnp.dot(act, wo_ref[...], preferred_element_type=F32)
        if not first:
            out = acc_scr[...] + out
        if last:
            o_ref[...] = read_x() + 0.5 * mod_ref[0, mi + 2:mi + 3, :] * _rms(out, g_ref[gi + 1:gi + 2, :])
        else:
            acc_scr[...] = out

    f = pl.program_id(1)
    if n_ff == 1:
        step(True, True)
    else:
        pl.when(f == 0)(functools.partial(step, True, False))
        if n_ff > 2:
            pl.when((f > 0) & (f < n_ff - 1))(functools.partial(step, False, False))
        pl.when(f == n_ff - 1)(functools.partial(step, False, True))


def _ffn(xs, mod, g, w_in, w_out, which, *, mi, gi, n_tiles, seg_of, tail=None):
    d = xs.shape[1]
    ff = w_out.shape[2]
    tm, tf = TOKEN_TILE, FF_TILE
    nf = ff // tf
    layer, half = which
    if tail is None:
        head_tiles = None
        tokens = [xs, xs]
        token_specs = [pl.BlockSpec((tm, d), lambda i, f: (i, 0)), pl.BlockSpec((8, d), lambda i, f: (0, 0))]
    else:
        head_tiles = xs.shape[0] // tm
        tokens = [xs, tail]
        token_specs = [pl.BlockSpec((tm, d), lambda i, f: (jnp.minimum(i, head_tiles - 1), 0)),
                       pl.BlockSpec((tm, d), lambda i, f: (jnp.maximum(i - head_tiles, 0), 0),
                                    pipeline_mode=pl.Buffered(1))]
    return pl.pallas_call(
        functools.partial(_ffn_kernel, mi=mi, gi=gi, head_tiles=head_tiles, n_ff=nf),
        out_shape=jax.ShapeDtypeStruct((n_tiles * tm, d), F32),
        grid=(n_tiles, nf),
        in_specs=token_specs + [
            pl.BlockSpec((1, MOD_ROWS, d), lambda i, f: (seg_of(i), 0, 0)),
            pl.BlockSpec((8, d), lambda i, f: (0, 0)),
            pl.BlockSpec((None, None, d, tf), lambda i, f: (layer, half, 0, f)),
            pl.BlockSpec((None, None, d, tf), lambda i, f: (layer, half, 0, nf + f)),
            pl.BlockSpec((None, None, tf, d), lambda i, f: (layer, half, f, 0)),
        ],
        out_specs=pl.BlockSpec((tm, d), lambda i, f: (i, 0)),
        scratch_shapes=[pltpu.VMEM((tm, d), BF16), pltpu.VMEM((tm, d), F32)],
        compiler_params=_params("parallel", "arbitrary"),
        name="ffn_halfstep",
    )(*tokens, mod, g, w_in, w_in, w_out)


def _rope(t, cos, sin, even_quarter):
    partner = jnp.where(even_quarter, pltpu.roll(t, HEAD_DIM - 32, 1), pltpu.roll(t, 32, 1))
    return t * cos + partner * sin


def _qkv_kernel(x_ref, mod_ref, g_ref, w_ref, cos_ref, sin_ref, q_ref, k_ref, v_ref, *, mi, gi):
    h = _rms(x_ref[...], g_ref[gi:gi + 1, :]) * (1.0 + mod_ref[0, mi + 1:mi + 2, :]) + mod_ref[0, mi:mi + 1, :]
    qkv = jnp.dot(h.astype(BF16), w_ref[...], preferred_element_type=F32)
    cos, sin = cos_ref[...], sin_ref[...]
    lane = lax.broadcasted_iota(jnp.int32, cos.shape, 1)
    even_quarter = (lane & 32) == 0
    scale = HEAD_DIM ** -0.5
    nq, nk = N_HEADS * HEAD_DIM, N_KV_HEADS * HEAD_DIM
    for hh in range(N_HEADS):
        sl = slice(hh * HEAD_DIM, (hh + 1) * HEAD_DIM)
        q_ref[:, sl] = (_rope(qkv[:, sl], cos, sin, even_quarter) * scale).astype(BF16)
    for hh in range(N_KV_HEADS):
        sl = slice(hh * HEAD_DIM, (hh + 1) * HEAD_DIM)
        src = slice(nq + hh * HEAD_DIM, nq + (hh + 1) * HEAD_DIM)
        k_ref[:, sl] = _rope(qkv[:, src], cos, sin, even_quarter).astype(BF16)
    v_ref[...] = qkv[:, nq + nk:].astype(BF16)


def _qkv(xs, mod, g, w, cos, sin, *, mi, gi, n_tiles, seg_of, pos_of):
    d = xs.shape[1]
    tm = TOKEN_TILE
    nq, nk = N_HEADS * HEAD_DIM, N_KV_HEADS * HEAD_DIM
    rows = n_tiles * tm
    return pl.pallas_call(
        functools.partial(_qkv_kernel, mi=mi, gi=gi),
        out_shape=(jax.ShapeDtypeStruct((rows, nq), BF16), jax.ShapeDtypeStruct((rows, nk), BF16),
                   jax.ShapeDtypeStruct((rows, nk), BF16)),
        grid=(n_tiles,),
        in_specs=[
            pl.BlockSpec((tm, d), lambda i: (i, 0)),
            pl.BlockSpec((1, MOD_ROWS, d), lambda i: (seg_of(i), 0, 0)),
            pl.BlockSpec((8, d), lambda i: (0, 0)),
            pl.BlockSpec((d, nq + 2 * nk), lambda i: (0, 0), pipeline_mode=pl.Buffered(1)),
            pl.BlockSpec((tm, HEAD_DIM), lambda i: (pos_of(i), 0)),
            pl.BlockSpec((tm, HEAD_DIM), lambda i: (pos_of(i), 0)),
        ],
        out_specs=(pl.BlockSpec((tm, nq), lambda i: (i, 0)), pl.BlockSpec((tm, nk), lambda i: (i, 0)),
                   pl.BlockSpec((tm, nk), lambda i: (i, 0))),
        compiler_params=_params("parallel"),
        name="attn_qkv_rope",
    )(xs, mod, g, w, cos, sin)


def _proj_in_kernel(x_ref, mod_ref, g_ref, w_ref, o_ref, slab_scr, *, mi, gi):
    h = _rms(x_ref[...], g_ref[gi:gi + 1, :]) * (1.0 + mod_ref[0, mi + 1:mi + 2, :]) + mod_ref[0, mi:mi + 1, :]
    u = jnp.dot(h.astype(BF16), w_ref[...], preferred_element_type=F32)
    n_slab, tm, lanes = slab_scr.shape
    pair = GROUPS_PER_STEP * SSM_GROUP_CH
    per_slab = lanes // pair
    for k in range(n_slab):
        slab_scr[k] = u[:, k * lanes:(k + 1) * lanes]
    for t in range(SSM_CHUNK):
        for k in range(n_slab):
            rows = slab_scr[k, pl.ds(t, tm // SSM_CHUNK, stride=SSM_CHUNK), :]
            for jj in range(per_slab):
                o_ref[k * per_slab + jj, :, t * pair:(t + 1) * pair] = rows[:, jj * pair:(jj + 1) * pair].astype(o_ref.dtype)


def _proj_in(xs, mod, g, w, *, mi, gi, n_tiles, seg_of):
    d = xs.shape[1]
    tm = TOKEN_TILE
    pair = GROUPS_PER_STEP * SSM_GROUP_CH
    n_pairs = w.shape[1] // pair
    return pl.pallas_call(
        functools.partial(_proj_in_kernel, mi=mi, gi=gi),
        out_shape=jax.ShapeDtypeStruct((n_pairs, n_tiles * tm // SSM_CHUNK, SSM_CHUNK * pair), BF16),
        grid=(n_tiles,),
        in_specs=[
            pl.BlockSpec((tm, d), lambda i: (i, 0)),
            pl.BlockSpec((1, MOD_ROWS, d), lambda i: (seg_of(i), 0, 0)),
            pl.BlockSpec((8, d), lambda i: (0, 0)),
            pl.BlockSpec(w.shape, lambda i: (0, 0), pipeline_mode=pl.Buffered(1)),
        ],
        out_specs=pl.BlockSpec((n_pairs, tm // SSM_CHUNK, SSM_CHUNK * pair), lambda i: (0, i, 0)),
        scratch_shapes=[pltpu.VMEM((w.shape[1] // LANES, tm, LANES), F32)],
        compiler_params=_params("parallel"),
        name="ssm_in_proj",
    )(xs, mod, g, w)


def _proj_out_kernel(a_ref, x_ref, mod_ref, g_ref, w_ref, o_ref, *, mi, gi):
    out = jnp.dot(a_ref[...], w_ref[...], preferred_element_type=F32)
    o_ref[...] = x_ref[...] + mod_ref[0, mi:mi + 1, :] * _rms(out, g_ref[gi:gi + 1, :])


def _proj_out(a, xs, mod, g, w, *, mi, gi, n_tiles, seg_of):
    d = xs.shape[1]
    tm = TOKEN_TILE
    return pl.pallas_call(
        functools.partial(_proj_out_kernel, mi=mi, gi=gi),
        out_shape=jax.ShapeDtypeStruct((n_tiles * tm, d), F32),
        grid=(n_tiles,),
        in_specs=[
            pl.BlockSpec((tm, a.shape[1]), lambda i: (i, 0)),
            pl.BlockSpec((tm, d), lambda i: (i, 0)),
            pl.BlockSpec((1, MOD_ROWS, d), lambda i: (seg_of(i), 0, 0)),
            pl.BlockSpec((8, d), lambda i: (0, 0)),
            pl.BlockSpec(w.shape, lambda i: (0, 0), pipeline_mode=pl.Buffered(1)),
        ],
        out_specs=pl.BlockSpec((tm, d), lambda i: (i, 0)),
        compiler_params=_params("parallel"),
        name="attn_out_proj",
    )(a, xs, mod, g, w)


def _attn_kernel(sink_ref, q_ref, kp_ref, ko_ref, kn_ref, vp_ref, vo_ref, vn_ref, kc_ref, vc_ref, bias_ref, o_ref):
    blk = ATT_BLOCK
    bias = jnp.concatenate([bias_ref[...]] * GQA_GROUP, axis=0)
    row = lax.broadcasted_iota(jnp.int32, (GQA_GROUP * blk, 1), 0)
    for h in range(N_KV_HEADS):
        hs = slice(h * HEAD_DIM, (h + 1) * HEAD_DIM)
        qs = jnp.concatenate(
            [q_ref[:, (h * GQA_GROUP + gq) * HEAD_DIM:(h * GQA_GROUP + gq + 1) * HEAD_DIM] for gq in range(GQA_GROUP)],
            axis=0)
        kb = jnp.concatenate([kp_ref[:, hs], ko_ref[:, hs], kn_ref[:, hs], kc_ref[:, hs]], axis=0)
        vb = jnp.concatenate([vp_ref[:, hs], vo_ref[:, hs], vn_ref[:, hs], vc_ref[:, hs]], axis=0)
        s = lax.dot_general(qs, kb, (((1,), (1,)), ((), ())), preferred_element_type=F32) + bias
        sink = jnp.full((GQA_GROUP * blk, 1), sink_ref[h * GQA_GROUP], F32)
        for gq in range(1, GQA_GROUP):
            sink = jnp.where(row >= gq * blk, sink_ref[h * GQA_GROUP + gq], sink)
        m = jnp.maximum(jnp.max(s, axis=-1, keepdims=True), sink)
        p = jnp.exp(s - m)
        denom = jnp.sum(p, axis=-1, keepdims=True) + jnp.exp(sink - m)
        o = jnp.dot(p.astype(BF16), vb, preferred_element_type=F32) / denom
        for gq in range(GQA_GROUP):
            col = (h * GQA_GROUP + gq) * HEAD_DIM
            o_ref[:, col:col + HEAD_DIM] = o[gq * blk:(gq + 1) * blk, :].astype(o_ref.dtype)


def _attention(q, k, v, sink, *, batch, seq_len, c_len):
    blk = ATT_BLOCK
    n_blk, c_blk = seq_len // blk, c_len // blk
    nq, nk = q.shape[1], k.shape[1]
    lat_blocks = batch * n_blk
    ctx_block0 = batch * seq_len // c_len

    def q_idx(b, n):
        return jnp.where(n < n_blk, b * n_blk + n, lat_blocks + b * c_blk + (n - n_blk))

    def band_idx(off):
        def idx(b, n, s):
            nn = jnp.clip(n + off, 0, n_blk - 1)
            return (b * n_blk + nn, 0)
        return idx

    assert n_blk >= 2
    n_keys = 3 * blk + c_len
    qi = jnp.arange(blk)[:, None]
    kj = jnp.arange(n_keys)[None, :]
    in_band = (kj - qi >= 0) & (kj - qi <= 2 * blk)
    cases = [in_band & (kj >= blk), in_band, in_band & (kj < 2 * blk), jnp.zeros_like(in_band)]
    bias = jnp.stack([jnp.where((kj < 3 * blk) & ~case, NEG_INF, 0.0).astype(F32) for case in cases])

    def bias_idx(b, n, s):
        return (jnp.where(n == 0, 0, jnp.where(n < n_blk - 1, 1, jnp.where(n == n_blk - 1, 2, 3))), 0, 0)

    kv_spec = [pl.BlockSpec((blk, nk), band_idx(off)) for off in (-1, 0, 1)]
    ctx_spec = pl.BlockSpec((c_len, nk), lambda b, n, s: (ctx_block0 + b, 0))
    grid_spec = pltpu.PrefetchScalarGridSpec(
        num_scalar_prefetch=1,
        grid=(batch, n_blk + c_blk),
        in_specs=[pl.BlockSpec((blk, nq), lambda b, n, s: (q_idx(b, n), 0))] + kv_spec + kv_spec + [ctx_spec, ctx_spec]
        + [pl.BlockSpec((None, blk, n_keys), bias_idx)],
        out_specs=pl.BlockSpec((blk, nq), lambda b, n, s: (q_idx(b, n), 0)),
    )
    return pl.pallas_call(
        _attn_kernel,
        out_shape=jax.ShapeDtypeStruct(q.shape, BF16),
        grid_spec=grid_spec,
        compiler_params=_params("parallel", "arbitrary"),
        name="window_gqa",
    )(sink, q, k, k, k, v, v, v, k, v, bias)


def _s5_kernel(u_ref, t_ref, ws_ref, wc_ref, al_ref, d_ref, y_ref, s_scr, h_scr, *, batch, n_lat, n_ctx):
    half = GROUPS_PER_STEP * SSM_STATE
    u = u_ref[0]
    s_scr[...] = jnp.dot(u, ws_ref[0], preferred_element_type=F32)
    sub = 8
    ctx0 = batch * n_lat
    zero = jnp.zeros((sub, half), F32)
    sub_row = lax.broadcasted_iota(jnp.int32, (sub, half), 0)

    def rep(v):
        return jnp.broadcast_to(v, (sub, half))

    def tables(row, fwd):
        pw = _powers(al_ref[0, row:row + 1, :], al_ref[0, row + 1:row + 2, :], sub)
        levels = []
        for k in (1, 2, 4):
            ok = (sub_row >= k) if fwd else (sub_row <= sub - 1 - k)
            levels.append((k if fwd else sub - k, jnp.where(ok, rep(pw[k][0]), 0.0), jnp.where(ok, rep(pw[k][1]), 0.0)))
        c_re, c_im = zero, zero
        for r in range(sub):
            e = r if fwd else sub - 1 - r
            c_re = jnp.where(sub_row == r, rep(pw[e][0]), c_re)
            c_im = jnp.where(sub_row == r, rep(pw[e][1]), c_im)
        inner = (sub_row >= 1) if fwd else (sub_row <= sub - 2)
        return dict(levels=levels, carry_w=(c_re, c_im), block_w=(rep(pw[sub][0]), rep(pw[sub][1])),
                    shift=1 if fwd else sub - 1, inner=inner, last=sub - 1 if fwd else 0)

    tbl_f, tbl_b = tables(0, True), tables(2, False)

    def scan_block(h, tbl, base, col, keep):
        h_re, h_im = h
        base = pl.multiple_of(base, sub)
        x_re = s_scr[pl.ds(base, sub), col:col + half]
        x_im = s_scr[pl.ds(base, sub), col + half:col + 2 * half]
        for shift, c_re, c_im in tbl["levels"]:
            r_re, r_im = pltpu.roll(x_re, shift, 0), pltpu.roll(x_im, shift, 0)
            x_re, x_im = x_re + c_re * r_re - c_im * r_im, x_im + c_re * r_im + c_im * r_re
        if keep:
            w_re, w_im = tbl["carry_w"]
            p_re = jnp.where(tbl["inner"], pltpu.roll(x_re, tbl["shift"], 0), 0.0)
            p_im = jnp.where(tbl["inner"], pltpu.roll(x_im, tbl["shift"], 0), 0.0)
            h_scr[pl.ds(base, sub), col:col + half] = w_re * h_re - w_im * h_im + p_re
            h_scr[pl.ds(base, sub), col + half:col + 2 * half] = w_re * h_im + w_im * h_re + p_im
        b_re, b_im = tbl["block_w"]
        last = tbl["last"]
        t_re, t_im = rep(x_re[last:last + 1, :]), rep(x_im[last:last + 1, :])
        return b_re * h_re - b_im * h_im + t_re, b_re * h_im + b_im * h_re + t_im

    def make_step(first_row, n_rows, keep):
        def step(m, carry):
            out = []
            for b in range(batch):
                hf, hb = carry[2 * b], carry[2 * b + 1]
                lo = first_row + b * n_rows
                hf = scan_block(hf, tbl_f, lo + m * sub, 0, keep)
                hb = scan_block(hb, tbl_b, lo + n_rows - sub - m * sub, 2 * half, keep)
                out += [hf, hb]
            return tuple(out)
        return step

    carry = lax.fori_loop(0, n_ctx // sub, make_step(ctx0, n_ctx, False), ((zero, zero),) * (2 * batch))
    lax.fori_loop(0, n_lat // sub, make_step(0, n_lat, True), carry)

    rows = batch * n_lat
    u_lat = u[:rows]
    y_ref[0] = (jnp.dot(u_lat, t_ref[0], preferred_element_type=F32)
                + jnp.dot(h_scr[...].astype(BF16), wc_ref[0], preferred_element_type=F32)
                + d_ref[0] * u_lat.astype(F32))


def _s5_core(ug, toep, ws, wc, al, d_lanes, *, batch, n_lat, n_ctx):
    n_steps, chunks, width = ug.shape
    assert n_lat % 8 == 0 and n_ctx % 8 == 0
    rows = batch * n_lat
    return pl.pallas_call(
        functools.partial(_s5_kernel, batch=batch, n_lat=n_lat, n_ctx=n_ctx),
        out_shape=jax.ShapeDtypeStruct((n_steps, rows, width), F32),
        grid=(n_steps,),
        in_specs=[
            pl.BlockSpec((1, chunks, width), lambda j: (j, 0, 0)),
            pl.BlockSpec((1,) + toep.shape[1:], lambda j: (j, 0, 0)),
            pl.BlockSpec((1,) + ws.shape[1:], lambda j: (j, 0, 0)),
            pl.BlockSpec((1,) + wc.shape[1:], lambda j: (j, 0, 0)),
            pl.BlockSpec((1,) + al.shape[1:], lambda j: (j, 0, 0)),
            pl.BlockSpec((1, 1, width), lambda j: (j, 0, 0)),
        ],
        out_specs=pl.BlockSpec((1, rows, width), lambda j: (j, 0, 0)),
        scratch_shapes=[pltpu.VMEM((chunks, 4 * GROUPS_PER_STEP * SSM_STATE), F32),
                        pltpu.VMEM((rows, 4 * GROUPS_PER_STEP * SSM_STATE), F32)],
        compiler_params=_params("parallel"),
        name="s5_chunked",
    )(ug, toep, ws, wc, al, d_lanes)


def _cmul(a_re, a_im, b_re, b_im):
    return a_re * b_re - a_im * b_im, a_re * b_im + a_im * b_re


def _zoh(a_re, a_im, log_dt):
    dt = jnp.exp(log_dt)
    mag = jnp.exp(a_re * dt)
    ab_re, ab_im = mag * jnp.cos(a_im * dt), mag * jnp.sin(a_im * dt)
    den = a_re * a_re + a_im * a_im
    x_re, x_im = ab_re - 1.0, ab_im
    return ab_re, ab_im, (x_re * a_re + x_im * a_im) / den, (x_im * a_re - x_re * a_im) / den


def _powers(ab_re, ab_im, n):
    out = [(jnp.ones_like(ab_re), jnp.zeros_like(ab_im))]
    for _ in range(n):
        out.append(_cmul(out[-1][0], out[-1][1], ab_re, ab_im))
    return out


def _s5_op_kernel(row_ref, col_ref, bt_ref, ct_ref, t_ref, ws_ref, wc_ref, al_ref):
    L = SSM_CHUNK
    pair = ct_ref.shape[-1]
    hi = lax.Precision.HIGHEST
    inject, taps, read = [], [], []
    for d in range(2):
        ab_re, ab_im, f_re, f_im = _zoh(*(row_ref[0, 3 * d + r:3 * d + r + 1, :] for r in range(3)))
        pw_row = _powers(ab_re, ab_im, L)
        bb_re, bb_im = _cmul(f_re, f_im, bt_ref[0, d, 0], bt_ref[0, d, 1])
        cb_re, cb_im, _, _ = _zoh(*(col_ref[0, :, 3 * d + r:3 * d + r + 1] for r in range(3)))
        pw_col = _powers(cb_re, cb_im, L)
        ca = [_cmul(ct_ref[0, d, 0], ct_ref[0, d, 1], pr, pi) for pr, pi in pw_col]
        order = range(L - 1, -1, -1) if d == 0 else range(L)
        inject.append([_cmul(pw_row[e][0], pw_row[e][1], bb_re, bb_im) for e in order])
        korder = range(L) if d == 0 else range(L - 1, -1, -1)
        cat_re = jnp.concatenate([ca[k][0] for k in korder], axis=1)
        cat_im = jnp.concatenate([ca[k][1] for k in korder], axis=1)
        taps.append(jnp.dot(bb_re, cat_re, precision=hi, preferred_element_type=F32)
                    - jnp.dot(bb_im, cat_im, precision=hi, preferred_element_type=F32))
        eorder = range(1, L + 1) if d == 0 else range(L, 0, -1)
        read.append((jnp.concatenate([ca[e][0] for e in eorder], axis=1),
                     jnp.concatenate([-ca[e][1] for e in eorder], axis=1)))
        al_ref[0, 2 * d:2 * d + 1, :] = pw_row[L][0]
        al_ref[0, 2 * d + 1:2 * d + 2, :] = pw_row[L][1]
    al_ref[0, 4:, :] = jnp.zeros((4, al_ref.shape[2]), F32)

    for s in range(L):
        ws_ref[0, s * pair:(s + 1) * pair, :] = jnp.concatenate(
            [inject[0][s][0], inject[0][s][1], inject[1][s][0], inject[1][s][1]], axis=1).astype(ws_ref.dtype)
    rows = read[0][0].shape[0]
    for q, blk in enumerate((read[0][0], read[0][1], read[1][0], read[1][1])):
        wc_ref[0, q * rows:(q + 1) * rows, :] = blk.astype(wc_ref.dtype)
    k_f, k_b = taps
    keep = (L - 1) * pair
    by_lag = jnp.concatenate([k_b[:, :keep], k_b[:, keep:] + k_f[:, :pair], k_f[:, pair:]], axis=1)
    for s in range(L):
        t_ref[0, s * pair:(s + 1) * pair, :] = by_lag[:, (L - 1 - s) * pair:(2 * L - 1 - s) * pair].astype(t_ref.dtype)


def _s5_operators(a_re, a_im, log_dt, b_re, b_im, c_re, c_im):
    n_g, n_p = a_re.shape[1], a_re.shape[2]
    gc = b_re.shape[-1]
    n_j, per = n_g // GROUPS_PER_STEP, GROUPS_PER_STEP
    states, pair, width = per * n_p, per * gc, SSM_CHUNK * per * gc
    lam = jnp.stack([a_re, a_im, jnp.broadcast_to(log_dt[..., None], a_re.shape)], axis=1)
    lam = jnp.transpose(lam.reshape(6, n_j, states), (1, 0, 2))
    lam = jnp.pad(lam, ((0, 0), (0, 2), (0, 0)))
    eye = jnp.eye(per, dtype=F32)
    bt = jnp.stack([b_re, b_im], axis=1).reshape(2, 2, n_j, per, n_p, gc)
    bt = jnp.einsum('drjgpc,gh->jdrgchp', bt, eye).reshape(n_j, 2, 2, pair, states)
    ct = jnp.stack([c_re, c_im], axis=1).reshape(2, 2, n_j, per, gc, n_p)
    ct = jnp.einsum('drjgcp,gh->jdrgphc', ct, eye).reshape(n_j, 2, 2, states, pair)
    mat = jax.ShapeDtypeStruct((n_j, width, width), BF16)
    return pl.pallas_call(
        _s5_op_kernel,
        out_shape=(mat, mat, mat, jax.ShapeDtypeStruct((n_j, 8, states), F32)),
        grid=(n_j,),
        in_specs=[
            pl.BlockSpec((1, 8, states), lambda j: (j, 0, 0)),
            pl.BlockSpec((1, states, 8), lambda j: (j, 0, 0)),
            pl.BlockSpec((1, 2, 2, pair, states), lambda j: (j, 0, 0, 0, 0)),
            pl.BlockSpec((1, 2, 2, states, pair), lambda j: (j, 0, 0, 0, 0)),
        ],
        out_specs=(pl.BlockSpec((1, width, width), lambda j: (j, 0, 0)),) * 3
        + (pl.BlockSpec((1, 8, states), lambda j: (j, 0, 0)),),
        compiler_params=_params("parallel"),
        name="s5_operators",
    )(lam, jnp.transpose(lam, (0, 2, 1)), bt, ct)


def _ssm_out_kernel(y_ref, x_ref, mod_ref, g_ref, w_ref, o_ref, slab_scr, act_even, act_odd, out_scr, *, mi, gi):
    i = pl.program_id(0)

    @pl.when(i == 0)
    def _():
        act_odd[...] = jnp.zeros_like(act_odd)

    @pl.when(i % 2 == 0)
    def _():
        _ssm_out_step(y_ref, x_ref, mod_ref, g_ref, w_ref, o_ref, slab_scr, act_even, act_odd, out_scr, mi=mi, gi=gi)

    @pl.when(i % 2 == 1)
    def _():
        _ssm_out_step(y_ref, x_ref, mod_ref, g_ref, w_ref, o_ref, slab_scr, act_odd, act_even, out_scr, mi=mi, gi=gi)


def _ssm_out_step(y_ref, x_ref, mod_ref, g_ref, w_ref, o_ref, slab_scr, act_new, act_old, out_scr, *, mi, gi):
    n_slab, tm, lanes = slab_scr.shape
    pair = GROUPS_PER_STEP * SSM_GROUP_CH
    per_slab = lanes // pair
    for t in range(SSM_CHUNK):
        for k in range(n_slab):
            slab_scr[k, pl.ds(t, tm // SSM_CHUNK, stride=SSM_CHUNK), :] = jnp.concatenate(
                [y_ref[k * per_slab + jj, :, t * pair:(t + 1) * pair] for jj in range(per_slab)], axis=-1)
    for k in range(n_slab):
        act_new[:, k * lanes:(k + 1) * lanes] = _gelu_tanh(slab_scr[k]).astype(BF16)

    n_c, _, tn = out_scr.shape
    d = n_c * tn
    a = act_old[...]
    ssq = jnp.zeros((tm, 1), F32)
    for cc in range(n_c):
        val = jnp.dot(a, w_ref[:, cc * tn:(cc + 1) * tn], preferred_element_type=F32)
        gate = jnp.dot(a, w_ref[:, d + cc * tn:d + (cc + 1) * tn], preferred_element_type=F32)
        blk = val * _sigmoid(gate)
        out_scr[cc] = blk
        ssq = ssq + jnp.sum(blk * blk, axis=-1, keepdims=True)
    inv = lax.rsqrt(ssq / d + EPS)
    for cc in range(n_c):
        sl = slice(cc * tn, (cc + 1) * tn)
        o_ref[:, sl] = x_ref[:, sl] + mod_ref[0, mi:mi + 1, sl] * (out_scr[cc] * inv * g_ref[gi:gi + 1, sl])


def _ssm_out(yg, xs, mod, g, w_glu, *, mi, gi, n_rows, seg_of_row):
    d = xs.shape[1]
    tm, tn = GLU_TOKEN_TILE, GLU_TILE
    n_c = d // tn
    n_pairs, _, width = yg.shape
    n_tiles = n_rows // tm

    def prev(i):
        return jnp.maximum(i - 1, 0)

    return pl.pallas_call(
        functools.partial(_ssm_out_kernel, mi=mi, gi=gi),
        out_shape=jax.ShapeDtypeStruct((n_rows, d), F32),
        grid=(n_tiles + 1,),
        in_specs=[
            pl.BlockSpec((n_pairs, tm // SSM_CHUNK, width), lambda i: (0, jnp.minimum(i, n_tiles - 1), 0)),
            pl.BlockSpec((tm, d), lambda i: (prev(i), 0)),
            pl.BlockSpec((1, MOD_ROWS, d), lambda i: (seg_of_row(prev(i) * tm), 0, 0)),
            pl.BlockSpec((8, d), lambda i: (0, 0)),
            pl.BlockSpec(w_glu.shape, lambda i: (0, 0), pipeline_mode=pl.Buffered(1)),
        ],
        out_specs=pl.BlockSpec((tm, d), lambda i: (prev(i), 0)),
        scratch_shapes=[pltpu.VMEM((d // LANES, tm, LANES), F32), pltpu.VMEM((tm, d), BF16),
                        pltpu.VMEM((tm, d), BF16), pltpu.VMEM((n_c, tm, tn), F32)],
        compiler_params=_params("arbitrary"),
        name="ssm_glu_out",
    )(yg, xs, mod, g, w_glu)


def _rope_tables(seq_len, extra_rows):
    rows = seq_len // GRID_W
    row = jnp.repeat(jnp.arange(rows, dtype=F32), GRID_W)
    col = jnp.tile(jnp.arange(GRID_W, dtype=F32), rows)
    axis_dim = HEAD_DIM // 2
    inv_freq = ROPE_BASE ** (-jnp.arange(0, axis_dim, 2, dtype=F32) / axis_dim)
    ang_r, ang_c = row[:, None] * inv_freq, col[:, None] * inv_freq
    cos = jnp.concatenate([jnp.cos(ang_r)] * 2 + [jnp.cos(ang_c)] * 2, axis=-1)
    sin = jnp.concatenate([-jnp.sin(ang_r), jnp.sin(ang_r), -jnp.sin(ang_c), jnp.sin(ang_c)], axis=-1)
    cos = jnp.concatenate([cos, jnp.ones((extra_rows, HEAD_DIM), F32)], axis=0)
    sin = jnp.concatenate([sin, jnp.zeros((extra_rows, HEAD_DIM), F32)], axis=0)
    return cos, sin


def kernel(x, c, ctx, c_ctx, ada_w, ada_b, norm_g, ffn_w_in, ffn_w_out, attn_w_in, attn_w_out, attn_sink,
           ssm_w_in, ssm_a_re, ssm_a_im, ssm_log_dt, ssm_b_re, ssm_b_im, ssm_c_re, ssm_c_im, ssm_d, ssm_w_glu):
    batch, seq_len, d = x.shape
    c_len = ctx.shape[1]
    depth = ada_w.shape[0]
    tm = TOKEN_TILE
    n_lat_rows, n_ctx_rows = batch * seq_len, batch * c_len
    assert seq_len % tm == 0 and n_ctx_rows % tm == 0 and n_lat_rows % c_len == 0
    assert seq_len % ATT_BLOCK == 0 and c_len % ATT_BLOCK == 0 and seq_len % GRID_W == 0
    tiles_per_batch = seq_len // tm
    lat_tiles = n_lat_rows // tm
    all_tiles = lat_tiles + n_ctx_rows // tm

    def seg_of(i):
        return jnp.minimum(i // tiles_per_batch, batch)

    def pos_of(i):
        return jnp.where(i < lat_tiles, i % tiles_per_batch, tiles_per_batch + (i - lat_tiles))

    def seg_of_row(r):
        return jnp.minimum(r // seq_len, batch)

    xs, tail = x.reshape(n_lat_rows, d), ctx.reshape(n_ctx_rows, d)
    w_ff_in, w_ff_out = ffn_w_in.astype(BF16), ffn_w_out.astype(BF16)

    n_cond = batch + 1
    cvec = jnp.concatenate([c, c_ctx[None], jnp.zeros((8 - n_cond, d), F32)], axis=0)
    mods = _modulation(cvec.T, ada_w, ada_b, n_cond)
    mods = mods[:, :n_cond].reshape(depth, n_cond, N_MOD, d)
    mods = jnp.pad(mods, ((0, 0), (0, 0), (0, MOD_ROWS - N_MOD), (0, 0)))
    gains = jnp.pad(norm_g, ((0, 0), (0, 8 - norm_g.shape[1]), (0, 0)))

    cos, sin = _rope_tables(seq_len, n_ctx_rows)

    for i in range(depth):
        need_ctx = i < depth - 1
        mod, g = mods[i], gains[i]
        j = i // 2
        xs = _ffn(xs, mod, g, w_ff_in, w_ff_out, (i, 0), mi=0, gi=0, n_tiles=all_tiles, seg_of=seg_of, tail=tail)
        tail = None
        post_tiles = all_tiles if need_ctx else lat_tiles
        if i % 2 == 0:
            q, k, v = _qkv(xs, mod, g, attn_w_in[j].astype(BF16), cos, sin,
                           mi=3, gi=2, n_tiles=all_tiles, seg_of=seg_of, pos_of=pos_of)
            o = _attention(q, k, v, attn_sink[j], batch=batch, seq_len=seq_len, c_len=c_len)
            xs = _proj_out(o, xs, mod, g, attn_w_out[j].astype(BF16),
                           mi=5, gi=3, n_tiles=post_tiles, seg_of=seg_of)
        else:
            if need_ctx:
                raise NotImplementedError("context output of the S5 mixer is only needed when another layer follows")
            ug = _proj_in(xs, mod, g, ssm_w_in[j].astype(BF16), mi=3, gi=2, n_tiles=all_tiles, seg_of=seg_of)
            L = SSM_CHUNK
            pair = GROUPS_PER_STEP * SSM_GROUP_CH
            toep, ws, wc, al = _s5_operators(ssm_a_re[j], ssm_a_im[j], ssm_log_dt[j], ssm_b_re[j], ssm_b_im[j],
                                             ssm_c_re[j], ssm_c_im[j])
            d_lanes = jnp.tile(ssm_d[j].reshape(d // pair, 1, pair), (1, 1, L))
            yg = _s5_core(ug, toep, ws, wc, al, d_lanes, batch=batch, n_lat=seq_len // L, n_ctx=c_len // L)
            xs = _ssm_out(yg, xs, mod, g, ssm_w_glu[j].astype(BF16),
                          mi=5, gi=3, n_rows=post_tiles * tm, seg_of_row=seg_of_row)
        xs = _ffn(xs, mod, g, w_ff_in, w_ff_out, (i, 1), mi=6, gi=4, n_tiles=post_tiles, seg_of=seg_of)
    return xs[:n_lat_rows].reshape(batch, seq_len, d)
```

```python
import functools
import math

import jax
import jax.numpy as jnp
from jax import lax
from jax.experimental import pallas as pl
from jax.experimental.pallas import tpu as pltpu

EPS = 1e-6
NEG_INF = -1e30
N_MOD = 9
GRID_W = 64
N_HEADS = 16
N_KV_HEADS = 4
HEAD_DIM = 128
GQA_GROUP = N_HEADS // N_KV_HEADS
ATT_BLOCK = 128
ROPE_BASE = 10000.0
SSM_GROUP_CH = 16
SSM_STATE = 64
SSM_CHUNK = 16
GROUPS_PER_STEP = 2

TOKEN_TILE = 512
FF_TILE = 512
GLU_TILE = 512
GLU_TOKEN_TILE = 256
MOD_TILE = 1024
MOD_ROWS = 16
VMEM_LIMIT = 56 * 1024 * 1024
LANES = 128

BF16 = jnp.bfloat16
F32 = jnp.float32


def _params(*sem):
    return pltpu.CompilerParams(dimension_semantics=sem, vmem_limit_bytes=VMEM_LIMIT)


def _rms(t, g):
    return t * lax.rsqrt(jnp.mean(t * t, axis=-1, keepdims=True) + EPS) * g


def _sigmoid(t):
    return 1.0 / (1.0 + jnp.exp(-t))


def _gelu_tanh(t):
    return 0.5 * t * (1.0 + jnp.tanh(math.sqrt(2.0 / math.pi) * (t + 0.044715 * (t * t * t))))


def _mod_kernel(ct_ref, w_ref, b_ref, o_ref, *, n_rows):
    w = w_ref[0]
    for r in range(n_rows):
        col = ct_ref[:, r:r + 1]
        col = col * _sigmoid(col)
        o_ref[0, r:r + 1, :] = jnp.sum(w * col, axis=0, keepdims=True) + b_ref[0]
    o_ref[0, n_rows:, :] = jnp.zeros((o_ref.shape[1] - n_rows, o_ref.shape[2]), F32)


def _modulation(cvec_t, ada_w, ada_b, n_rows):
    depth, d, nd = ada_w.shape
    tn = MOD_TILE
    return pl.pallas_call(
        functools.partial(_mod_kernel, n_rows=n_rows),
        out_shape=jax.ShapeDtypeStruct((depth, 8, nd), F32),
        grid=(depth, nd // tn),
        in_specs=[
            pl.BlockSpec((d, 8), lambda l, j: (0, 0)),
            pl.BlockSpec((1, d, tn), lambda l, j: (l, 0, j)),
            pl.BlockSpec((1, 1, tn), lambda l, j: (l, 0, j)),
        ],
        out_specs=pl.BlockSpec((1, 8, tn), lambda l, j: (l, 0, j)),
        compiler_params=_params("arbitrary", "arbitrary"),
        name="adaln_mod",
    )(cvec_t, ada_w, ada_b.reshape(depth, 1, nd))


def _ffn_kernel(x_ref, tail_ref, mod_ref, g_ref, wg_ref, wu_ref, wo_ref, o_ref, h_scr, acc_scr,
                *, mi, gi, head_tiles, n_ff):
    def read_x():
        if head_tiles is None:
            return x_ref[...]
        return jnp.where(pl.program_id(0) < head_tiles, x_ref[...], tail_ref[...])

    def step(first, last):
        if first:
            xv = read_x()
            inv = lax.rsqrt(jnp.mean(xv * xv, axis=-1, keepdims=True) + EPS)
            h = (xv * inv) * (g_ref[gi:gi + 1, :] * (1.0 + mod_ref[0, mi + 1:mi + 2, :])) + mod_ref[0, mi:mi + 1, :]
            h = h.astype(BF16)
            if not last:
                h_scr[...] = h
        else:
            h = h_scr[...]
        gate = jnp.dot(h, wg_ref[...], preferred_element_type=F32)
        up = jnp.dot(h, wu_ref[...], preferred_element_type=F32)
        act = (gate * _sigmoid(gate) * up).astype(BF16)
        out = jnp.dot(act, wo_ref[...], preferred_element_type=F32)
        if not first:
            out = acc_scr[...] + out
        if last:
            inv = lax.rsqrt(jnp.mean(out * out, axis=-1, keepdims=True) + EPS)
            o_ref[...] = read_x() + (out * inv) * (0.5 * mod_ref[0, mi + 2:mi + 3, :] * g_ref[gi + 1:gi + 2, :])
        else:
            acc_scr[...] = out

    f = pl.program_id(1)
    if n_ff == 1:
        step(True, True)
    else:
        pl.when(f == 0)(functools.partial(step, True, False))
        if n_ff > 2:
            pl.when((f > 0) & (f < n_ff - 1))(functools.partial(step, False, False))
        pl.when(f == n_ff - 1)(functools.partial(step, False, True))


def _ffn(xs, mod, g, w_in, w_out, which, *, mi, gi, n_tiles, seg_of, tail=None):
    d = xs.shape[1]
    ff = w_out.shape[2]
    tm, tf = TOKEN_TILE, FF_TILE
    nf = ff // tf
    layer, half = which
    if tail is None:
        head_tiles = None
        tokens = [xs, xs]
        token_specs = [pl.BlockSpec((tm, d), lambda i, f: (i, 0)), pl.BlockSpec((8, d), lambda i, f: (0, 0))]
    else:
        head_tiles = xs.shape[0] // tm
        tokens = [xs, tail]
        token_specs = [pl.BlockSpec((tm, d), lambda i, f: (jnp.minimum(i, head_tiles - 1), 0)),
                       pl.BlockSpec((tm, d), lambda i, f: (jnp.maximum(i - head_tiles, 0), 0),
                                    pipeline_mode=pl.Buffered(1))]
    return pl.pallas_call(
        functools.partial(_ffn_kernel, mi=mi, gi=gi, head_tiles=head_tiles, n_ff=nf),
        out_shape=jax.ShapeDtypeStruct((n_tiles * tm, d), F32),
        grid=(n_tiles, nf),
        in_specs=token_specs + [
            pl.BlockSpec((1, MOD_ROWS, d), lambda i, f: (seg_of(i), 0, 0)),
            pl.BlockSpec((8, d), lambda i, f: (0, 0)),
            pl.BlockSpec((None, None, d, tf), lambda i, f: (layer, half, 0, f)),
            pl.BlockSpec((None, None, d, tf), lambda i, f: (layer, half, 0, nf + f)),
            pl.BlockSpec((None, None, tf, d), lambda i, f: (layer, half, f, 0)),
        ],
        out_specs=pl.BlockSpec((tm, d), lambda i, f: (i, 0)),
        scratch_shapes=[pltpu.VMEM((tm, d), BF16), pltpu.VMEM((tm, d), F32)],
        compiler_params=_params("parallel", "arbitrary"),
        name="ffn_halfstep",
    )(*tokens, mod, g, w_in, w_in, w_out)


def _rope(t, cos, sin, even_quarter):
    partner = jnp.where(even_quarter, pltpu.roll(t, HEAD_DIM - 32, 1), pltpu.roll(t, 32, 1))
    return t * cos + partner * sin


def _qkv_kernel(x_ref, mod_ref, g_ref, w_ref, cos_ref, sin_ref, q_ref, k_ref, v_ref, *, mi, gi):
    h = _rms(x_ref[...], g_ref[gi:gi + 1, :]) * (1.0 + mod_ref[0, mi + 1:mi + 2, :]) + mod_ref[0, mi:mi + 1, :]
    qkv = jnp.dot(h.astype(BF16), w_ref[...], preferred_element_type=F32)
    cos, sin = cos_ref[...], sin_ref[...]
    lane = lax.broadcasted_iota(jnp.int32, cos.shape, 1)
    even_quarter = (lane & 32) == 0
    scale = HEAD_DIM ** -0.5
    nq, nk = N_HEADS * HEAD_DIM, N_KV_HEADS * HEAD_DIM
    for hh in range(N_HEADS):
        sl = slice(hh * HEAD_DIM, (hh + 1) * HEAD_DIM)
        q_ref[:, sl] = (_rope(qkv[:, sl], cos, sin, even_quarter) * scale).astype(BF16)
    for hh in range(N_KV_HEADS):
        sl = slice(hh * HEAD_DIM, (hh + 1) * HEAD_DIM)
        src = slice(nq + hh * HEAD_DIM, nq + (hh + 1) * HEAD_DIM)
        k_ref[:, sl] = _rope(qkv[:, src], cos, sin, even_quarter).astype(BF16)
    v_ref[...] = qkv[:, nq + nk:].astype(BF16)


def _qkv(xs, mod, g, w, cos, sin, *, mi, gi, n_tiles, seg_of, pos_of):
    d = xs.shape[1]
    tm = TOKEN_TILE
    nq, nk = N_HEADS * HEAD_DIM, N_KV_HEADS * HEAD_DIM
    rows = n_tiles * tm
    return pl.pallas_call(
        functools.partial(_qkv_kernel, mi=mi, gi=gi),
        out_shape=(jax.ShapeDtypeStruct((rows, nq), BF16), jax.ShapeDtypeStruct((rows, nk), BF16),
                   jax.ShapeDtypeStruct((rows, nk), BF16)),
        grid=(n_tiles,),
        in_specs=[
            pl.BlockSpec((tm, d), lambda i: (i, 0)),
            pl.BlockSpec((1, MOD_ROWS, d), lambda i: (seg_of(i), 0, 0)),
            pl.BlockSpec((8, d), lambda i: (0, 0)),
            pl.BlockSpec((d, nq + 2 * nk), lambda i: (0, 0), pipeline_mode=pl.Buffered(1)),
            pl.BlockSpec((tm, HEAD_DIM), lambda i: (pos_of(i), 0)),
            pl.BlockSpec((tm, HEAD_DIM), lambda i: (pos_of(i), 0)),
        ],
        out_specs=(pl.BlockSpec((tm, nq), lambda i: (i, 0)), pl.BlockSpec((tm, nk), lambda i: (i, 0)),
                   pl.BlockSpec((tm, nk), lambda i: (i, 0))),
        compiler_params=_params("parallel"),
        name="attn_qkv_rope",
    )(xs, mod, g, w, cos, sin)


def _proj_in_kernel(x_ref, mod_ref, g_ref, w_ref, o_ref, slab_scr, *, mi, gi):
    h = _rms(x_ref[...], g_ref[gi:gi + 1, :]) * (1.0 + mod_ref[0, mi + 1:mi + 2, :]) + mod_ref[0, mi:mi + 1, :]
    u = jnp.dot(h.astype(BF16), w_ref[...], preferred_element_type=F32)
    n_slab, tm, lanes = slab_scr.shape
    pair = GROUPS_PER_STEP * SSM_GROUP_CH
    per_slab = lanes // pair
    for k in range(n_slab):
        slab_scr[k] = u[:, k * lanes:(k + 1) * lanes]
    for t in range(SSM_CHUNK):
        for k in range(n_slab):
            rows = slab_scr[k, pl.ds(t, tm // SSM_CHUNK, stride=SSM_CHUNK), :]
            for jj in range(per_slab):
                o_ref[k * per_slab + jj, :, t * pair:(t + 1) * pair] = rows[:, jj * pair:(jj + 1) * pair].astype(o_ref.dtype)


def _proj_in(xs, mod, g, w, *, mi, gi, n_tiles, seg_of):
    d = xs.shape[1]
    tm = TOKEN_TILE
    pair = GROUPS_PER_STEP * SSM_GROUP_CH
    n_pairs = w.shape[1] // pair
    return pl.pallas_call(
        functools.partial(_proj_in_kernel, mi=mi, gi=gi),
        out_shape=jax.ShapeDtypeStruct((n_pairs, n_tiles * tm // SSM_CHUNK, SSM_CHUNK * pair), BF16),
        grid=(n_tiles,),
        in_specs=[
            pl.BlockSpec((tm, d), lambda i: (i, 0)),
            pl.BlockSpec((1, MOD_ROWS, d), lambda i: (seg_of(i), 0, 0)),
            pl.BlockSpec((8, d), lambda i: (0, 0)),
            pl.BlockSpec(w.shape, lambda i: (0, 0), pipeline_mode=pl.Buffered(1)),
        ],
        out_specs=pl.BlockSpec((n_pairs, tm // SSM_CHUNK, SSM_CHUNK * pair), lambda i: (0, i, 0)),
        scratch_shapes=[pltpu.VMEM((w.shape[1] // LANES, tm, LANES), F32)],
        compiler_params=_params("parallel"),
        name="ssm_in_proj",
    )(xs, mod, g, w)


def _proj_out_kernel(a_ref, x_ref, mod_ref, g_ref, w_ref, o_ref, *, mi, gi):
    out = jnp.dot(a_ref[...], w_ref[...], preferred_element_type=F32)
    o_ref[...] = x_ref[...] + mod_ref[0, mi:mi + 1, :] * _rms(out, g_ref[gi:gi + 1, :])


def _proj_out(a, xs, mod, g, w, *, mi, gi, n_tiles, seg_of):
    d = xs.shape[1]
    tm = TOKEN_TILE
    return pl.pallas_call(
        functools.partial(_proj_out_kernel, mi=mi, gi=gi),
        out_shape=jax.ShapeDtypeStruct((n_tiles * tm, d), F32),
        grid=(n_tiles,),
        in_specs=[
            pl.BlockSpec((tm, a.shape[1]), lambda i: (i, 0)),
            pl.BlockSpec((tm, d), lambda i: (i, 0)),
            pl.BlockSpec((1, MOD_ROWS, d), lambda i: (seg_of(i), 0, 0)),
            pl.BlockSpec((8, d), lambda i: (0, 0)),
            pl.BlockSpec(w.shape, lambda i: (0, 0), pipeline_mode=pl.Buffered(1)),
        ],
        out_specs=pl.BlockSpec((tm, d), lambda i: (i, 0)),
        compiler_params=_params("parallel"),
        name="attn_out_proj",
    )(a, xs, mod, g, w)


def _attn_kernel(sink_ref, q_ref, kp_ref, ko_ref, kn_ref, vp_ref, vo_ref, vn_ref, kc_ref, vc_ref, bias_ref, o_ref):
    blk = ATT_BLOCK
    bias = jnp.concatenate([bias_ref[...]] * GQA_GROUP, axis=0)
    row = lax.broadcasted_iota(jnp.int32, (GQA_GROUP * blk, 1), 0)
    for h in range(N_KV_HEADS):
        hs = slice(h * HEAD_DIM, (h + 1) * HEAD_DIM)
        qs = jnp.concatenate(
            [q_ref[:, (h * GQA_GROUP + gq) * HEAD_DIM:(h * GQA_GROUP + gq + 1) * HEAD_DIM] for gq in range(GQA_GROUP)],
            axis=0)
        kb = jnp.concatenate([kp_ref[:, hs], ko_ref[:, hs], kn_ref[:, hs], kc_ref[:, hs]], axis=0)
        vb = jnp.concatenate([vp_ref[:, hs], vo_ref[:, hs], vn_ref[:, hs], vc_ref[:, hs]], axis=0)
        s = lax.dot_general(qs, kb, (((1,), (1,)), ((), ())), preferred_element_type=F32) + bias
        sink = jnp.full((GQA_GROUP * blk, 1), sink_ref[h * GQA_GROUP], F32)
        for gq in range(1, GQA_GROUP):
            sink = jnp.where(row >= gq * blk, sink_ref[h * GQA_GROUP + gq], sink)
        m = jnp.maximum(jnp.max(s, axis=-1, keepdims=True), sink)
        p = jnp.exp(s - m)
        denom = jnp.sum(p, axis=-1, keepdims=True) + jnp.exp(sink - m)
        o = jnp.dot(p.astype(BF16), vb, preferred_element_type=F32) / denom
        for gq in range(GQA_GROUP):
            col = (h * GQA_GROUP + gq) * HEAD_DIM
            o_ref[:, col:col + HEAD_DIM] = o[gq * blk:(gq + 1) * blk, :].astype(o_ref.dtype)


def _attention(q, k, v, sink, *, batch, seq_len, c_len):
    blk = ATT_BLOCK
    n_blk, c_blk = seq_len // blk, c_len // blk
    nq, nk = q.shape[1], k.shape[1]
    lat_blocks = batch * n_blk
    ctx_block0 = batch * seq_len // c_len

    def q_idx(b, n):
        return jnp.where(n < n_blk, b * n_blk + n, lat_blocks + b * c_blk + (n - n_blk))

    def band_idx(off):
        def idx(b, n, s):
            nn = jnp.clip(n + off, 0, n_blk - 1)
            return (b * n_blk + nn, 0)
        return idx

    assert n_blk >= 2
    n_keys = 3 * blk + c_len
    qi = jnp.arange(blk)[:, None]
    kj = jnp.arange(n_keys)[None, :]
    in_band = (kj - qi >= 0) & (kj - qi <= 2 * blk)
    cases = [in_band & (kj >= blk), in_band, in_band & (kj < 2 * blk), jnp.zeros_like(in_band)]
    bias = jnp.stack([jnp.where((kj < 3 * blk) & ~case, NEG_INF, 0.0).astype(F32) for case in cases])

    def bias_idx(b, n, s):
        return (jnp.where(n == 0, 0, jnp.where(n < n_blk - 1, 1, jnp.where(n == n_blk - 1, 2, 3))), 0, 0)

    kv_spec = [pl.BlockSpec((blk, nk), band_idx(off)) for off in (-1, 0, 1)]
    ctx_spec = pl.BlockSpec((c_len, nk), lambda b, n, s: (ctx_block0 + b, 0))
    grid_spec = pltpu.PrefetchScalarGridSpec(
        num_scalar_prefetch=1,
        grid=(batch, n_blk + c_blk),
        in_specs=[pl.BlockSpec((blk, nq), lambda b, n, s: (q_idx(b, n), 0))] + kv_spec + kv_spec + [ctx_spec, ctx_spec]
        + [pl.BlockSpec((None, blk, n_keys), bias_idx)],
        out_specs=pl.BlockSpec((blk, nq), lambda b, n, s: (q_idx(b, n), 0)),
    )
    return pl.pallas_call(
        _attn_kernel,
        out_shape=jax.ShapeDtypeStruct(q.shape, BF16),
        grid_spec=grid_spec,
        compiler_params=_params("parallel", "arbitrary"),
        name="window_gqa",
    )(sink, q, k, k, k, v, v, v, k, v, bias)


def _s5_kernel(u_ref, t_ref, ws_ref, wc_ref, al_ref, d_ref, y_ref, s_scr, h_scr, *, batch, n_lat, n_ctx):
    half = GROUPS_PER_STEP * SSM_STATE
    u = u_ref[0]
    s_scr[...] = jnp.dot(u, ws_ref[0], preferred_element_type=F32)
    sub = 8
    ctx0 = batch * n_lat
    zero = jnp.zeros((sub, half), F32)
    sub_row = lax.broadcasted_iota(jnp.int32, (sub, half), 0)

    def rep(v):
        return jnp.broadcast_to(v, (sub, half))

    def tables(row, fwd):
        pw = _powers(al_ref[0, row:row + 1, :], al_ref[0, row + 1:row + 2, :], sub)
        levels = []
        for k in (1, 2, 4):
            ok = (sub_row >= k) if fwd else (sub_row <= sub - 1 - k)
            levels.append((k if fwd else sub - k, jnp.where(ok, rep(pw[k][0]), 0.0), jnp.where(ok, rep(pw[k][1]), 0.0)))
        c_re, c_im = zero, zero
        for r in range(sub):
            e = r if fwd else sub - 1 - r
            c_re = jnp.where(sub_row == r, rep(pw[e][0]), c_re)
            c_im = jnp.where(sub_row == r, rep(pw[e][1]), c_im)
        inner = (sub_row >= 1) if fwd else (sub_row <= sub - 2)
        return dict(levels=levels, carry_w=(c_re, c_im), block_w=(rep(pw[sub][0]), rep(pw[sub][1])),
                    shift=1 if fwd else sub - 1, inner=inner, last=sub - 1 if fwd else 0)

    tbl_f, tbl_b = tables(0, True), tables(2, False)

    def scan_block(h, tbl, base, col, keep):
        h_re, h_im = h
        base = pl.multiple_of(base, sub)
        x_re = s_scr[pl.ds(base, sub), col:col + half]
        x_im = s_scr[pl.ds(base, sub), col + half:col + 2 * half]
        for shift, c_re, c_im in tbl["levels"]:
            r_re, r_im = pltpu.roll(x_re, shift, 0), pltpu.roll(x_im, shift, 0)
            x_re, x_im = x_re + c_re * r_re - c_im * r_im, x_im + c_re * r_im + c_im * r_re
        if keep:
            w_re, w_im = tbl["carry_w"]
            p_re = jnp.where(tbl["inner"], pltpu.roll(x_re, tbl["shift"], 0), 0.0)
            p_im = jnp.where(tbl["inner"], pltpu.roll(x_im, tbl["shift"], 0), 0.0)
            h_scr[pl.ds(base, sub), col:col + half] = w_re * h_re - w_im * h_im + p_re
            h_scr[pl.ds(base, sub), col + half:col + 2 * half] = w_re * h_im + w_im * h_re + p_im
        b_re, b_im = tbl["block_w"]
        last = tbl["last"]
        t_re, t_im = rep(x_re[last:last + 1, :]), rep(x_im[last:last + 1, :])
        return b_re * h_re - b_im * h_im + t_re, b_re * h_im + b_im * h_re + t_im

    def make_step(first_row, n_rows, keep):
        def step(m, carry):
            out = []
            for b in range(batch):
                hf, hb = carry[2 * b], carry[2 * b + 1]
                lo = first_row + b * n_rows
                hf = scan_block(hf, tbl_f, lo + m * sub, 0, keep)
                hb = scan_block(hb, tbl_b, lo + n_rows - sub - m * sub, 2 * half, keep)
                out += [hf, hb]
            return tuple(out)
        return step

    carry = lax.fori_loop(0, n_ctx // sub, make_step(ctx0, n_ctx, False), ((zero, zero),) * (2 * batch))
    lax.fori_loop(0, n_lat // sub, make_step(0, n_lat, True), carry)

    rows = batch * n_lat
    u_lat = u[:rows]
    y_ref[0] = (jnp.dot(u_lat, t_ref[0], preferred_element_type=F32)
                + jnp.dot(h_scr[...].astype(BF16), wc_ref[0], preferred_element_type=F32)
                + d_ref[0] * u_lat.astype(F32))


def _s5_core(ug, toep, ws, wc, al, d_lanes, *, batch, n_lat, n_ctx):
    n_steps, chunks, width = ug.shape
    assert n_lat % 8 == 0 and n_ctx % 8 == 0
    rows = batch * n_lat
    return pl.pallas_call(
        functools.partial(_s5_kernel, batch=batch, n_lat=n_lat, n_ctx=n_ctx),
        out_shape=jax.ShapeDtypeStruct((n_steps, rows, width), F32),
        grid=(n_steps,),
        in_specs=[
            pl.BlockSpec((1, chunks, width), lambda j: (j, 0, 0)),
            pl.BlockSpec((1,) + toep.shape[1:], lambda j: (j, 0, 0)),
            pl.BlockSpec((1,) + ws.shape[1:], lambda j: (j, 0, 0)),
            pl.BlockSpec((1,) + wc.shape[1:], lambda j: (j, 0, 0)),
            pl.BlockSpec((1,) + al.shape[1:], lambda j: (j, 0, 0)),
            pl.BlockSpec((1, 1, width), lambda j: (j, 0, 0)),
        ],
        out_specs=pl.BlockSpec((1, rows, width), lambda j: (j, 0, 0)),
        scratch_shapes=[pltpu.VMEM((chunks, 4 * GROUPS_PER_STEP * SSM_STATE), F32),
                        pltpu.VMEM((rows, 4 * GROUPS_PER_STEP * SSM_STATE), F32)],
        compiler_params=_params("parallel"),
        name="s5_chunked",
    )(ug, toep, ws, wc, al, d_lanes)


def _cmul(a_re, a_im, b_re, b_im):
    return a_re * b_re - a_im * b_im, a_re * b_im + a_im * b_re


def _zoh(a_re, a_im, log_dt):
    dt = jnp.exp(log_dt)
    mag = jnp.exp(a_re * dt)
    ab_re, ab_im = mag * jnp.cos(a_im * dt), mag * jnp.sin(a_im * dt)
    den = a_re * a_re + a_im * a_im
    x_re, x_im = ab_re - 1.0, ab_im
    return ab_re, ab_im, (x_re * a_re + x_im * a_im) / den, (x_im * a_re - x_re * a_im) / den


def _powers(ab_re, ab_im, n):
    out = [(jnp.ones_like(ab_re), jnp.zeros_like(ab_im))]
    for _ in range(n):
        out.append(_cmul(out[-1][0], out[-1][1], ab_re, ab_im))
    return out


def _s5_op_kernel(row_ref, col_ref, bt_ref, ct_ref, t_ref, ws_ref, wc_ref, al_ref):
    L = SSM_CHUNK
    pair = ct_ref.shape[-1] // L
    assert L & (L - 1) == 0 and pair & (pair - 1) == 0
    hi = lax.Precision.HIGHEST
    lane_block = lax.broadcasted_iota(jnp.int32, ct_ref.shape[-2:], 1) >> (pair.bit_length() - 1)
    inject, taps, read = [], [], []
    for d in range(2):
        ab_re, ab_im, f_re, f_im = _zoh(*(row_ref[0, 3 * d + r:3 * d + r + 1, :] for r in range(3)))
        pw_row = _powers(ab_re, ab_im, L)
        bb_re, bb_im = _cmul(f_re, f_im, bt_ref[0, d, 0], bt_ref[0, d, 1])
        order = range(L - 1, -1, -1) if d == 0 else range(L)
        inject.append([_cmul(pw_row[e][0], pw_row[e][1], bb_re, bb_im) for e in order])
        sq = [_zoh(*(col_ref[0, :, 3 * d + r:3 * d + r + 1] for r in range(3)))[:2]]
        while len(sq) < L.bit_length() - 1:
            sq.append(_cmul(*sq[-1], *sq[-1]))
        expo = lane_block if d == 0 else (L - 1) - lane_block
        low = (expo & 1) == 1
        p_re, p_im = jnp.where(low, sq[0][0], 1.0), jnp.where(low, sq[0][1], 0.0)
        for b in range(1, len(sq)):
            n_re, n_im = _cmul(p_re, p_im, *sq[b])
            bit = ((expo >> b) & 1) == 1
            p_re, p_im = jnp.where(bit, n_re, p_re), jnp.where(bit, n_im, p_im)
        ca_re, ca_im = _cmul(ct_ref[0, d, 0], ct_ref[0, d, 1], p_re, p_im)
        taps.append(jnp.dot(bb_re, ca_re, precision=hi, preferred_element_type=F32)
                    - jnp.dot(bb_im, ca_im, precision=hi, preferred_element_type=F32))
        rd_re, rd_im = _cmul(ca_re, ca_im, *sq[0])
        read.append((rd_re, -rd_im))
        al_ref[0, 2 * d:2 * d + 1, :] = pw_row[L][0]
        al_ref[0, 2 * d + 1:2 * d + 2, :] = pw_row[L][1]
    al_ref[0, 4:, :] = jnp.zeros((4, al_ref.shape[2]), F32)

    for s in range(L):
        ws_ref[0, s * pair:(s + 1) * pair, :] = jnp.concatenate(
            [inject[0][s][0], inject[0][s][1], inject[1][s][0], inject[1][s][1]], axis=1).astype(ws_ref.dtype)
    rows = read[0][0].shape[0]
    for q, blk in enumerate((read[0][0], read[0][1], read[1][0], read[1][1])):
        wc_ref[0, q * rows:(q + 1) * rows, :] = blk.astype(wc_ref.dtype)
    k_f, k_b = taps
    keep = (L - 1) * pair
    by_lag = jnp.concatenate([k_b[:, :keep], k_b[:, keep:] + k_f[:, :pair], k_f[:, pair:]], axis=1)
    for s in range(L):
        t_ref[0, s * pair:(s + 1) * pair, :] = by_lag[:, (L - 1 - s) * pair:(2 * L - 1 - s) * pair].astype(t_ref.dtype)


def _s5_operators(a_re, a_im, log_dt, b_re, b_im, c_re, c_im):
    n_g, n_p = a_re.shape[1], a_re.shape[2]
    gc = b_re.shape[-1]
    n_j, per = n_g // GROUPS_PER_STEP, GROUPS_PER_STEP
    states, pair, width = per * n_p, per * gc, SSM_CHUNK * per * gc
    lam = jnp.stack([a_re, a_im, jnp.broadcast_to(log_dt[..., None], a_re.shape)], axis=1)
    lam = jnp.transpose(lam.reshape(6, n_j, states), (1, 0, 2))
    lam = jnp.pad(lam, ((0, 0), (0, 2), (0, 0)))
    eye = jnp.eye(per, dtype=F32)
    bt = jnp.stack([b_re, b_im], axis=1).reshape(2, 2, n_j, per, n_p, gc)
    bt = jnp.einsum('drjgpc,gh->jdrgchp', bt, eye).reshape(n_j, 2, 2, pair, states)
    ct = jnp.stack([c_re, c_im], axis=1).reshape(2, 2, n_j, per, gc, n_p)
    ct = jnp.einsum('drjgcp,gh->jdrgphc', ct, eye).reshape(n_j, 2, 2, states, pair)
    ct = jnp.tile(ct, (1, 1, 1, 1, SSM_CHUNK))
    mat = jax.ShapeDtypeStruct((n_j, width, width), BF16)
    return pl.pallas_call(
        _s5_op_kernel,
        out_shape=(mat, mat, mat, jax.ShapeDtypeStruct((n_j, 8, states), F32)),
        grid=(n_j,),
        in_specs=[
            pl.BlockSpec((1, 8, states), lambda j: (j, 0, 0)),
            pl.BlockSpec((1, states, 8), lambda j: (j, 0, 0)),
            pl.BlockSpec((1, 2, 2, pair, states), lambda j: (j, 0, 0, 0, 0)),
            pl.BlockSpec((1, 2, 2, states, width), lambda j: (j, 0, 0, 0, 0)),
        ],
        out_specs=(pl.BlockSpec((1, width, width), lambda j: (j, 0, 0)),) * 3
        + (pl.BlockSpec((1, 8, states), lambda j: (j, 0, 0)),),
        compiler_params=_params("parallel"),
        name="s5_operators",
    )(lam, jnp.transpose(lam, (0, 2, 1)), bt, ct)


def _ssm_out_kernel(y_ref, x_ref, mod_ref, g_ref, w_ref, o_ref, slab_scr, act_even, act_odd, out_scr, *, mi, gi):
    i = pl.program_id(0)

    @pl.when(i == 0)
    def _():
        act_odd[...] = jnp.zeros_like(act_odd)

    @pl.when(i % 2 == 0)
    def _():
        _ssm_out_step(y_ref, x_ref, mod_ref, g_ref, w_ref, o_ref, slab_scr, act_even, act_odd, out_scr, mi=mi, gi=gi)

    @pl.when(i % 2 == 1)
    def _():
        _ssm_out_step(y_ref, x_ref, mod_ref, g_ref, w_ref, o_ref, slab_scr, act_odd, act_even, out_scr, mi=mi, gi=gi)


def _ssm_out_step(y_ref, x_ref, mod_ref, g_ref, w_ref, o_ref, slab_scr, act_new, act_old, out_scr, *, mi, gi):
    n_slab, tm, lanes = slab_scr.shape
    pair = GROUPS_PER_STEP * SSM_GROUP_CH
    per_slab = lanes // pair
    for t in range(SSM_CHUNK):
        for k in range(n_slab):
            slab_scr[k, pl.ds(t, tm // SSM_CHUNK, stride=SSM_CHUNK), :] = jnp.concatenate(
                [y_ref[k * per_slab + jj, :, t * pair:(t + 1) * pair] for jj in range(per_slab)], axis=-1)
    for k in range(n_slab):
        act_new[:, k * lanes:(k + 1) * lanes] = _gelu_tanh(slab_scr[k]).astype(BF16)

    n_c, _, tn = out_scr.shape
    d = n_c * tn
    a = act_old[...]
    ssq = jnp.zeros((tm, 1), F32)
    for cc in range(n_c):
        val = jnp.dot(a, w_ref[:, cc * tn:(cc + 1) * tn], preferred_element_type=F32)
        gate = jnp.dot(a, w_ref[:, d + cc * tn:d + (cc + 1) * tn], preferred_element_type=F32)
        blk = val * _sigmoid(gate)
        out_scr[cc] = blk
        ssq = ssq + jnp.sum(blk * blk, axis=-1, keepdims=True)
    inv = lax.rsqrt(ssq / d + EPS)
    for cc in range(n_c):
        sl = slice(cc * tn, (cc + 1) * tn)
        o_ref[:, sl] = x_ref[:, sl] + mod_ref[0, mi:mi + 1, sl] * (out_scr[cc] * inv * g_ref[gi:gi + 1, sl])


def _ssm_out(yg, xs, mod, g, w_glu, *, mi, gi, n_rows, seg_of_row):
    d = xs.shape[1]
    tm, tn = GLU_TOKEN_TILE, GLU_TILE
    n_c = d // tn
    n_pairs, _, width = yg.shape
    n_tiles = n_rows // tm

    def prev(i):
        return jnp.maximum(i - 1, 0)

    return pl.pallas_call(
        functools.partial(_ssm_out_kernel, mi=mi, gi=gi),
        out_shape=jax.ShapeDtypeStruct((n_rows, d), F32),
        grid=(n_tiles + 1,),
        in_specs=[
            pl.BlockSpec((n_pairs, tm // SSM_CHUNK, width), lambda i: (0, jnp.minimum(i, n_tiles - 1), 0)),
            pl.BlockSpec((tm, d), lambda i: (prev(i), 0)),
            pl.BlockSpec((1, MOD_ROWS, d), lambda i: (seg_of_row(prev(i) * tm), 0, 0)),
            pl.BlockSpec((8, d), lambda i: (0, 0)),
            pl.BlockSpec(w_glu.shape, lambda i: (0, 0), pipeline_mode=pl.Buffered(1)),
        ],
        out_specs=pl.BlockSpec((tm, d), lambda i: (prev(i), 0)),
        scratch_shapes=[pltpu.VMEM((d // LANES, tm, LANES), F32), pltpu.VMEM((tm, d), BF16),
                        pltpu.VMEM((tm, d), BF16), pltpu.VMEM((n_c, tm, tn), F32)],
        compiler_params=_params("arbitrary"),
        name="ssm_glu_out",
    )(yg, xs, mod, g, w_glu)


def _rope_tables(seq_len, extra_rows):
    rows = seq_len // GRID_W
    axis_dim = HEAD_DIM // 2
    inv_freq = ROPE_BASE ** (-jnp.arange(0, axis_dim, 2, dtype=F32) / axis_dim)
    ang_r = jnp.arange(rows, dtype=F32)[:, None] * inv_freq
    ang_c = jnp.arange(GRID_W, dtype=F32)[:, None] * inv_freq
    by_row = lambda t: jnp.repeat(t, GRID_W, axis=0)
    by_col = lambda t: jnp.tile(t, (rows, 1))
    cos = jnp.concatenate([by_row(jnp.cos(ang_r))] * 2 + [by_col(jnp.cos(ang_c))] * 2, axis=-1)
    sin = jnp.concatenate([by_row(-jnp.sin(ang_r)), by_row(jnp.sin(ang_r)),
                           by_col(-jnp.sin(ang_c)), by_col(jnp.sin(ang_c))], axis=-1)
    cos = jnp.concatenate([cos, jnp.ones((extra_rows, HEAD_DIM), F32)], axis=0)
    sin = jnp.concatenate([sin, jnp.zeros((extra_rows, HEAD_DIM), F32)], axis=0)
    return cos, sin


def kernel(x, c, ctx, c_ctx, ada_w, ada_b, norm_g, ffn_w_in, ffn_w_out, attn_w_in, attn_w_out, attn_sink,
           ssm_w_in, ssm_a_re, ssm_a_im, ssm_log_dt, ssm_b_re, ssm_b_im, ssm_c_re, ssm_c_im, ssm_d, ssm_w_glu):
    batch, seq_len, d = x.shape
    c_len = ctx.shape[1]
    depth = ada_w.shape[0]
    tm = TOKEN_TILE
    n_lat_rows, n_ctx_rows = batch * seq_len, batch * c_len
    assert seq_len % tm == 0 and n_ctx_rows % tm == 0 and n_lat_rows % c_len == 0
    assert seq_len % ATT_BLOCK == 0 and c_len % ATT_BLOCK == 0 and seq_len % GRID_W == 0
    tiles_per_batch = seq_len // tm
    lat_tiles = n_lat_rows // tm
    all_tiles = lat_tiles + n_ctx_rows // tm

    def seg_of(i):
        return jnp.minimum(i // tiles_per_batch, batch)

    def pos_of(i):
        return jnp.where(i < lat_tiles, i % tiles_per_batch, tiles_per_batch + (i - lat_tiles))

    def seg_of_row(r):
        return jnp.minimum(r // seq_len, batch)

    xs, tail = x.reshape(n_lat_rows, d), ctx.reshape(n_ctx_rows, d)
    w_ff_in, w_ff_out = ffn_w_in.astype(BF16), ffn_w_out.astype(BF16)

    n_cond = batch + 1
    cvec = jnp.concatenate([c, c_ctx[None], jnp.zeros((8 - n_cond, d), F32)], axis=0)
    mods = _modulation(cvec.T, ada_w, ada_b, n_cond)
    mods = mods[:, :n_cond].reshape(depth, n_cond, N_MOD, d)
    mods = jnp.pad(mods, ((0, 0), (0, 0), (0, MOD_ROWS - N_MOD), (0, 0)))
    gains = jnp.pad(norm_g, ((0, 0), (0, 8 - norm_g.shape[1]), (0, 0)))

    cos, sin = _rope_tables(seq_len, n_ctx_rows)

    for i in range(depth):
        need_ctx = i < depth - 1
        mod, g = mods[i], gains[i]
        j = i // 2
        xs = _ffn(xs, mod, g, w_ff_in, w_ff_out, (i, 0), mi=0, gi=0, n_tiles=all_tiles, seg_of=seg_of, tail=tail)
        tail = None
        post_tiles = all_tiles if need_ctx else lat_tiles
        if i % 2 == 0:
            q, k, v = _qkv(xs, mod, g, attn_w_in[j].astype(BF16), cos, sin,
                           mi=3, gi=2, n_tiles=all_tiles, seg_of=seg_of, pos_of=pos_of)
            o = _attention(q, k, v, attn_sink[j], batch=batch, seq_len=seq_len, c_len=c_len)
            xs = _proj_out(o, xs, mod, g, attn_w_out[j].astype(BF16),
                           mi=5, gi=3, n_tiles=post_tiles, seg_of=seg_of)
        else:
            if need_ctx:
                raise NotImplementedError("context output of the S5 mixer is only needed when another layer follows")
            ug = _proj_in(xs, mod, g, ssm_w_in[j].astype(BF16), mi=3, gi=2, n_tiles=all_tiles, seg_of=seg_of)
            L = SSM_CHUNK
            pair = GROUPS_PER_STEP * SSM_GROUP_CH
            toep, ws, wc, al = _s5_operators(ssm_a_re[j], ssm_a_im[j], ssm_log_dt[j], ssm_b_re[j], ssm_b_im[j],
                                             ssm_c_re[j], ssm_c_im[j])
            d_lanes = jnp.tile(ssm_d[j].reshape(d // pair, 1, pair), (1, 1, L))
            yg = _s5_core(ug, toep, ws, wc, al, d_lanes, batch=batch, n_lat=seq_len // L, n_ctx=c_len // L)
            xs = _ssm_out(yg, xs, mod, g, ssm_w_glu[j].astype(BF16),
                          mi=5, gi=3, n_rows=post_tiles * tm, seg_of_row=seg_of_row)
        xs = _ffn(xs, mod, g, w_ff_in, w_ff_out, (i, 1), mi=6, gi=4, n_tiles=post_tiles, seg_of=seg_of)
    return xs[:n_lat_rows].reshape(batch, seq_len, d)
```

```python
import functools
import math

import jax
import jax.numpy as jnp
from jax import lax
from jax.experimental import pallas as pl
from jax.experimental.pallas import tpu as pltpu

EPS = 1e-6
NEG_INF = -1e30
N_MOD = 9
GRID_W = 64
N_HEADS = 16
N_KV_HEADS = 4
HEAD_DIM = 128
GQA_GROUP = N_HEADS // N_KV_HEADS
ATT_BLOCK = 128
ROPE_BASE = 10000.0
ROPE_QUARTER = HEAD_DIM // 4
SSM_GROUP_CH = 16
SSM_STATE = 64
SSM_CHUNK = 16
GROUPS_PER_STEP = 2
SCAN_ROWS = 8

TOKEN_TILE = 512
FF_TILE = 512
GLU_TILE = 512
GLU_TOKEN_TILE = 256
MOD_TILE = 1024
MOD_ROWS = 16
VMEM_LIMIT = 56 * 1024 * 1024
LANES = 128
SUBLANES = 8

BF16 = jnp.bfloat16
F32 = jnp.float32


def _params(*sem):
    return pltpu.CompilerParams(dimension_semantics=sem, vmem_limit_bytes=VMEM_LIMIT)


def _rms(t, g):
    return t * lax.rsqrt(jnp.mean(t * t, axis=-1, keepdims=True) + EPS) * g


def _sigmoid(t):
    return 1.0 / (1.0 + jnp.exp(-t))


def _gelu_tanh(t):
    return 0.5 * t * (1.0 + jnp.tanh(math.sqrt(2.0 / math.pi) * (t + 0.044715 * (t * t * t))))


def _mod_kernel(ct_ref, w_ref, b_ref, o_ref, *, n_rows):
    w = w_ref[0]
    for r in range(n_rows):
        col = ct_ref[:, r:r + 1]
        col = col * _sigmoid(col)
        o_ref[0, r:r + 1, :] = jnp.sum(w * col, axis=0, keepdims=True) + b_ref[0]
    o_ref[0, n_rows:, :] = jnp.zeros((o_ref.shape[1] - n_rows, o_ref.shape[2]), F32)


def _modulation(cvec_t, ada_w, ada_b, n_rows):
    depth, d, nd = ada_w.shape
    tn = MOD_TILE
    return pl.pallas_call(
        functools.partial(_mod_kernel, n_rows=n_rows),
        out_shape=jax.ShapeDtypeStruct((depth, SUBLANES, nd), F32),
        grid=(depth, nd // tn),
        in_specs=[
            pl.BlockSpec((d, SUBLANES), lambda l, j: (0, 0)),
            pl.BlockSpec((1, d, tn), lambda l, j: (l, 0, j)),
            pl.BlockSpec((1, 1, tn), lambda l, j: (l, 0, j)),
        ],
        out_specs=pl.BlockSpec((1, SUBLANES, tn), lambda l, j: (l, 0, j)),
        compiler_params=_params("arbitrary", "arbitrary"),
        name="adaln_mod",
    )(cvec_t, ada_w, ada_b.reshape(depth, 1, nd))


def _ffn_kernel(x_ref, tail_ref, mod_ref, g_ref, wg_ref, wu_ref, wo_ref, o_ref, h_scr, acc_scr,
                *, mi, gi, head_tiles, n_ff):
    def read_x():
        if head_tiles is None:
            return x_ref[...]
        return jnp.where(pl.program_id(0) < head_tiles, x_ref[...], tail_ref[...])

    def step(first, last):
        if first:
            xv = read_x()
            inv = lax.rsqrt(jnp.mean(xv * xv, axis=-1, keepdims=True) + EPS)
            h = (xv * inv) * (g_ref[gi:gi + 1, :] * (1.0 + mod_ref[0, mi + 1:mi + 2, :])) + mod_ref[0, mi:mi + 1, :]
            h = h.astype(BF16)
            if not last:
                h_scr[...] = h
        else:
            h = h_scr[...]
        gate = jnp.dot(h, wg_ref[...], preferred_element_type=F32)
        up = jnp.dot(h, wu_ref[...], preferred_element_type=F32)
        act = (gate * _sigmoid(gate) * up).astype(BF16)
        out = jnp.dot(act, wo_ref[...], preferred_element_type=F32)
        if not first:
            out = acc_scr[...] + out
        if last:
            inv = lax.rsqrt(jnp.mean(out * out, axis=-1, keepdims=True) + EPS)
            o_ref[...] = read_x() + (out * inv) * (0.5 * mod_ref[0, mi + 2:mi + 3, :] * g_ref[gi + 1:gi + 2, :])
        else:
            acc_scr[...] = out

    f = pl.program_id(1)
    if n_ff == 1:
        step(True, True)
    else:
        pl.when(f == 0)(functools.partial(step, True, False))
        if n_ff > 2:
            pl.when((f > 0) & (f < n_ff - 1))(functools.partial(step, False, False))
        pl.when(f == n_ff - 1)(functools.partial(step, False, True))


def _ffn(xs, mod, g, w_in, w_out, which, *, mi, gi, n_tiles, seg_of, tail=None):
    d = xs.shape[1]
    ff = w_out.shape[2]
    tm, tf = TOKEN_TILE, FF_TILE
    nf = ff // tf
    layer, half = which
    if tail is None:
        head_tiles = None
        tokens = [xs, xs]
        token_specs = [pl.BlockSpec((tm, d), lambda i, f: (i, 0)), pl.BlockSpec((SUBLANES, d),lambda i, f: (0, 0))]
    else:
        head_tiles = xs.shape[0] // tm
        tokens = [xs, tail]
        token_specs = [pl.BlockSpec((tm, d), lambda i, f: (jnp.minimum(i, head_tiles - 1), 0)),
                       pl.BlockSpec((tm, d), lambda i, f: (jnp.maximum(i - head_tiles, 0), 0),
                                    pipeline_mode=pl.Buffered(1))]
    return pl.pallas_call(
        functools.partial(_ffn_kernel, mi=mi, gi=gi, head_tiles=head_tiles, n_ff=nf),
        out_shape=jax.ShapeDtypeStruct((n_tiles * tm, d), F32),
        grid=(n_tiles, nf),
        in_specs=token_specs + [
            pl.BlockSpec((1, MOD_ROWS, d), lambda i, f: (seg_of(i), 0, 0)),
            pl.BlockSpec((SUBLANES, d),lambda i, f: (0, 0)),
            pl.BlockSpec((None, None, d, tf), lambda i, f: (layer, half, 0, f)),
            pl.BlockSpec((None, None, d, tf), lambda i, f: (layer, half, 0, nf + f)),
            pl.BlockSpec((None, None, tf, d), lambda i, f: (layer, half, f, 0)),
        ],
        out_specs=pl.BlockSpec((tm, d), lambda i, f: (i, 0)),
        scratch_shapes=[pltpu.VMEM((tm, d), BF16), pltpu.VMEM((tm, d), F32)],
        compiler_params=_params("parallel", "arbitrary"),
        name="ffn_halfstep",
    )(*tokens, mod, g, w_in, w_in, w_out)


def _rope(t, cos, sin, even_quarter):
    partner = jnp.where(even_quarter, pltpu.roll(t, HEAD_DIM - ROPE_QUARTER, 1), pltpu.roll(t, ROPE_QUARTER, 1))
    return t * cos + partner * sin


def _qkv_kernel(x_ref, mod_ref, g_ref, w_ref, cos_ref, sin_ref, q_ref, k_ref, v_ref, *, mi, gi):
    h = _rms(x_ref[...], g_ref[gi:gi + 1, :]) * (1.0 + mod_ref[0, mi + 1:mi + 2, :]) + mod_ref[0, mi:mi + 1, :]
    qkv = jnp.dot(h.astype(BF16), w_ref[...], preferred_element_type=F32)
    cos, sin = cos_ref[...], sin_ref[...]
    lane = lax.broadcasted_iota(jnp.int32, cos.shape, 1)
    even_quarter = (lane & ROPE_QUARTER) == 0
    scale = HEAD_DIM ** -0.5
    nq, nk = N_HEADS * HEAD_DIM, N_KV_HEADS * HEAD_DIM
    for hh in range(N_HEADS):
        sl = slice(hh * HEAD_DIM, (hh + 1) * HEAD_DIM)
        q_ref[:, sl] = (_rope(qkv[:, sl], cos, sin, even_quarter) * scale).astype(BF16)
    for hh in range(N_KV_HEADS):
        sl = slice(hh * HEAD_DIM, (hh + 1) * HEAD_DIM)
        src = slice(nq + hh * HEAD_DIM, nq + (hh + 1) * HEAD_DIM)
        k_ref[:, sl] = _rope(qkv[:, src], cos, sin, even_quarter).astype(BF16)
    v_ref[...] = qkv[:, nq + nk:].astype(BF16)


def _qkv(xs, mod, g, w, cos, sin, *, mi, gi, n_tiles, seg_of, pos_of):
    d = xs.shape[1]
    tm = TOKEN_TILE
    nq, nk = N_HEADS * HEAD_DIM, N_KV_HEADS * HEAD_DIM
    rows = n_tiles * tm
    return pl.pallas_call(
        functools.partial(_qkv_kernel, mi=mi, gi=gi),
        out_shape=(jax.ShapeDtypeStruct((rows, nq), BF16), jax.ShapeDtypeStruct((rows, nk), BF16),
                   jax.ShapeDtypeStruct((rows, nk), BF16)),
        grid=(n_tiles,),
        in_specs=[
            pl.BlockSpec((tm, d), lambda i: (i, 0)),
            pl.BlockSpec((1, MOD_ROWS, d), lambda i: (seg_of(i), 0, 0)),
            pl.BlockSpec((SUBLANES, d),lambda i: (0, 0)),
            pl.BlockSpec((d, nq + 2 * nk), lambda i: (0, 0), pipeline_mode=pl.Buffered(1)),
            pl.BlockSpec((tm, HEAD_DIM), lambda i: (pos_of(i), 0)),
            pl.BlockSpec((tm, HEAD_DIM), lambda i: (pos_of(i), 0)),
        ],
        out_specs=(pl.BlockSpec((tm, nq), lambda i: (i, 0)), pl.BlockSpec((tm, nk), lambda i: (i, 0)),
                   pl.BlockSpec((tm, nk), lambda i: (i, 0))),
        compiler_params=_params("parallel"),
        name="attn_qkv_rope",
    )(xs, mod, g, w, cos, sin)


def _proj_in_kernel(x_ref, mod_ref, g_ref, w_ref, o_ref, slab_scr, *, mi, gi):
    h = _rms(x_ref[...], g_ref[gi:gi + 1, :]) * (1.0 + mod_ref[0, mi + 1:mi + 2, :]) + mod_ref[0, mi:mi + 1, :]
    u = jnp.dot(h.astype(BF16), w_ref[...], preferred_element_type=F32)
    n_slab, tm, lanes = slab_scr.shape
    pair = GROUPS_PER_STEP * SSM_GROUP_CH
    per_slab = lanes // pair
    for k in range(n_slab):
        slab_scr[k] = u[:, k * lanes:(k + 1) * lanes]
    for t in range(SSM_CHUNK):
        for k in range(n_slab):
            rows = slab_scr[k, pl.ds(t, tm // SSM_CHUNK, stride=SSM_CHUNK), :]
            for jj in range(per_slab):
                o_ref[k * per_slab + jj, :, t * pair:(t + 1) * pair] = rows[:, jj * pair:(jj + 1) * pair].astype(o_ref.dtype)


def _proj_in(xs, mod, g, w, *, mi, gi, n_tiles, seg_of):
    d = xs.shape[1]
    tm = TOKEN_TILE
    pair = GROUPS_PER_STEP * SSM_GROUP_CH
    n_pairs = w.shape[1] // pair
    return pl.pallas_call(
        functools.partial(_proj_in_kernel, mi=mi, gi=gi),
        out_shape=jax.ShapeDtypeStruct((n_pairs, n_tiles * tm // SSM_CHUNK, SSM_CHUNK * pair), BF16),
        grid=(n_tiles,),
        in_specs=[
            pl.BlockSpec((tm, d), lambda i: (i, 0)),
            pl.BlockSpec((1, MOD_ROWS, d), lambda i: (seg_of(i), 0, 0)),
            pl.BlockSpec((SUBLANES, d),lambda i: (0, 0)),
            pl.BlockSpec(w.shape, lambda i: (0, 0), pipeline_mode=pl.Buffered(1)),
        ],
        out_specs=pl.BlockSpec((n_pairs, tm // SSM_CHUNK, SSM_CHUNK * pair), lambda i: (0, i, 0)),
        scratch_shapes=[pltpu.VMEM((w.shape[1] // LANES, tm, LANES), F32)],
        compiler_params=_params("parallel"),
        name="ssm_in_proj",
    )(xs, mod, g, w)


def _proj_out_kernel(a_ref, x_ref, mod_ref, g_ref, w_ref, o_ref, *, mi, gi):
    out = jnp.dot(a_ref[...], w_ref[...], preferred_element_type=F32)
    o_ref[...] = x_ref[...] + mod_ref[0, mi:mi + 1, :] * _rms(out, g_ref[gi:gi + 1, :])


def _proj_out(a, xs, mod, g, w, *, mi, gi, n_tiles, seg_of):
    d = xs.shape[1]
    tm = TOKEN_TILE
    return pl.pallas_call(
        functools.partial(_proj_out_kernel, mi=mi, gi=gi),
        out_shape=jax.ShapeDtypeStruct((n_tiles * tm, d), F32),
        grid=(n_tiles,),
        in_specs=[
            pl.BlockSpec((tm, a.shape[1]), lambda i: (i, 0)),
            pl.BlockSpec((tm, d), lambda i: (i, 0)),
            pl.BlockSpec((1, MOD_ROWS, d), lambda i: (seg_of(i), 0, 0)),
            pl.BlockSpec((SUBLANES, d),lambda i: (0, 0)),
            pl.BlockSpec(w.shape, lambda i: (0, 0), pipeline_mode=pl.Buffered(1)),
        ],
        out_specs=pl.BlockSpec((tm, d), lambda i: (i, 0)),
        compiler_params=_params("parallel"),
        name="attn_out_proj",
    )(a, xs, mod, g, w)


def _attn_kernel(sink_ref, q_ref, kp_ref, ko_ref, kn_ref, vp_ref, vo_ref, vn_ref, kc_ref, vc_ref, bias_ref, o_ref):
    blk = ATT_BLOCK
    bias = jnp.concatenate([bias_ref[...]] * GQA_GROUP, axis=0)
    row = lax.broadcasted_iota(jnp.int32, (GQA_GROUP * blk, 1), 0)
    for h in range(N_KV_HEADS):
        hs = slice(h * HEAD_DIM, (h + 1) * HEAD_DIM)
        qs = jnp.concatenate(
            [q_ref[:, (h * GQA_GROUP + gq) * HEAD_DIM:(h * GQA_GROUP + gq + 1) * HEAD_DIM] for gq in range(GQA_GROUP)],
            axis=0)
        sink = jnp.full((GQA_GROUP * blk, 1), sink_ref[h * GQA_GROUP], F32)
        for gq in range(1, GQA_GROUP):
            sink = jnp.where(row >= gq * blk, sink_ref[h * GQA_GROUP + gq], sink)
        scores, col = [], 0
        for k_ref in (kp_ref, ko_ref, kn_ref, kc_ref):
            s = lax.dot_general(qs, k_ref[:, hs], (((1,), (1,)), ((), ())), preferred_element_type=F32)
            if k_ref is not kc_ref:
                s = s + bias[:, col:col + k_ref.shape[0]]
            col += k_ref.shape[0]
            scores.append(s)

        def lane_slabs(t):
            return [t[:, c:c + LANES] for c in range(0, t.shape[1], LANES)]

        m = functools.reduce(jnp.maximum, [sl for s in scores for sl in lane_slabs(s)])
        m = jnp.maximum(jnp.max(m, axis=-1, keepdims=True), sink)
        probs = [jnp.exp(s - m) for s in scores]
        denom = jnp.sum(functools.reduce(jnp.add, [sl for p in probs for sl in lane_slabs(p)]), axis=-1, keepdims=True)
        denom = denom + jnp.exp(sink - m)
        o = jnp.zeros((GQA_GROUP * blk, HEAD_DIM), F32)
        for p, v_ref in zip(probs, (vp_ref, vo_ref, vn_ref, vc_ref)):
            o = o + jnp.dot(p.astype(BF16), v_ref[:, hs], preferred_element_type=F32)
        o = o / denom
        for gq in range(GQA_GROUP):
            col = (h * GQA_GROUP + gq) * HEAD_DIM
            o_ref[:, col:col + HEAD_DIM] = o[gq * blk:(gq + 1) * blk, :].astype(o_ref.dtype)


def _attention(q, k, v, sink, *, batch, seq_len, c_len):
    blk = ATT_BLOCK
    n_blk, c_blk = seq_len // blk, c_len // blk
    nq, nk = q.shape[1], k.shape[1]
    lat_blocks = batch * n_blk
    ctx_block0 = batch * seq_len // c_len

    def q_idx(b, n):
        return jnp.where(n < n_blk, b * n_blk + n, lat_blocks + b * c_blk + (n - n_blk))

    def band_idx(off):
        def idx(b, n, s):
            nn = jnp.clip(n + off, 0, n_blk - 1)
            return (b * n_blk + nn, 0)
        return idx

    assert n_blk >= 2
    n_keys = 3 * blk + c_len
    qi = jnp.arange(blk)[:, None]
    kj = jnp.arange(n_keys)[None, :]
    in_band = (kj - qi >= 0) & (kj - qi <= 2 * blk)
    cases = [in_band & (kj >= blk), in_band, in_band & (kj < 2 * blk), jnp.zeros_like(in_band)]
    bias = jnp.stack([jnp.where((kj < 3 * blk) & ~case, NEG_INF, 0.0).astype(F32) for case in cases])

    def bias_idx(b, n, s):
        return (jnp.where(n == 0, 0, jnp.where(n < n_blk - 1, 1, jnp.where(n == n_blk - 1, 2, 3))), 0, 0)

    kv_spec = [pl.BlockSpec((blk, nk), band_idx(off)) for off in (-1, 0, 1)]
    ctx_spec = pl.BlockSpec((c_len, nk), lambda b, n, s: (ctx_block0 + b, 0))
    grid_spec = pltpu.PrefetchScalarGridSpec(
        num_scalar_prefetch=1,
        grid=(batch, n_blk + c_blk),
        in_specs=[pl.BlockSpec((blk, nq), lambda b, n, s: (q_idx(b, n), 0))] + kv_spec + kv_spec + [ctx_spec, ctx_spec]
        + [pl.BlockSpec((None, blk, n_keys), bias_idx)],
        out_specs=pl.BlockSpec((blk, nq), lambda b, n, s: (q_idx(b, n), 0)),
    )
    return pl.pallas_call(
        _attn_kernel,
        out_shape=jax.ShapeDtypeStruct(q.shape, BF16),
        grid_spec=grid_spec,
        compiler_params=_params("parallel", "arbitrary"),
        name="window_gqa",
    )(sink, q, k, k, k, v, v, v, k, v, bias)


def _s5_kernel(u_ref, t_ref, ws_ref, wc_ref, al_ref, d_ref, y_ref, s_scr, h_scr, *, batch, n_lat, n_ctx):
    half = GROUPS_PER_STEP * SSM_STATE
    u = u_ref[0]
    s_scr[...] = jnp.dot(u, ws_ref[0], preferred_element_type=F32)
    sub = SCAN_ROWS
    ctx0 = batch * n_lat
    zero = jnp.zeros((sub, half), F32)
    sub_row = lax.broadcasted_iota(jnp.int32, (sub, half), 0)

    def rep(v):
        return jnp.broadcast_to(v, (sub, half))

    def tables(row, fwd):
        pw = _powers(al_ref[0, row:row + 1, :], al_ref[0, row + 1:row + 2, :], sub)
        levels = []
        for k in (1 << b for b in range(sub.bit_length() - 1)):
            ok = (sub_row >= k) if fwd else (sub_row <= sub - 1 - k)
            levels.append((k if fwd else sub - k, jnp.where(ok, rep(pw[k][0]), 0.0), jnp.where(ok, rep(pw[k][1]), 0.0)))
        c_re, c_im = zero, zero
        for r in range(sub):
            e = r if fwd else sub - 1 - r
            c_re = jnp.where(sub_row == r, rep(pw[e][0]), c_re)
            c_im = jnp.where(sub_row == r, rep(pw[e][1]), c_im)
        inner = (sub_row >= 1) if fwd else (sub_row <= sub - 2)
        return dict(levels=levels, carry_w=(c_re, c_im), block_w=(rep(pw[sub][0]), rep(pw[sub][1])),
                    shift=1 if fwd else sub - 1, inner=inner, last=sub - 1 if fwd else 0)

    tbl_f, tbl_b = tables(0, True), tables(2, False)

    def scan_block(h, tbl, base, col, keep):
        h_re, h_im = h
        base = pl.multiple_of(base, sub)
        x_re = s_scr[pl.ds(base, sub), col:col + half]
        x_im = s_scr[pl.ds(base, sub), col + half:col + 2 * half]
        for shift, c_re, c_im in tbl["levels"]:
            r_re, r_im = pltpu.roll(x_re, shift, 0), pltpu.roll(x_im, shift, 0)
            x_re, x_im = x_re + c_re * r_re - c_im * r_im, x_im + c_re * r_im + c_im * r_re
        if keep:
            w_re, w_im = tbl["carry_w"]
            p_re = jnp.where(tbl["inner"], pltpu.roll(x_re, tbl["shift"], 0), 0.0)
            p_im = jnp.where(tbl["inner"], pltpu.roll(x_im, tbl["shift"], 0), 0.0)
            h_scr[pl.ds(base, sub), col:col + half] = w_re * h_re - w_im * h_im + p_re
            h_scr[pl.ds(base, sub), col + half:col + 2 * half] = w_re * h_im + w_im * h_re + p_im
        b_re, b_im = tbl["block_w"]
        last = tbl["last"]
        t_re, t_im = rep(x_re[last:last + 1, :]), rep(x_im[last:last + 1, :])
        return b_re * h_re - b_im * h_im + t_re, b_re * h_im + b_im * h_re + t_im

    def make_step(first_row, n_rows, keep):
        def step(m, carry):
            out = []
            for b in range(batch):
                hf, hb = carry[2 * b], carry[2 * b + 1]
                lo = first_row + b * n_rows
                hf = scan_block(hf, tbl_f, lo + m * sub, 0, keep)
                hb = scan_block(hb, tbl_b, lo + n_rows - sub - m * sub, 2 * half, keep)
                out += [hf, hb]
            return tuple(out)
        return step

    carry = lax.fori_loop(0, n_ctx // sub, make_step(ctx0, n_ctx, False), ((zero, zero),) * (2 * batch))
    lax.fori_loop(0, n_lat // sub, make_step(0, n_lat, True), carry)

    rows = batch * n_lat
    u_lat = u[:rows]
    y_ref[0] = (jnp.dot(u_lat, t_ref[0], preferred_element_type=F32)
                + jnp.dot(h_scr[...].astype(BF16), wc_ref[0], preferred_element_type=F32)
                + d_ref[0] * u_lat.astype(F32))


def _s5_core(ug, toep, ws, wc, al, d_lanes, *, batch, n_lat, n_ctx):
    n_steps, chunks, width = ug.shape
    assert n_lat % SCAN_ROWS == 0 and n_ctx % SCAN_ROWS == 0 and SCAN_ROWS & (SCAN_ROWS - 1) == 0
    rows = batch * n_lat
    return pl.pallas_call(
        functools.partial(_s5_kernel, batch=batch, n_lat=n_lat, n_ctx=n_ctx),
        out_shape=jax.ShapeDtypeStruct((n_steps, rows, width), F32),
        grid=(n_steps,),
        in_specs=[
            pl.BlockSpec((1, chunks, width), lambda j: (j, 0, 0)),
            pl.BlockSpec((1,) + toep.shape[1:], lambda j: (j, 0, 0)),
            pl.BlockSpec((1,) + ws.shape[1:], lambda j: (j, 0, 0)),
            pl.BlockSpec((1,) + wc.shape[1:], lambda j: (j, 0, 0)),
            pl.BlockSpec((1,) + al.shape[1:], lambda j: (j, 0, 0)),
            pl.BlockSpec((1, 1, width), lambda j: (j, 0, 0)),
        ],
        out_specs=pl.BlockSpec((1, rows, width), lambda j: (j, 0, 0)),
        scratch_shapes=[pltpu.VMEM((chunks, 4 * GROUPS_PER_STEP * SSM_STATE), F32),
                        pltpu.VMEM((rows, 4 * GROUPS_PER_STEP * SSM_STATE), F32)],
        compiler_params=_params("parallel"),
        name="s5_chunked",
    )(ug, toep, ws, wc, al, d_lanes)


def _cmul(a_re, a_im, b_re, b_im):
    return a_re * b_re - a_im * b_im, a_re * b_im + a_im * b_re


def _zoh(a_re, a_im, log_dt):
    dt = jnp.exp(log_dt)
    mag = jnp.exp(a_re * dt)
    ab_re, ab_im = mag * jnp.cos(a_im * dt), mag * jnp.sin(a_im * dt)
    den = a_re * a_re + a_im * a_im
    x_re, x_im = ab_re - 1.0, ab_im
    return ab_re, ab_im, (x_re * a_re + x_im * a_im) / den, (x_im * a_re - x_re * a_im) / den


def _powers(ab_re, ab_im, n):
    out = [(jnp.ones_like(ab_re), jnp.zeros_like(ab_im))]
    for _ in range(n):
        out.append(_cmul(out[-1][0], out[-1][1], ab_re, ab_im))
    return out


def _s5_op_kernel(row_ref, col_ref, bt_ref, ct_ref, t_ref, ws_ref, wc_ref, al_ref):
    L = SSM_CHUNK
    pair = ct_ref.shape[-1] // L
    assert L & (L - 1) == 0 and pair & (pair - 1) == 0
    hi = lax.Precision.HIGHEST
    lane_block = lax.broadcasted_iota(jnp.int32, ct_ref.shape[-2:], 1) >> (pair.bit_length() - 1)
    inject, taps, read = [], [], []
    for d in range(2):
        ab_re, ab_im, f_re, f_im = _zoh(*(row_ref[0, 3 * d + r:3 * d + r + 1, :] for r in range(3)))
        pw_row = _powers(ab_re, ab_im, L)
        bb_re, bb_im = _cmul(f_re, f_im, bt_ref[0, d, 0], bt_ref[0, d, 1])
        order = range(L - 1, -1, -1) if d == 0 else range(L)
        inject.append([_cmul(pw_row[e][0], pw_row[e][1], bb_re, bb_im) for e in order])
        sq = [_zoh(*(col_ref[0, :, 3 * d + r:3 * d + r + 1] for r in range(3)))[:2]]
        while len(sq) < L.bit_length() - 1:
            sq.append(_cmul(*sq[-1], *sq[-1]))
        expo = lane_block if d == 0 else (L - 1) - lane_block
        low = (expo & 1) == 1
        p_re, p_im = jnp.where(low, sq[0][0], 1.0), jnp.where(low, sq[0][1], 0.0)
        for b in range(1, len(sq)):
            n_re, n_im = _cmul(p_re, p_im, *sq[b])
            bit = ((expo >> b) & 1) == 1
            p_re, p_im = jnp.where(bit, n_re, p_re), jnp.where(bit, n_im, p_im)
        ca_re, ca_im = _cmul(ct_ref[0, d, 0], ct_ref[0, d, 1], p_re, p_im)
        taps.append(jnp.dot(bb_re, ca_re, precision=hi, preferred_element_type=F32)
                    - jnp.dot(bb_im, ca_im, precision=hi, preferred_element_type=F32))
        rd_re, rd_im = _cmul(ca_re, ca_im, *sq[0])
        read.append((rd_re, -rd_im))
        al_ref[0, 2 * d:2 * d + 1, :] = pw_row[L][0]
        al_ref[0, 2 * d + 1:2 * d + 2, :] = pw_row[L][1]
    al_ref[0, 4:, :] = jnp.zeros((al_ref.shape[1] - 4, al_ref.shape[2]), F32)

    for s in range(L):
        ws_ref[0, s * pair:(s + 1) * pair, :] = jnp.concatenate(
            [inject[0][s][0], inject[0][s][1], inject[1][s][0], inject[1][s][1]], axis=1).astype(ws_ref.dtype)
    rows = read[0][0].shape[0]
    for q, blk in enumerate((read[0][0], read[0][1], read[1][0], read[1][1])):
        wc_ref[0, q * rows:(q + 1) * rows, :] = blk.astype(wc_ref.dtype)
    k_f, k_b = taps
    keep = (L - 1) * pair
    by_lag = jnp.concatenate([k_b[:, :keep], k_b[:, keep:] + k_f[:, :pair], k_f[:, pair:]], axis=1)
    for s in range(L):
        t_ref[0, s * pair:(s + 1) * pair, :] = by_lag[:, (L - 1 - s) * pair:(2 * L - 1 - s) * pair].astype(t_ref.dtype)


def _s5_operators(a_re, a_im, log_dt, b_re, b_im, c_re, c_im):
    n_g, n_p = a_re.shape[1], a_re.shape[2]
    gc = b_re.shape[-1]
    n_j, per = n_g // GROUPS_PER_STEP, GROUPS_PER_STEP
    states, pair, width = per * n_p, per * gc, SSM_CHUNK * per * gc
    lam = jnp.stack([a_re, a_im, jnp.broadcast_to(log_dt[..., None], a_re.shape)], axis=1)
    lam = jnp.transpose(lam.reshape(6, n_j, states), (1, 0, 2))
    lam = jnp.pad(lam, ((0, 0), (0, SUBLANES - lam.shape[1]), (0, 0)))
    eye = jnp.eye(per, dtype=F32)
    bt = jnp.stack([b_re, b_im], axis=1).reshape(2, 2, n_j, per, n_p, gc)
    bt = jnp.einsum('drjgpc,gh->jdrgchp', bt, eye).reshape(n_j, 2, 2, pair, states)
    ct = jnp.stack([c_re, c_im], axis=1).reshape(2, 2, n_j, per, gc, n_p)
    ct = jnp.einsum('drjgcp,gh->jdrgphc', ct, eye).reshape(n_j, 2, 2, states, pair)
    ct = jnp.tile(ct, (1, 1, 1, 1, SSM_CHUNK))
    mat = jax.ShapeDtypeStruct((n_j, width, width), BF16)
    return pl.pallas_call(
        _s5_op_kernel,
        out_shape=(mat, mat, mat, jax.ShapeDtypeStruct((n_j, SUBLANES, states), F32)),
        grid=(n_j,),
        in_specs=[
            pl.BlockSpec((1, SUBLANES, states), lambda j: (j, 0, 0)),
            pl.BlockSpec((1, states, SUBLANES), lambda j: (j, 0, 0)),
            pl.BlockSpec((1, 2, 2, pair, states), lambda j: (j, 0, 0, 0, 0)),
            pl.BlockSpec((1, 2, 2, states, width), lambda j: (j, 0, 0, 0, 0)),
        ],
        out_specs=(pl.BlockSpec((1, width, width), lambda j: (j, 0, 0)),) * 3
        + (pl.BlockSpec((1, SUBLANES, states), lambda j: (j, 0, 0)),),
        compiler_params=_params("parallel"),
        name="s5_operators",
    )(lam, jnp.transpose(lam, (0, 2, 1)), bt, ct)


def _ssm_out_kernel(y_ref, x_ref, mod_ref, g_ref, w_ref, o_ref, slab_scr, act_even, act_odd, out_scr, *, mi, gi):
    i = pl.program_id(0)

    @pl.when(i == 0)
    def _():
        act_odd[...] = jnp.zeros_like(act_odd)

    @pl.when(i % 2 == 0)
    def _():
        _ssm_out_step(y_ref, x_ref, mod_ref, g_ref, w_ref, o_ref, slab_scr, act_even, act_odd, out_scr, mi=mi, gi=gi)

    @pl.when(i % 2 == 1)
    def _():
        _ssm_out_step(y_ref, x_ref, mod_ref, g_ref, w_ref, o_ref, slab_scr, act_odd, act_even, out_scr, mi=mi, gi=gi)


def _ssm_out_step(y_ref, x_ref, mod_ref, g_ref, w_ref, o_ref, slab_scr, act_new, act_old, out_scr, *, mi, gi):
    n_slab, tm, lanes = slab_scr.shape
    pair = GROUPS_PER_STEP * SSM_GROUP_CH
    per_slab = lanes // pair
    for t in range(SSM_CHUNK):
        for k in range(n_slab):
            slab_scr[k, pl.ds(t, tm // SSM_CHUNK, stride=SSM_CHUNK), :] = jnp.concatenate(
                [y_ref[k * per_slab + jj, :, t * pair:(t + 1) * pair] for jj in range(per_slab)], axis=-1)
    for k in range(n_slab):
        act_new[:, k * lanes:(k + 1) * lanes] = _gelu_tanh(slab_scr[k]).astype(BF16)

    n_c, _, tn = out_scr.shape
    d = n_c * tn
    a = act_old[...]
    ssq = jnp.zeros((tm, 1), F32)
    for cc in range(n_c):
        val = jnp.dot(a, w_ref[:, cc * tn:(cc + 1) * tn], preferred_element_type=F32)
        gate = jnp.dot(a, w_ref[:, d + cc * tn:d + (cc + 1) * tn], preferred_element_type=F32)
        blk = val * _sigmoid(gate)
        out_scr[cc] = blk
        ssq = ssq + jnp.sum(blk * blk, axis=-1, keepdims=True)
    inv = lax.rsqrt(ssq / d + EPS)
    for cc in range(n_c):
        sl = slice(cc * tn, (cc + 1) * tn)
        o_ref[:, sl] = x_ref[:, sl] + mod_ref[0, mi:mi + 1, sl] * (out_scr[cc] * inv * g_ref[gi:gi + 1, sl])


def _ssm_out(yg, xs, mod, g, w_glu, *, mi, gi, n_rows, seg_of_row):
    d = xs.shape[1]
    tm, tn = GLU_TOKEN_TILE, GLU_TILE
    n_c = d // tn
    n_pairs, _, width = yg.shape
    n_tiles = n_rows // tm

    def prev(i):
        return jnp.maximum(i - 1, 0)

    return pl.pallas_call(
        functools.partial(_ssm_out_kernel, mi=mi, gi=gi),
        out_shape=jax.ShapeDtypeStruct((n_rows, d), F32),
        grid=(n_tiles + 1,),
        in_specs=[
            pl.BlockSpec((n_pairs, tm // SSM_CHUNK, width), lambda i: (0, jnp.minimum(i, n_tiles - 1), 0)),
            pl.BlockSpec((tm, d), lambda i: (prev(i), 0)),
            pl.BlockSpec((1, MOD_ROWS, d), lambda i: (seg_of_row(prev(i) * tm), 0, 0)),
            pl.BlockSpec((SUBLANES, d),lambda i: (0, 0)),
            pl.BlockSpec(w_glu.shape, lambda i: (0, 0), pipeline_mode=pl.Buffered(1)),
        ],
        out_specs=pl.BlockSpec((tm, d), lambda i: (prev(i), 0)),
        scratch_shapes=[pltpu.VMEM((d // LANES, tm, LANES), F32), pltpu.VMEM((tm, d), BF16),
                        pltpu.VMEM((tm, d), BF16), pltpu.VMEM((n_c, tm, tn), F32)],
        compiler_params=_params("arbitrary"),
        name="ssm_glu_out",
    )(yg, xs, mod, g, w_glu)


def _rope_tables(seq_len, extra_rows):
    rows = seq_len // GRID_W
    axis_dim = HEAD_DIM // 2
    inv_freq = ROPE_BASE ** (-jnp.arange(0, axis_dim, 2, dtype=F32) / axis_dim)
    ang_r = jnp.arange(rows, dtype=F32)[:, None] * inv_freq
    ang_c = jnp.arange(GRID_W, dtype=F32)[:, None] * inv_freq
    by_row = lambda t: jnp.repeat(t, GRID_W, axis=0)
    by_col = lambda t: jnp.tile(t, (rows, 1))
    cos = jnp.concatenate([by_row(jnp.cos(ang_r))] * 2 + [by_col(jnp.cos(ang_c))] * 2, axis=-1)
    sin = jnp.concatenate([by_row(-jnp.sin(ang_r)), by_row(jnp.sin(ang_r)),
                           by_col(-jnp.sin(ang_c)), by_col(jnp.sin(ang_c))], axis=-1)
    cos = jnp.concatenate([cos, jnp.ones((extra_rows, HEAD_DIM), F32)], axis=0)
    sin = jnp.concatenate([sin, jnp.zeros((extra_rows, HEAD_DIM), F32)], axis=0)
    return cos, sin


def kernel(x, c, ctx, c_ctx, ada_w, ada_b, norm_g, ffn_w_in, ffn_w_out, attn_w_in, attn_w_out, attn_sink,
           ssm_w_in, ssm_a_re, ssm_a_im, ssm_log_dt, ssm_b_re, ssm_b_im, ssm_c_re, ssm_c_im, ssm_d, ssm_w_glu):
    batch, seq_len, d = x.shape
    c_len = ctx.shape[1]
    depth = ada_w.shape[0]
    tm = TOKEN_TILE
    n_lat_rows, n_ctx_rows = batch * seq_len, batch * c_len
    assert seq_len % tm == 0 and n_ctx_rows % tm == 0 and n_lat_rows % c_len == 0
    assert seq_len % ATT_BLOCK == 0 and c_len % ATT_BLOCK == 0 and seq_len % GRID_W == 0
    tiles_per_batch = seq_len // tm
    lat_tiles = n_lat_rows // tm
    all_tiles = lat_tiles + n_ctx_rows // tm

    def seg_of(i):
        return jnp.minimum(i // tiles_per_batch, batch)

    def pos_of(i):
        return jnp.where(i < lat_tiles, i % tiles_per_batch, tiles_per_batch + (i - lat_tiles))

    def seg_of_row(r):
        return jnp.minimum(r // seq_len, batch)

    xs, tail = x.reshape(n_lat_rows, d), ctx.reshape(n_ctx_rows, d)
    w_ff_in, w_ff_out = ffn_w_in.astype(BF16), ffn_w_out.astype(BF16)

    n_cond = batch + 1
    cvec = jnp.concatenate([c, c_ctx[None], jnp.zeros((SUBLANES - n_cond, d), F32)], axis=0)
    mods = _modulation(cvec.T, ada_w, ada_b, n_cond)
    mods = mods[:, :n_cond].reshape(depth, n_cond, N_MOD, d)
    mods = jnp.pad(mods, ((0, 0), (0, 0), (0, MOD_ROWS - N_MOD), (0, 0)))
    gains = jnp.pad(norm_g, ((0, 0), (0, SUBLANES - norm_g.shape[1]), (0, 0)))

    cos, sin = _rope_tables(seq_len, n_ctx_rows)

    for i in range(depth):
        need_ctx = i < depth - 1
        mod, g = mods[i], gains[i]
        j = i // 2
        xs = _ffn(xs, mod, g, w_ff_in, w_ff_out, (i, 0), mi=0, gi=0, n_tiles=all_tiles, seg_of=seg_of, tail=tail)
        tail = None
        post_tiles = all_tiles if need_ctx else lat_tiles
        if i % 2 == 0:
            q, k, v = _qkv(xs, mod, g, attn_w_in[j].astype(BF16), cos, sin,
                           mi=3, gi=2, n_tiles=all_tiles, seg_of=seg_of, pos_of=pos_of)
            o = _attention(q, k, v, attn_sink[j], batch=batch, seq_len=seq_len, c_len=c_len)
            xs = _proj_out(o, xs, mod, g, attn_w_out[j].astype(BF16),
                           mi=5, gi=3, n_tiles=post_tiles, seg_of=seg_of)
        else:
            if need_ctx:
                raise NotImplementedError("context output of the S5 mixer is only needed when another layer follows")
            ug = _proj_in(xs, mod, g, ssm_w_in[j].astype(BF16), mi=3, gi=2, n_tiles=all_tiles, seg_of=seg_of)
            L = SSM_CHUNK
            pair = GROUPS_PER_STEP * SSM_GROUP_CH
            toep, ws, wc, al = _s5_operators(ssm_a_re[j], ssm_a_im[j], ssm_log_dt[j], ssm_b_re[j], ssm_b_im[j],
                                             ssm_c_re[j], ssm_c_im[j])
            d_lanes = jnp.tile(ssm_d[j].reshape(d // pair, 1, pair), (1, 1, L))
            yg = _s5_core(ug, toep, ws, wc, al, d_lanes, batch=batch, n_lat=seq_len // L, n_ctx=c_len // L)
            xs = _ssm_out(yg, xs, mod, g, ssm_w_glu[j].astype(BF16),
                          mi=5, gi=3, n_rows=post_tiles * tm, seg_of_row=seg_of_row)
        xs = _ffn(xs, mod, g, w_ff_in, w_ff_out, (i, 1), mi=6, gi=4, n_tiles=post_tiles, seg_of=seg_of)
    return xs[:n_lat_rows].reshape(batch, seq_len, d)
```

```python
import functools
import math

import jax
import jax.numpy as jnp
from jax import lax
from jax.experimental import pallas as pl
from jax.experimental.pallas import tpu as pltpu

EPS = 1e-6
NEG_INF = -1e30
N_MOD = 9
GRID_W = 64
N_HEADS = 16
N_KV_HEADS = 4
HEAD_DIM = 128
GQA_GROUP = N_HEADS // N_KV_HEADS
ATT_BLOCK = 128
ROPE_BASE = 10000.0
ROPE_QUARTER = HEAD_DIM // 4
SSM_GROUP_CH = 16
SSM_STATE = 64
SSM_CHUNK = 16
GROUPS_PER_STEP = 2
SCAN_ROWS = 8

TOKEN_TILE = 512
FF_TILE = 512
GLU_TILE = 512
GLU_TOKEN_TILE = 256
MOD_TILE = 1024
MOD_ROWS = 16
VMEM_LIMIT = 56 * 1024 * 1024
LANES = 128
SUBLANES = 8

BF16 = jnp.bfloat16
F32 = jnp.float32


def _params(*sem):
    return pltpu.CompilerParams(dimension_semantics=sem, vmem_limit_bytes=VMEM_LIMIT)


def _rms(t, g):
    return t * lax.rsqrt(jnp.mean(t * t, axis=-1, keepdims=True) + EPS) * g


def _sigmoid(t):
    return 1.0 / (1.0 + jnp.exp(-t))


def _gelu_tanh(t):
    return 0.5 * t * (1.0 + jnp.tanh(math.sqrt(2.0 / math.pi) * (t + 0.044715 * (t * t * t))))


def _mod_kernel(ct_ref, w_ref, b_ref, o_ref, *, n_rows):
    w = w_ref[0]
    for r in range(n_rows):
        col = ct_ref[:, r:r + 1]
        col = col * _sigmoid(col)
        o_ref[0, r:r + 1, :] = jnp.sum(w * col, axis=0, keepdims=True) + b_ref[0]
    o_ref[0, n_rows:, :] = jnp.zeros((o_ref.shape[1] - n_rows, o_ref.shape[2]), F32)


def _modulation(cvec_t, ada_w, ada_b, n_rows):
    depth, d, nd = ada_w.shape
    tn = MOD_TILE
    return pl.pallas_call(
        functools.partial(_mod_kernel, n_rows=n_rows),
        out_shape=jax.ShapeDtypeStruct((depth, SUBLANES, nd), F32),
        grid=(depth, nd // tn),
        in_specs=[
            pl.BlockSpec((d, SUBLANES), lambda l, j: (0, 0)),
            pl.BlockSpec((1, d, tn), lambda l, j: (l, 0, j)),
            pl.BlockSpec((1, 1, tn), lambda l, j: (l, 0, j)),
        ],
        out_specs=pl.BlockSpec((1, SUBLANES, tn), lambda l, j: (l, 0, j)),
        compiler_params=_params("arbitrary", "arbitrary"),
        name="adaln_mod",
    )(cvec_t, ada_w, ada_b.reshape(depth, 1, nd))


def _ffn_w_cast_kernel(w_ref, o_ref):
    o_ref[...] = w_ref[...].astype(o_ref.dtype)


def _ffn_w_in_chunks(w):
    n_l, n_h, d, f2 = w.shape
    tf = FF_TILE
    nf = f2 // 2 // tf
    return pl.pallas_call(
        _ffn_w_cast_kernel,
        out_shape=jax.ShapeDtypeStruct((n_l, n_h, nf, 2, d, tf), BF16),
        grid=(n_l, n_h, nf, 2),
        in_specs=[pl.BlockSpec((None, None, d, tf), lambda l, h, f, gu: (l, h, 0, gu * nf + f))],
        out_specs=pl.BlockSpec((None, None, None, None, d, tf), lambda l, h, f, gu: (l, h, f, gu, 0, 0)),
        compiler_params=_params("parallel", "parallel", "parallel", "parallel"),
        name="ffn_w_cast",
    )(w)


def _ffn_kernel(x_ref, tail_ref, mod_ref, g_ref, wgu_ref, wo_ref, o_ref, h_scr, acc_scr,
                *, mi, gi, head_tiles, n_ff):
    def read_x():
        if head_tiles is None:
            return x_ref[...]
        return jnp.where(pl.program_id(0) < head_tiles, x_ref[...], tail_ref[...])

    def step(first, last):
        if first:
            xv = read_x()
            inv = lax.rsqrt(jnp.mean(xv * xv, axis=-1, keepdims=True) + EPS)
            h = (xv * inv) * (g_ref[gi:gi + 1, :] * (1.0 + mod_ref[0, mi + 1:mi + 2, :])) + mod_ref[0, mi:mi + 1, :]
            h = h.astype(BF16)
            if not last:
                h_scr[...] = h
        else:
            h = h_scr[...]
        gate = jnp.dot(h, wgu_ref[0], preferred_element_type=F32)
        up = jnp.dot(h, wgu_ref[1], preferred_element_type=F32)
        act = (gate * _sigmoid(gate) * up).astype(BF16)
        out = jnp.dot(act, wo_ref[...], preferred_element_type=F32)
        if not first:
            out = acc_scr[...] + out
        if last:
            inv = lax.rsqrt(jnp.mean(out * out, axis=-1, keepdims=True) + EPS)
            o_ref[...] = read_x() + (out * inv) * (0.5 * mod_ref[0, mi + 2:mi + 3, :] * g_ref[gi + 1:gi + 2, :])
        else:
            acc_scr[...] = out

    f = pl.program_id(1)
    if n_ff == 1:
        step(True, True)
    else:
        pl.when(f == 0)(functools.partial(step, True, False))
        if n_ff > 2:
            pl.when((f > 0) & (f < n_ff - 1))(functools.partial(step, False, False))
        pl.when(f == n_ff - 1)(functools.partial(step, False, True))


def _ffn(xs, mod, g, w_in, w_out, which, *, mi, gi, n_tiles, seg_of, tail=None):
    d = xs.shape[1]
    ff = w_out.shape[2]
    tm, tf = TOKEN_TILE, FF_TILE
    nf = ff // tf
    layer, half = which
    if tail is None:
        head_tiles = None
        tokens = [xs, xs]
        token_specs = [pl.BlockSpec((tm, d), lambda i, f: (i, 0)), pl.BlockSpec((SUBLANES, d),lambda i, f: (0, 0))]
    else:
        head_tiles = xs.shape[0] // tm
        tokens = [xs, tail]
        token_specs = [pl.BlockSpec((tm, d), lambda i, f: (jnp.minimum(i, head_tiles - 1), 0)),
                       pl.BlockSpec((tm, d), lambda i, f: (jnp.maximum(i - head_tiles, 0), 0),
                                    pipeline_mode=pl.Buffered(1))]
    return pl.pallas_call(
        functools.partial(_ffn_kernel, mi=mi, gi=gi, head_tiles=head_tiles, n_ff=nf),
        out_shape=jax.ShapeDtypeStruct((n_tiles * tm, d), F32),
        grid=(n_tiles, nf),
        in_specs=token_specs + [
            pl.BlockSpec((1, MOD_ROWS, d), lambda i, f: (seg_of(i), 0, 0)),
            pl.BlockSpec((SUBLANES, d),lambda i, f: (0, 0)),
            pl.BlockSpec((None, None, None, 2, d, tf), lambda i, f: (layer, half, f, 0, 0, 0)),
            pl.BlockSpec((None, None, tf, d), lambda i, f: (layer, half, f, 0)),
        ],
        out_specs=pl.BlockSpec((tm, d), lambda i, f: (i, 0)),
        scratch_shapes=[pltpu.VMEM((tm, d), BF16), pltpu.VMEM((tm, d), F32)],
        compiler_params=_params("parallel", "arbitrary"),
        name="ffn_halfstep",
    )(*tokens, mod, g, w_in, w_out)


def _rope(t, cos, sin, even_quarter):
    partner = jnp.where(even_quarter, pltpu.roll(t, HEAD_DIM - ROPE_QUARTER, 1), pltpu.roll(t, ROPE_QUARTER, 1))
    return t * cos + partner * sin


def _qkv_kernel(x_ref, mod_ref, g_ref, w_ref, cos_ref, sin_ref, q_ref, k_ref, v_ref, *, mi, gi):
    h = _rms(x_ref[...], g_ref[gi:gi + 1, :]) * (1.0 + mod_ref[0, mi + 1:mi + 2, :]) + mod_ref[0, mi:mi + 1, :]
    qkv = jnp.dot(h.astype(BF16), w_ref[...], preferred_element_type=F32)
    cos, sin = cos_ref[...], sin_ref[...]
    lane = lax.broadcasted_iota(jnp.int32, cos.shape, 1)
    even_quarter = (lane & ROPE_QUARTER) == 0
    scale = HEAD_DIM ** -0.5
    nq, nk = N_HEADS * HEAD_DIM, N_KV_HEADS * HEAD_DIM
    for hh in range(N_HEADS):
        sl = slice(hh * HEAD_DIM, (hh + 1) * HEAD_DIM)
        q_ref[:, sl] = (_rope(qkv[:, sl], cos, sin, even_quarter) * scale).astype(BF16)
    for hh in range(N_KV_HEADS):
        sl = slice(hh * HEAD_DIM, (hh + 1) * HEAD_DIM)
        src = slice(nq + hh * HEAD_DIM, nq + (hh + 1) * HEAD_DIM)
        k_ref[:, sl] = _rope(qkv[:, src], cos, sin, even_quarter).astype(BF16)
    v_ref[...] = qkv[:, nq + nk:].astype(BF16)


def _qkv(xs, mod, g, w, cos, sin, *, mi, gi, n_tiles, seg_of, pos_of):
    d = xs.shape[1]
    tm = TOKEN_TILE
    nq, nk = N_HEADS * HEAD_DIM, N_KV_HEADS * HEAD_DIM
    rows = n_tiles * tm
    return pl.pallas_call(
        functools.partial(_qkv_kernel, mi=mi, gi=gi),
        out_shape=(jax.ShapeDtypeStruct((rows, nq), BF16), jax.ShapeDtypeStruct((rows, nk), BF16),
                   jax.ShapeDtypeStruct((rows, nk), BF16)),
        grid=(n_tiles,),
        in_specs=[
            pl.BlockSpec((tm, d), lambda i: (i, 0)),
            pl.BlockSpec((1, MOD_ROWS, d), lambda i: (seg_of(i), 0, 0)),
            pl.BlockSpec((SUBLANES, d),lambda i: (0, 0)),
            pl.BlockSpec((d, nq + 2 * nk), lambda i: (0, 0), pipeline_mode=pl.Buffered(1)),
            pl.BlockSpec((tm, HEAD_DIM), lambda i: (pos_of(i), 0)),
            pl.BlockSpec((tm, HEAD_DIM), lambda i: (pos_of(i), 0)),
        ],
        out_specs=(pl.BlockSpec((tm, nq), lambda i: (i, 0)), pl.BlockSpec((tm, nk), lambda i: (i, 0)),
                   pl.BlockSpec((tm, nk), lambda i: (i, 0))),
        compiler_params=_params("parallel"),
        name="attn_qkv_rope",
    )(xs, mod, g, w, cos, sin)


def _proj_in_kernel(x_ref, mod_ref, g_ref, w_ref, o_ref, slab_scr, *, mi, gi):
    h = _rms(x_ref[...], g_ref[gi:gi + 1, :]) * (1.0 + mod_ref[0, mi + 1:mi + 2, :]) + mod_ref[0, mi:mi + 1, :]
    u = jnp.dot(h.astype(BF16), w_ref[...], preferred_element_type=F32)
    n_slab, tm, lanes = slab_scr.shape
    pair = GROUPS_PER_STEP * SSM_GROUP_CH
    per_slab = lanes // pair
    for k in range(n_slab):
        slab_scr[k] = u[:, k * lanes:(k + 1) * lanes]
    for t in range(SSM_CHUNK):
        for k in range(n_slab):
            rows = slab_scr[k, pl.ds(t, tm // SSM_CHUNK, stride=SSM_CHUNK), :]
            for jj in range(per_slab):
                o_ref[k * per_slab + jj, :, t * pair:(t + 1) * pair] = rows[:, jj * pair:(jj + 1) * pair].astype(o_ref.dtype)


def _proj_in(xs, mod, g, w, *, mi, gi, n_tiles, seg_of):
    d = xs.shape[1]
    tm = TOKEN_TILE
    pair = GROUPS_PER_STEP * SSM_GROUP_CH
    n_pairs = w.shape[1] // pair
    return pl.pallas_call(
        functools.partial(_proj_in_kernel, mi=mi, gi=gi),
        out_shape=jax.ShapeDtypeStruct((n_pairs, n_tiles * tm // SSM_CHUNK, SSM_CHUNK * pair), BF16),
        grid=(n_tiles,),
        in_specs=[
            pl.BlockSpec((tm, d), lambda i: (i, 0)),
            pl.BlockSpec((1, MOD_ROWS, d), lambda i: (seg_of(i), 0, 0)),
            pl.BlockSpec((SUBLANES, d),lambda i: (0, 0)),
            pl.BlockSpec(w.shape, lambda i: (0, 0), pipeline_mode=pl.Buffered(1)),
        ],
        out_specs=pl.BlockSpec((n_pairs, tm // SSM_CHUNK, SSM_CHUNK * pair), lambda i: (0, i, 0)),
        scratch_shapes=[pltpu.VMEM((w.shape[1] // LANES, tm, LANES), F32)],
        compiler_params=_params("parallel"),
        name="ssm_in_proj",
    )(xs, mod, g, w)


def _proj_out_kernel(a_ref, x_ref, mod_ref, g_ref, w_ref, o_ref, *, mi, gi):
    out = jnp.dot(a_ref[...], w_ref[...], preferred_element_type=F32)
    o_ref[...] = x_ref[...] + mod_ref[0, mi:mi + 1, :] * _rms(out, g_ref[gi:gi + 1, :])


def _proj_out(a, xs, mod, g, w, *, mi, gi, n_tiles, seg_of):
    d = xs.shape[1]
    tm = TOKEN_TILE
    return pl.pallas_call(
        functools.partial(_proj_out_kernel, mi=mi, gi=gi),
        out_shape=jax.ShapeDtypeStruct((n_tiles * tm, d), F32),
        grid=(n_tiles,),
        in_specs=[
            pl.BlockSpec((tm, a.shape[1]), lambda i: (i, 0)),
            pl.BlockSpec((tm, d), lambda i: (i, 0)),
            pl.BlockSpec((1, MOD_ROWS, d), lambda i: (seg_of(i), 0, 0)),
            pl.BlockSpec((SUBLANES, d),lambda i: (0, 0)),
            pl.BlockSpec(w.shape, lambda i: (0, 0), pipeline_mode=pl.Buffered(1)),
        ],
        out_specs=pl.BlockSpec((tm, d), lambda i: (i, 0)),
        compiler_params=_params("parallel"),
        name="attn_out_proj",
    )(a, xs, mod, g, w)


def _attn_kernel(sink_ref, q_ref, kp_ref, ko_ref, kn_ref, vp_ref, vo_ref, vn_ref, kc_ref, vc_ref, bias_ref, o_ref):
    blk = ATT_BLOCK
    bias = jnp.concatenate([bias_ref[...]] * GQA_GROUP, axis=0)
    row = lax.broadcasted_iota(jnp.int32, (GQA_GROUP * blk, 1), 0)
    for h in range(N_KV_HEADS):
        hs = slice(h * HEAD_DIM, (h + 1) * HEAD_DIM)
        qs = jnp.concatenate(
            [q_ref[:, (h * GQA_GROUP + gq) * HEAD_DIM:(h * GQA_GROUP + gq + 1) * HEAD_DIM] for gq in range(GQA_GROUP)],
            axis=0)
        sink = jnp.full((GQA_GROUP * blk, 1), sink_ref[h * GQA_GROUP], F32)
        for gq in range(1, GQA_GROUP):
            sink = jnp.where(row >= gq * blk, sink_ref[h * GQA_GROUP + gq], sink)
        scores, col = [], 0
        for k_ref in (kp_ref, ko_ref, kn_ref, kc_ref):
            s = lax.dot_general(qs, k_ref[:, hs], (((1,), (1,)), ((), ())), preferred_element_type=F32)
            if k_ref is not kc_ref:
                s = s + bias[:, col:col + k_ref.shape[0]]
            col += k_ref.shape[0]
            scores.append(s)

        def lane_slabs(t):
            return [t[:, c:c + LANES] for c in range(0, t.shape[1], LANES)]

        m = functools.reduce(jnp.maximum, [sl for s in scores for sl in lane_slabs(s)])
        m = jnp.maximum(jnp.max(m, axis=-1, keepdims=True), sink)
        probs = [jnp.exp(s - m) for s in scores]
        denom = jnp.sum(functools.reduce(jnp.add, [sl for p in probs for sl in lane_slabs(p)]), axis=-1, keepdims=True)
        denom = denom + jnp.exp(sink - m)
        o = jnp.zeros((GQA_GROUP * blk, HEAD_DIM), F32)
        for p, v_ref in zip(probs, (vp_ref, vo_ref, vn_ref, vc_ref)):
            o = o + jnp.dot(p.astype(BF16), v_ref[:, hs], preferred_element_type=F32)
        o = o / denom
        for gq in range(GQA_GROUP):
            col = (h * GQA_GROUP + gq) * HEAD_DIM
            o_ref[:, col:col + HEAD_DIM] = o[gq * blk:(gq + 1) * blk, :].astype(o_ref.dtype)


def _attention(q, k, v, sink, *, batch, seq_len, c_len):
    blk = ATT_BLOCK
    n_blk, c_blk = seq_len // blk, c_len // blk
    nq, nk = q.shape[1], k.shape[1]
    lat_blocks = batch * n_blk
    ctx_block0 = batch * seq_len // c_len

    def q_idx(b, n):
        return jnp.where(n < n_blk, b * n_blk + n, lat_blocks + b * c_blk + (n - n_blk))

    def band_idx(off):
        def idx(b, n, s):
            nn = jnp.clip(n + off, 0, n_blk - 1)
            return (b * n_blk + nn, 0)
        return idx

    assert n_blk >= 2
    n_keys = 3 * blk + c_len
    qi = jnp.arange(blk)[:, None]
    kj = jnp.arange(n_keys)[None, :]
    in_band = (kj - qi >= 0) & (kj - qi <= 2 * blk)
    cases = [in_band & (kj >= blk), in_band, in_band & (kj < 2 * blk), jnp.zeros_like(in_band)]
    bias = jnp.stack([jnp.where((kj < 3 * blk) & ~case, NEG_INF, 0.0).astype(F32) for case in cases])

    def bias_idx(b, n, s):
        return (jnp.where(n == 0, 0, jnp.where(n < n_blk - 1, 1, jnp.where(n == n_blk - 1, 2, 3))), 0, 0)

    kv_spec = [pl.BlockSpec((blk, nk), band_idx(off)) for off in (-1, 0, 1)]
    ctx_spec = pl.BlockSpec((c_len, nk), lambda b, n, s: (ctx_block0 + b, 0))
    grid_spec = pltpu.PrefetchScalarGridSpec(
        num_scalar_prefetch=1,
        grid=(batch, n_blk + c_blk),
        in_specs=[pl.BlockSpec((blk, nq), lambda b, n, s: (q_idx(b, n), 0))] + kv_spec + kv_spec + [ctx_spec, ctx_spec]
        + [pl.BlockSpec((None, blk, n_keys), bias_idx)],
        out_specs=pl.BlockSpec((blk, nq), lambda b, n, s: (q_idx(b, n), 0)),
    )
    return pl.pallas_call(
        _attn_kernel,
        out_shape=jax.ShapeDtypeStruct(q.shape, BF16),
        grid_spec=grid_spec,
        compiler_params=_params("parallel", "arbitrary"),
        name="window_gqa",
    )(sink, q, k, k, k, v, v, v, k, v, bias)


def _s5_kernel(u_ref, t_ref, ws_ref, wc_ref, al_ref, d_ref, y_ref, s_scr, h_scr, *, batch, n_lat, n_ctx):
    half = GROUPS_PER_STEP * SSM_STATE
    u = u_ref[0]
    s_scr[...] = jnp.dot(u, ws_ref[0], preferred_element_type=F32)
    sub = SCAN_ROWS
    ctx0 = batch * n_lat
    zero = jnp.zeros((sub, half), F32)
    sub_row = lax.broadcasted_iota(jnp.int32, (sub, half), 0)

    def rep(v):
        return jnp.broadcast_to(v, (sub, half))

    def tables(row, fwd):
        pw = _powers(al_ref[0, row:row + 1, :], al_ref[0, row + 1:row + 2, :], sub)
        levels = []
        for k in (1 << b for b in range(sub.bit_length() - 1)):
            ok = (sub_row >= k) if fwd else (sub_row <= sub - 1 - k)
            levels.append((k if fwd else sub - k, jnp.where(ok, rep(pw[k][0]), 0.0), jnp.where(ok, rep(pw[k][1]), 0.0)))
        c_re, c_im = zero, zero
        for r in range(sub):
            e = r if fwd else sub - 1 - r
            c_re = jnp.where(sub_row == r, rep(pw[e][0]), c_re)
            c_im = jnp.where(sub_row == r, rep(pw[e][1]), c_im)
        inner = (sub_row >= 1) if fwd else (sub_row <= sub - 2)
        return dict(levels=levels, carry_w=(c_re, c_im), block_w=(rep(pw[sub][0]), rep(pw[sub][1])),
                    shift=1 if fwd else sub - 1, inner=inner, last=sub - 1 if fwd else 0)

    tbl_f, tbl_b = tables(0, True), tables(2, False)

    def scan_block(h, tbl, base, col, keep):
        h_re, h_im = h
        base = pl.multiple_of(base, sub)
        x_re = s_scr[pl.ds(base, sub), col:col + half]
        x_im = s_scr[pl.ds(base, sub), col + half:col + 2 * half]
        for shift, c_re, c_im in tbl["levels"]:
            r_re, r_im = pltpu.roll(x_re, shift, 0), pltpu.roll(x_im, shift, 0)
            x_re, x_im = x_re + c_re * r_re - c_im * r_im, x_im + c_re * r_im + c_im * r_re
        if keep:
            w_re, w_im = tbl["carry_w"]
            p_re = jnp.where(tbl["inner"], pltpu.roll(x_re, tbl["shift"], 0), 0.0)
            p_im = jnp.where(tbl["inner"], pltpu.roll(x_im, tbl["shift"], 0), 0.0)
            h_scr[pl.ds(base, sub), col:col + half] = w_re * h_re - w_im * h_im + p_re
            h_scr[pl.ds(base, sub), col + half:col + 2 * half] = w_re * h_im + w_im * h_re + p_im
        b_re, b_im = tbl["block_w"]
        last = tbl["last"]
        t_re, t_im = rep(x_re[last:last + 1, :]), rep(x_im[last:last + 1, :])
        return b_re * h_re - b_im * h_im + t_re, b_re * h_im + b_im * h_re + t_im

    def make_step(first_row, n_rows, keep):
        def step(m, carry):
            out = []
            for b in range(batch):
                hf, hb = carry[2 * b], carry[2 * b + 1]
                lo = first_row + b * n_rows
                hf = scan_block(hf, tbl_f, lo + m * sub, 0, keep)
                hb = scan_block(hb, tbl_b, lo + n_rows - sub - m * sub, 2 * half, keep)
                out += [hf, hb]
            return tuple(out)
        return step

    carry = lax.fori_loop(0, n_ctx // sub, make_step(ctx0, n_ctx, False), ((zero, zero),) * (2 * batch))
    lax.fori_loop(0, n_lat // sub, make_step(0, n_lat, True), carry)

    rows = batch * n_lat
    u_lat = u[:rows]
    y_ref[0] = (jnp.dot(u_lat, t_ref[0], preferred_element_type=F32)
                + jnp.dot(h_scr[...].astype(BF16), wc_ref[0], preferred_element_type=F32)
                + d_ref[0] * u_lat.astype(F32))


def _s5_core(ug, toep, ws, wc, al, d_lanes, *, batch, n_lat, n_ctx):
    n_steps, chunks, width = ug.shape
    assert n_lat % SCAN_ROWS == 0 and n_ctx % SCAN_ROWS == 0 and SCAN_ROWS & (SCAN_ROWS - 1) == 0
    rows = batch * n_lat
    return pl.pallas_call(
        functools.partial(_s5_kernel, batch=batch, n_lat=n_lat, n_ctx=n_ctx),
        out_shape=jax.ShapeDtypeStruct((n_steps, rows, width), F32),
        grid=(n_steps,),
        in_specs=[
            pl.BlockSpec((1, chunks, width), lambda j: (j, 0, 0)),
            pl.BlockSpec((1,) + toep.shape[1:], lambda j: (j, 0, 0)),
            pl.BlockSpec((1,) + ws.shape[1:], lambda j: (j, 0, 0)),
            pl.BlockSpec((1,) + wc.shape[1:], lambda j: (j, 0, 0)),
            pl.BlockSpec((1,) + al.shape[1:], lambda j: (j, 0, 0)),
            pl.BlockSpec((1, 1, width), lambda j: (j, 0, 0)),
        ],
        out_specs=pl.BlockSpec((1, rows, width), lambda j: (j, 0, 0)),
        scratch_shapes=[pltpu.VMEM((chunks, 4 * GROUPS_PER_STEP * SSM_STATE), F32),
                        pltpu.VMEM((rows, 4 * GROUPS_PER_STEP * SSM_STATE), F32)],
        compiler_params=_params("parallel"),
        name="s5_chunked",
    )(ug, toep, ws, wc, al, d_lanes)


def _cmul(a_re, a_im, b_re, b_im):
    return a_re * b_re - a_im * b_im, a_re * b_im + a_im * b_re


def _zoh(a_re, a_im, log_dt):
    dt = jnp.exp(log_dt)
    mag = jnp.exp(a_re * dt)
    ab_re, ab_im = mag * jnp.cos(a_im * dt), mag * jnp.sin(a_im * dt)
    den = a_re * a_re + a_im * a_im
    x_re, x_im = ab_re - 1.0, ab_im
    return ab_re, ab_im, (x_re * a_re + x_im * a_im) / den, (x_im * a_re - x_re * a_im) / den


def _powers(ab_re, ab_im, n):
    out = [(jnp.ones_like(ab_re), jnp.zeros_like(ab_im))]
    for _ in range(n):
        out.append(_cmul(out[-1][0], out[-1][1], ab_re, ab_im))
    return out


def _s5_op_kernel(row_ref, col_ref, bt_ref, ct_ref, t_ref, ws_ref, wc_ref, al_ref):
    L = SSM_CHUNK
    pair = ct_ref.shape[-1] // L
    assert L & (L - 1) == 0 and pair & (pair - 1) == 0
    hi = lax.Precision.HIGHEST
    lane_block = lax.broadcasted_iota(jnp.int32, ct_ref.shape[-2:], 1) >> (pair.bit_length() - 1)
    inject, taps, read = [], [], []
    for d in range(2):
        ab_re, ab_im, f_re, f_im = _zoh(*(row_ref[0, 3 * d + r:3 * d + r + 1, :] for r in range(3)))
        pw_row = _powers(ab_re, ab_im, L)
        bb_re, bb_im = _cmul(f_re, f_im, bt_ref[0, d, 0], bt_ref[0, d, 1])
        order = range(L - 1, -1, -1) if d == 0 else range(L)
        inject.append([_cmul(pw_row[e][0], pw_row[e][1], bb_re, bb_im) for e in order])
        sq = [_zoh(*(col_ref[0, :, 3 * d + r:3 * d + r + 1] for r in range(3)))[:2]]
        while len(sq) < L.bit_length() - 1:
            sq.append(_cmul(*sq[-1], *sq[-1]))
        expo = lane_block if d == 0 else (L - 1) - lane_block
        low = (expo & 1) == 1
        p_re, p_im = jnp.where(low, sq[0][0], 1.0), jnp.where(low, sq[0][1], 0.0)
        for b in range(1, len(sq)):
            n_re, n_im = _cmul(p_re, p_im, *sq[b])
            bit = ((expo >> b) & 1) == 1
            p_re, p_im = jnp.where(bit, n_re, p_re), jnp.where(bit, n_im, p_im)
        ca_re, ca_im = _cmul(ct_ref[0, d, 0], ct_ref[0, d, 1], p_re, p_im)
        taps.append(jnp.dot(bb_re, ca_re, precision=hi, preferred_element_type=F32)
                    - jnp.dot(bb_im, ca_im, precision=hi, preferred_element_type=F32))
        rd_re, rd_im = _cmul(ca_re, ca_im, *sq[0])
        read.append((rd_re, -rd_im))
        al_ref[0, 2 * d:2 * d + 1, :] = pw_row[L][0]
        al_ref[0, 2 * d + 1:2 * d + 2, :] = pw_row[L][1]
    al_ref[0, 4:, :] = jnp.zeros((al_ref.shape[1] - 4, al_ref.shape[2]), F32)

    for s in range(L):
        ws_ref[0, s * pair:(s + 1) * pair, :] = jnp.concatenate(
            [inject[0][s][0], inject[0][s][1], inject[1][s][0], inject[1][s][1]], axis=1).astype(ws_ref.dtype)
    rows = read[0][0].shape[0]
    for q, blk in enumerate((read[0][0], read[0][1], read[1][0], read[1][1])):
        wc_ref[0, q * rows:(q + 1) * rows, :] = blk.astype(wc_ref.dtype)
    k_f, k_b = taps
    keep = (L - 1) * pair
    by_lag = jnp.concatenate([k_b[:, :keep], k_b[:, keep:] + k_f[:, :pair], k_f[:, pair:]], axis=1)
    for s in range(L):
        t_ref[0, s * pair:(s + 1) * pair, :] = by_lag[:, (L - 1 - s) * pair:(2 * L - 1 - s) * pair].astype(t_ref.dtype)


def _s5_operators(a_re, a_im, log_dt, b_re, b_im, c_re, c_im):
    n_g, n_p = a_re.shape[1], a_re.shape[2]
    gc = b_re.shape[-1]
    n_j, per = n_g // GROUPS_PER_STEP, GROUPS_PER_STEP
    states, pair, width = per * n_p, per * gc, SSM_CHUNK * per * gc
    lam = jnp.stack([a_re, a_im, jnp.broadcast_to(log_dt[..., None], a_re.shape)], axis=1)
    lam = jnp.transpose(lam.reshape(6, n_j, states), (1, 0, 2))
    lam = jnp.pad(lam, ((0, 0), (0, SUBLANES - lam.shape[1]), (0, 0)))
    eye = jnp.eye(per, dtype=F32)
    bt = jnp.stack([b_re, b_im], axis=1).reshape(2, 2, n_j, per, n_p, gc)
    bt = jnp.einsum('drjgpc,gh->jdrgchp', bt, eye).reshape(n_j, 2, 2, pair, states)
    ct = jnp.stack([c_re, c_im], axis=1).reshape(2, 2, n_j, per, gc, n_p)
    ct = jnp.einsum('drjgcp,gh->jdrgphc', ct, eye).reshape(n_j, 2, 2, states, pair)
    ct = jnp.tile(ct, (1, 1, 1, 1, SSM_CHUNK))
    mat = jax.ShapeDtypeStruct((n_j, width, width), BF16)
    return pl.pallas_call(
        _s5_op_kernel,
        out_shape=(mat, mat, mat, jax.ShapeDtypeStruct((n_j, SUBLANES, states), F32)),
        grid=(n_j,),
        in_specs=[
            pl.BlockSpec((1, SUBLANES, states), lambda j: (j, 0, 0)),
            pl.BlockSpec((1, states, SUBLANES), lambda j: (j, 0, 0)),
            pl.BlockSpec((1, 2, 2, pair, states), lambda j: (j, 0, 0, 0, 0)),
            pl.BlockSpec((1, 2, 2, states, width), lambda j: (j, 0, 0, 0, 0)),
        ],
        out_specs=(pl.BlockSpec((1, width, width), lambda j: (j, 0, 0)),) * 3
        + (pl.BlockSpec((1, SUBLANES, states), lambda j: (j, 0, 0)),),
        compiler_params=_params("parallel"),
        name="s5_operators",
    )(lam, jnp.transpose(lam, (0, 2, 1)), bt, ct)


def _ssm_out_kernel(y_ref, x_ref, mod_ref, g_ref, w_ref, o_ref, slab_scr, act_even, act_odd, out_scr, *, mi, gi):
    i = pl.program_id(0)

    @pl.when(i == 0)
    def _():
        act_odd[...] = jnp.zeros_like(act_odd)

    @pl.when(i % 2 == 0)
    def _():
        _ssm_out_step(y_ref, x_ref, mod_ref, g_ref, w_ref, o_ref, slab_scr, act_even, act_odd, out_scr, mi=mi, gi=gi)

    @pl.when(i % 2 == 1)
    def _():
        _ssm_out_step(y_ref, x_ref, mod_ref, g_ref, w_ref, o_ref, slab_scr, act_odd, act_even, out_scr, mi=mi, gi=gi)


def _ssm_out_step(y_ref, x_ref, mod_ref, g_ref, w_ref, o_ref, slab_scr, act_new, act_old, out_scr, *, mi, gi):
    n_slab, tm, lanes = slab_scr.shape
    pair = GROUPS_PER_STEP * SSM_GROUP_CH
    per_slab = lanes // pair
    for t in range(SSM_CHUNK):
        for k in range(n_slab):
            slab_scr[k, pl.ds(t, tm // SSM_CHUNK, stride=SSM_CHUNK), :] = jnp.concatenate(
                [y_ref[k * per_slab + jj, :, t * pair:(t + 1) * pair] for jj in range(per_slab)], axis=-1)
    for k in range(n_slab):
        act_new[:, k * lanes:(k + 1) * lanes] = _gelu_tanh(slab_scr[k]).astype(BF16)

    n_c, _, tn = out_scr.shape
    d = n_c * tn
    a = act_old[...]
    ssq = jnp.zeros((tm, 1), F32)
    for cc in range(n_c):
        val = jnp.dot(a, w_ref[:, cc * tn:(cc + 1) * tn], preferred_element_type=F32)
        gate = jnp.dot(a, w_ref[:, d + cc * tn:d + (cc + 1) * tn], preferred_element_type=F32)
        blk = val * _sigmoid(gate)
        out_scr[cc] = blk
        ssq = ssq + jnp.sum(blk * blk, axis=-1, keepdims=True)
    inv = lax.rsqrt(ssq / d + EPS)
    for cc in range(n_c):
        sl = slice(cc * tn, (cc + 1) * tn)
        o_ref[:, sl] = x_ref[:, sl] + mod_ref[0, mi:mi + 1, sl] * (out_scr[cc] * inv * g_ref[gi:gi + 1, sl])


def _ssm_out(yg, xs, mod, g, w_glu, *, mi, gi, n_rows, seg_of_row):
    d = xs.shape[1]
    tm, tn = GLU_TOKEN_TILE, GLU_TILE
    n_c = d // tn
    n_pairs, _, width = yg.shape
    n_tiles = n_rows // tm

    def prev(i):
        return jnp.maximum(i - 1, 0)

    return pl.pallas_call(
        functools.partial(_ssm_out_kernel, mi=mi, gi=gi),
        out_shape=jax.ShapeDtypeStruct((n_rows, d), F32),
        grid=(n_tiles + 1,),
        in_specs=[
            pl.BlockSpec((n_pairs, tm // SSM_CHUNK, width), lambda i: (0, jnp.minimum(i, n_tiles - 1), 0)),
            pl.BlockSpec((tm, d), lambda i: (prev(i), 0)),
            pl.BlockSpec((1, MOD_ROWS, d), lambda i: (seg_of_row(prev(i) * tm), 0, 0)),
            pl.BlockSpec((SUBLANES, d),lambda i: (0, 0)),
            pl.BlockSpec(w_glu.shape, lambda i: (0, 0), pipeline_mode=pl.Buffered(1)),
        ],
        out_specs=pl.BlockSpec((tm, d), lambda i: (prev(i), 0)),
        scratch_shapes=[pltpu.VMEM((d // LANES, tm, LANES), F32), pltpu.VMEM((tm, d), BF16),
                        pltpu.VMEM((tm, d), BF16), pltpu.VMEM((n_c, tm, tn), F32)],
        compiler_params=_params("arbitrary"),
        name="ssm_glu_out",
    )(yg, xs, mod, g, w_glu)


def _rope_tables(seq_len, extra_rows):
    rows = seq_len // GRID_W
    axis_dim = HEAD_DIM // 2
    inv_freq = ROPE_BASE ** (-jnp.arange(0, axis_dim, 2, dtype=F32) / axis_dim)
    ang_r = jnp.arange(rows, dtype=F32)[:, None] * inv_freq
    ang_c = jnp.arange(GRID_W, dtype=F32)[:, None] * inv_freq
    by_row = lambda t: jnp.repeat(t, GRID_W, axis=0)
    by_col = lambda t: jnp.tile(t, (rows, 1))
    cos = jnp.concatenate([by_row(jnp.cos(ang_r))] * 2 + [by_col(jnp.cos(ang_c))] * 2, axis=-1)
    sin = jnp.concatenate([by_row(-jnp.sin(ang_r)), by_row(jnp.sin(ang_r)),
                           by_col(-jnp.sin(ang_c)), by_col(jnp.sin(ang_c))], axis=-1)
    cos = jnp.concatenate([cos, jnp.ones((extra_rows, HEAD_DIM), F32)], axis=0)
    sin = jnp.concatenate([sin, jnp.zeros((extra_rows, HEAD_DIM), F32)], axis=0)
    return cos, sin


def kernel(x, c, ctx, c_ctx, ada_w, ada_b, norm_g, ffn_w_in, ffn_w_out, attn_w_in, attn_w_out, attn_sink,
           ssm_w_in, ssm_a_re, ssm_a_im, ssm_log_dt, ssm_b_re, ssm_b_im, ssm_c_re, ssm_c_im, ssm_d, ssm_w_glu):
    batch, seq_len, d = x.shape
    c_len = ctx.shape[1]
    depth = ada_w.shape[0]
    tm = TOKEN_TILE
    n_lat_rows, n_ctx_rows = batch * seq_len, batch * c_len
    assert seq_len % tm == 0 and n_ctx_rows % tm == 0 and n_lat_rows % c_len == 0
    assert seq_len % ATT_BLOCK == 0 and c_len % ATT_BLOCK == 0 and seq_len % GRID_W == 0
    tiles_per_batch = seq_len // tm
    lat_tiles = n_lat_rows // tm
    all_tiles = lat_tiles + n_ctx_rows // tm

    def seg_of(i):
        return jnp.minimum(i // tiles_per_batch, batch)

    def pos_of(i):
        return jnp.where(i < lat_tiles, i % tiles_per_batch, tiles_per_batch + (i - lat_tiles))

    def seg_of_row(r):
        return jnp.minimum(r // seq_len, batch)

    xs, tail = x.reshape(n_lat_rows, d), ctx.reshape(n_ctx_rows, d)
    w_ff_in, w_ff_out = _ffn_w_in_chunks(ffn_w_in), ffn_w_out.astype(BF16)

    n_cond = batch + 1
    cvec = jnp.concatenate([c, c_ctx[None], jnp.zeros((SUBLANES - n_cond, d), F32)], axis=0)
    mods = _modulation(cvec.T, ada_w, ada_b, n_cond)
    mods = mods[:, :n_cond].reshape(depth, n_cond, N_MOD, d)
    mods = jnp.pad(mods, ((0, 0), (0, 0), (0, MOD_ROWS - N_MOD), (0, 0)))
    gains = jnp.pad(norm_g, ((0, 0), (0, SUBLANES - norm_g.shape[1]), (0, 0)))

    cos, sin = _rope_tables(seq_len, n_ctx_rows)

    for i in range(depth):
        need_ctx = i < depth - 1
        mod, g = mods[i], gains[i]
        j = i // 2
        xs = _ffn(xs, mod, g, w_ff_in, w_ff_out, (i, 0), mi=0, gi=0, n_tiles=all_tiles, seg_of=seg_of, tail=tail)
        tail = None
        post_tiles = all_tiles if need_ctx else lat_tiles
        if i % 2 == 0:
            q, k, v = _qkv(xs, mod, g, attn_w_in[j].astype(BF16), cos, sin,
                           mi=3, gi=2, n_tiles=all_tiles, seg_of=seg_of, pos_of=pos_of)
            o = _attention(q, k, v, attn_sink[j], batch=batch, seq_len=seq_len, c_len=c_len)
            xs = _proj_out(o, xs, mod, g, attn_w_out[j].astype(BF16),
                           mi=5, gi=3, n_tiles=post_tiles, seg_of=seg_of)
        else:
            if need_ctx:
                raise NotImplementedError("context output of the S5 mixer is only needed when another layer follows")
            ug = _proj_in(xs, mod, g, ssm_w_in[j].astype(BF16), mi=3, gi=2, n_tiles=all_tiles, seg_of=seg_of)
            L = SSM_CHUNK
            pair = GROUPS_PER_STEP * SSM_GROUP_CH
            toep, ws, wc, al = _s5_operators(ssm_a_re[j], ssm_a_im[j], ssm_log_dt[j], ssm_b_re[j], ssm_b_im[j],
                                             ssm_c_re[j], ssm_c_im[j])
            d_lanes = jnp.tile(ssm_d[j].reshape(d // pair, 1, pair), (1, 1, L))
            yg = _s5_core(ug, toep, ws, wc, al, d_lanes, batch=batch, n_lat=seq_len // L, n_ctx=c_len // L)
            xs = _ssm_out(yg, xs, mod, g, ssm_w_glu[j].astype(BF16),
                          mi=5, gi=3, n_rows=post_tiles * tm, seg_of_row=seg_of_row)
        xs = _ffn(xs, mod, g, w_ff_in, w_ff_out, (i, 1), mi=6, gi=4, n_tiles=post_tiles, seg_of=seg_of)
    return xs[:n_lat_rows].reshape(batch, seq_len, d)
```

```python
import functools
import math

import jax
import jax.numpy as jnp
from jax import lax
from jax.experimental import pallas as pl
from jax.experimental.pallas import tpu as pltpu

EPS = 1e-6
NEG_INF = -1e30
N_MOD = 9
GRID_W = 64
N_HEADS = 16
N_KV_HEADS = 4
HEAD_DIM = 128
GQA_GROUP = N_HEADS // N_KV_HEADS
ATT_BLOCK = 128
ROPE_BASE = 10000.0
ROPE_QUARTER = HEAD_DIM // 4
SSM_GROUP_CH = 16
SSM_STATE = 64
SSM_CHUNK = 16
GROUPS_PER_STEP = 2
SCAN_ROWS = 8

TOKEN_TILE = 512
FF_TILE = 512
GLU_TILE = 512
GLU_TOKEN_TILE = 256
MOD_TILE = 1024
MOD_ROWS = 16
VMEM_LIMIT = 56 * 1024 * 1024
LANES = 128
SUBLANES = 8

BF16 = jnp.bfloat16
F32 = jnp.float32


def _params(*sem):
    return pltpu.CompilerParams(dimension_semantics=sem, vmem_limit_bytes=VMEM_LIMIT)


def _rms(t, g):
    return t * lax.rsqrt(jnp.mean(t * t, axis=-1, keepdims=True) + EPS) * g


def _sigmoid(t):
    return 1.0 / (1.0 + jnp.exp(-t))


def _gelu_tanh(t):
    return 0.5 * t * (1.0 + jnp.tanh(math.sqrt(2.0 / math.pi) * (t + 0.044715 * (t * t * t))))


def _mod_kernel(ct_ref, w_ref, b_ref, o_ref, *, n_rows):
    w = w_ref[0]
    for r in range(n_rows):
        col = ct_ref[:, r:r + 1]
        col = col * _sigmoid(col)
        o_ref[0, r:r + 1, :] = jnp.sum(w * col, axis=0, keepdims=True) + b_ref[0]
    o_ref[0, n_rows:, :] = jnp.zeros((o_ref.shape[1] - n_rows, o_ref.shape[2]), F32)


def _modulation(cvec_t, ada_w, ada_b, n_rows):
    depth, d, nd = ada_w.shape
    tn = MOD_TILE
    return pl.pallas_call(
        functools.partial(_mod_kernel, n_rows=n_rows),
        out_shape=jax.ShapeDtypeStruct((depth, SUBLANES, nd), F32),
        grid=(depth, nd // tn),
        in_specs=[
            pl.BlockSpec((d, SUBLANES), lambda l, j: (0, 0)),
            pl.BlockSpec((1, d, tn), lambda l, j: (l, 0, j)),
            pl.BlockSpec((1, 1, tn), lambda l, j: (l, 0, j)),
        ],
        out_specs=pl.BlockSpec((1, SUBLANES, tn), lambda l, j: (l, 0, j)),
        compiler_params=_params("arbitrary", "arbitrary"),
        name="adaln_mod",
    )(cvec_t, ada_w, ada_b.reshape(depth, 1, nd))


def _ffn_w_cast_kernel(w_ref, o_ref):
    o_ref[...] = w_ref[...].astype(o_ref.dtype)


def _ffn_w_in_chunks(w):
    n_l, n_h, d, f2 = w.shape
    tf = FF_TILE
    nf = f2 // 2 // tf
    return pl.pallas_call(
        _ffn_w_cast_kernel,
        out_shape=jax.ShapeDtypeStruct((n_l, n_h, nf, 2, d, tf), BF16),
        grid=(n_l, n_h, nf, 2),
        in_specs=[pl.BlockSpec((None, None, d, tf), lambda l, h, f, gu: (l, h, 0, gu * nf + f))],
        out_specs=pl.BlockSpec((None, None, None, None, d, tf), lambda l, h, f, gu: (l, h, f, gu, 0, 0)),
        compiler_params=_params("parallel", "parallel", "parallel", "parallel"),
        name="ffn_w_cast",
    )(w)


def _ffn_kernel(x_ref, tail_ref, mod_ref, g_ref, wgu_ref, wo_ref, o_ref, h_scr, acc_scr,
                *, mi, gi, head_tiles, n_ff):
    def read_x():
        if head_tiles is None:
            return x_ref[...]
        return jnp.where(pl.program_id(0) < head_tiles, x_ref[...], tail_ref[...])

    def step(first, last):
        if first:
            xv = read_x()
            inv = lax.rsqrt(jnp.mean(xv * xv, axis=-1, keepdims=True) + EPS)
            h = (xv * inv) * (g_ref[gi:gi + 1, :] * (1.0 + mod_ref[0, mi + 1:mi + 2, :])) + mod_ref[0, mi:mi + 1, :]
            h = h.astype(BF16)
            if not last:
                h_scr[...] = h
        else:
            h = h_scr[...]
        gate = jnp.dot(h, wgu_ref[0], preferred_element_type=F32)
        up = jnp.dot(h, wgu_ref[1], preferred_element_type=F32)
        act = (gate * _sigmoid(gate) * up).astype(BF16)
        out = jnp.dot(act, wo_ref[...], preferred_element_type=F32)
        if not first:
            out = acc_scr[...] + out
        if last:
            inv = lax.rsqrt(jnp.mean(out * out, axis=-1, keepdims=True) + EPS)
            o_ref[...] = read_x() + (out * inv) * (0.5 * mod_ref[0, mi + 2:mi + 3, :] * g_ref[gi + 1:gi + 2, :])
        else:
            acc_scr[...] = out

    f = pl.program_id(1)
    if n_ff == 1:
        step(True, True)
    else:
        pl.when(f == 0)(functools.partial(step, True, False))
        if n_ff > 2:
            pl.when((f > 0) & (f < n_ff - 1))(functools.partial(step, False, False))
        pl.when(f == n_ff - 1)(functools.partial(step, False, True))


def _ffn(xs, mod, g, w_in, w_out, which, *, mi, gi, n_tiles, seg_of, tail=None):
    d = xs.shape[1]
    ff = w_out.shape[2]
    tm, tf = TOKEN_TILE, FF_TILE
    nf = ff // tf
    layer, half = which
    if tail is None:
        head_tiles = None
        tokens = [xs, xs]
        token_specs = [pl.BlockSpec((tm, d), lambda i, f: (i, 0)), pl.BlockSpec((SUBLANES, d), lambda i, f: (0, 0))]
    else:
        head_tiles = xs.shape[0] // tm
        tokens = [xs, tail]
        token_specs = [pl.BlockSpec((tm, d), lambda i, f: (jnp.minimum(i, head_tiles - 1), 0)),
                       pl.BlockSpec((tm, d), lambda i, f: (jnp.maximum(i - head_tiles, 0), 0),
                                    pipeline_mode=pl.Buffered(1))]
    return pl.pallas_call(
        functools.partial(_ffn_kernel, mi=mi, gi=gi, head_tiles=head_tiles, n_ff=nf),
        out_shape=jax.ShapeDtypeStruct((n_tiles * tm, d), F32),
        grid=(n_tiles, nf),
        in_specs=token_specs + [
            pl.BlockSpec((1, MOD_ROWS, d), lambda i, f: (seg_of(i), 0, 0)),
            pl.BlockSpec((SUBLANES, d), lambda i, f: (0, 0)),
            pl.BlockSpec((None, None, None, 2, d, tf), lambda i, f: (layer, half, f, 0, 0, 0)),
            pl.BlockSpec((None, None, tf, d), lambda i, f: (layer, half, f, 0)),
        ],
        out_specs=pl.BlockSpec((tm, d), lambda i, f: (i, 0)),
        scratch_shapes=[pltpu.VMEM((tm, d), BF16), pltpu.VMEM((tm, d), F32)],
        compiler_params=_params("parallel", "arbitrary"),
        name="ffn_halfstep",
    )(*tokens, mod, g, w_in, w_out)


def _rope(t, cos, sin, even_quarter):
    partner = jnp.where(even_quarter, pltpu.roll(t, HEAD_DIM - ROPE_QUARTER, 1), pltpu.roll(t, ROPE_QUARTER, 1))
    return t * cos + partner * sin


def _qkv_kernel(x_ref, mod_ref, g_ref, w_ref, cos_ref, sin_ref, q_ref, k_ref, v_ref, *, mi, gi):
    h = _rms(x_ref[...], g_ref[gi:gi + 1, :]) * (1.0 + mod_ref[0, mi + 1:mi + 2, :]) + mod_ref[0, mi:mi + 1, :]
    qkv = jnp.dot(h.astype(BF16), w_ref[...], preferred_element_type=F32)
    cos, sin = cos_ref[...], sin_ref[...]
    lane = lax.broadcasted_iota(jnp.int32, cos.shape, 1)
    even_quarter = (lane & ROPE_QUARTER) == 0
    scale = HEAD_DIM ** -0.5
    nq, nk = N_HEADS * HEAD_DIM, N_KV_HEADS * HEAD_DIM
    for hh in range(N_HEADS):
        sl = slice(hh * HEAD_DIM, (hh + 1) * HEAD_DIM)
        q_ref[:, sl] = (_rope(qkv[:, sl], cos, sin, even_quarter) * scale).astype(BF16)
    for hh in range(N_KV_HEADS):
        sl = slice(hh * HEAD_DIM, (hh + 1) * HEAD_DIM)
        src = slice(nq + hh * HEAD_DIM, nq + (hh + 1) * HEAD_DIM)
        k_ref[:, sl] = _rope(qkv[:, src], cos, sin, even_quarter).astype(BF16)
    v_ref[...] = qkv[:, nq + nk:].astype(BF16)


def _qkv(xs, mod, g, w, cos, sin, *, mi, gi, n_tiles, seg_of, pos_of):
    d = xs.shape[1]
    tm = TOKEN_TILE
    nq, nk = N_HEADS * HEAD_DIM, N_KV_HEADS * HEAD_DIM
    rows = n_tiles * tm
    return pl.pallas_call(
        functools.partial(_qkv_kernel, mi=mi, gi=gi),
        out_shape=(jax.ShapeDtypeStruct((rows, nq), BF16), jax.ShapeDtypeStruct((rows, nk), BF16),
                   jax.ShapeDtypeStruct((rows, nk), BF16)),
        grid=(n_tiles,),
        in_specs=[
            pl.BlockSpec((tm, d), lambda i: (i, 0)),
            pl.BlockSpec((1, MOD_ROWS, d), lambda i: (seg_of(i), 0, 0)),
            pl.BlockSpec((SUBLANES, d), lambda i: (0, 0)),
            pl.BlockSpec((d, nq + 2 * nk), lambda i: (0, 0), pipeline_mode=pl.Buffered(1)),
            pl.BlockSpec((tm, HEAD_DIM), lambda i: (pos_of(i), 0)),
            pl.BlockSpec((tm, HEAD_DIM), lambda i: (pos_of(i), 0)),
        ],
        out_specs=(pl.BlockSpec((tm, nq), lambda i: (i, 0)), pl.BlockSpec((tm, nk), lambda i: (i, 0)),
                   pl.BlockSpec((tm, nk), lambda i: (i, 0))),
        compiler_params=_params("parallel"),
        name="attn_qkv_rope",
    )(xs, mod, g, w, cos, sin)


def _proj_in_kernel(x_ref, mod_ref, g_ref, w_ref, o_ref, slab_even, slab_odd, *, mi, gi):
    i = pl.program_id(0)

    @pl.when(i == 0)
    def _():
        slab_odd[...] = jnp.zeros_like(slab_odd)

    @pl.when(i % 2 == 0)
    def _():
        _proj_in_step(x_ref, mod_ref, g_ref, w_ref, o_ref, slab_even, slab_odd, mi=mi, gi=gi)

    @pl.when(i % 2 == 1)
    def _():
        _proj_in_step(x_ref, mod_ref, g_ref, w_ref, o_ref, slab_odd, slab_even, mi=mi, gi=gi)


def _proj_in_step(x_ref, mod_ref, g_ref, w_ref, o_ref, slab_new, slab_old, *, mi, gi):
    h = _rms(x_ref[...], g_ref[gi:gi + 1, :]) * (1.0 + mod_ref[0, mi + 1:mi + 2, :]) + mod_ref[0, mi:mi + 1, :]
    u = jnp.dot(h.astype(BF16), w_ref[...], preferred_element_type=F32)
    n_slab, tm, lanes = slab_new.shape
    pair = GROUPS_PER_STEP * SSM_GROUP_CH
    per_slab = lanes // pair
    for k in range(n_slab):
        slab_new[k] = u[:, k * lanes:(k + 1) * lanes]
    for t in range(SSM_CHUNK):
        for k in range(n_slab):
            rows = slab_old[k, pl.ds(t, tm // SSM_CHUNK, stride=SSM_CHUNK), :]
            for jj in range(per_slab):
                o_ref[k * per_slab + jj, :, t * pair:(t + 1) * pair] = rows[:, jj * pair:(jj + 1) * pair].astype(o_ref.dtype)


def _proj_in(xs, mod, g, w, *, mi, gi, n_tiles, seg_of):
    d = xs.shape[1]
    tm = TOKEN_TILE
    pair = GROUPS_PER_STEP * SSM_GROUP_CH
    n_pairs = w.shape[1] // pair
    return pl.pallas_call(
        functools.partial(_proj_in_kernel, mi=mi, gi=gi),
        out_shape=jax.ShapeDtypeStruct((n_pairs, n_tiles * tm // SSM_CHUNK, SSM_CHUNK * pair), BF16),
        grid=(n_tiles + 1,),
        in_specs=[
            pl.BlockSpec((tm, d), lambda i: (jnp.minimum(i, n_tiles - 1), 0)),
            pl.BlockSpec((1, MOD_ROWS, d), lambda i: (seg_of(jnp.minimum(i, n_tiles - 1)), 0, 0)),
            pl.BlockSpec((SUBLANES, d), lambda i: (0, 0)),
            pl.BlockSpec(w.shape, lambda i: (0, 0), pipeline_mode=pl.Buffered(1)),
        ],
        out_specs=pl.BlockSpec((n_pairs, tm // SSM_CHUNK, SSM_CHUNK * pair), lambda i: (0, jnp.maximum(i - 1, 0), 0)),
        scratch_shapes=[pltpu.VMEM((w.shape[1] // LANES, tm, LANES), F32)] * 2,
        compiler_params=_params("arbitrary"),
        name="ssm_in_proj",
    )(xs, mod, g, w)


def _proj_out_kernel(a_ref, x_ref, mod_ref, g_ref, w_ref, o_ref, *, mi, gi):
    out = jnp.dot(a_ref[...], w_ref[...], preferred_element_type=F32)
    o_ref[...] = x_ref[...] + mod_ref[0, mi:mi + 1, :] * _rms(out, g_ref[gi:gi + 1, :])


def _proj_out(a, xs, mod, g, w, *, mi, gi, n_tiles, seg_of):
    d = xs.shape[1]
    tm = TOKEN_TILE
    return pl.pallas_call(
        functools.partial(_proj_out_kernel, mi=mi, gi=gi),
        out_shape=jax.ShapeDtypeStruct((n_tiles * tm, d), F32),
        grid=(n_tiles,),
        in_specs=[
            pl.BlockSpec((tm, a.shape[1]), lambda i: (i, 0)),
            pl.BlockSpec((tm, d), lambda i: (i, 0)),
            pl.BlockSpec((1, MOD_ROWS, d), lambda i: (seg_of(i), 0, 0)),
            pl.BlockSpec((SUBLANES, d), lambda i: (0, 0)),
            pl.BlockSpec(w.shape, lambda i: (0, 0), pipeline_mode=pl.Buffered(1)),
        ],
        out_specs=pl.BlockSpec((tm, d), lambda i: (i, 0)),
        compiler_params=_params("parallel"),
        name="attn_out_proj",
    )(a, xs, mod, g, w)


def _attn_kernel(sink_ref, q_ref, kp_ref, ko_ref, kn_ref, vp_ref, vo_ref, vn_ref, kc_ref, vc_ref, bias_ref, o_ref):
    blk = ATT_BLOCK
    bias = jnp.concatenate([bias_ref[...]] * GQA_GROUP, axis=0)
    row = lax.broadcasted_iota(jnp.int32, (GQA_GROUP * blk, 1), 0)
    for h in range(N_KV_HEADS):
        hs = slice(h * HEAD_DIM, (h + 1) * HEAD_DIM)
        qs = jnp.concatenate(
            [q_ref[:, (h * GQA_GROUP + gq) * HEAD_DIM:(h * GQA_GROUP + gq + 1) * HEAD_DIM] for gq in range(GQA_GROUP)],
            axis=0)
        sink = jnp.full((GQA_GROUP * blk, 1), sink_ref[h * GQA_GROUP], F32)
        for gq in range(1, GQA_GROUP):
            sink = jnp.where(row >= gq * blk, sink_ref[h * GQA_GROUP + gq], sink)
        scores, col = [], 0
        for k_ref in (kp_ref, ko_ref, kn_ref, kc_ref):
            s = lax.dot_general(qs, k_ref[:, hs], (((1,), (1,)), ((), ())), preferred_element_type=F32)
            if k_ref is not kc_ref:
                s = s + bias[:, col:col + k_ref.shape[0]]
            col += k_ref.shape[0]
            scores.append(s)

        def lane_slabs(t):
            return [t[:, c:c + LANES] for c in range(0, t.shape[1], LANES)]

        m = functools.reduce(jnp.maximum, [sl for s in scores for sl in lane_slabs(s)])
        m = jnp.maximum(jnp.max(m, axis=-1, keepdims=True), sink)
        probs = [jnp.exp(s - m) for s in scores]
        denom = jnp.sum(functools.reduce(jnp.add, [sl for p in probs for sl in lane_slabs(p)]), axis=-1, keepdims=True)
        denom = denom + jnp.exp(sink - m)
        o = jnp.zeros((GQA_GROUP * blk, HEAD_DIM), F32)
        for p, v_ref in zip(probs, (vp_ref, vo_ref, vn_ref, vc_ref)):
            o = o + jnp.dot(p.astype(BF16), v_ref[:, hs], preferred_element_type=F32)
        o = o / denom
        for gq in range(GQA_GROUP):
            col = (h * GQA_GROUP + gq) * HEAD_DIM
            o_ref[:, col:col + HEAD_DIM] = o[gq * blk:(gq + 1) * blk, :].astype(o_ref.dtype)


def _attention(q, k, v, sink, *, batch, seq_len, c_len):
    blk = ATT_BLOCK
    n_blk, c_blk = seq_len // blk, c_len // blk
    nq, nk = q.shape[1], k.shape[1]
    lat_blocks = batch * n_blk
    ctx_block0 = batch * seq_len // c_len

    def q_idx(b, n):
        return jnp.where(n < n_blk, b * n_blk + n, lat_blocks + b * c_blk + (n - n_blk))

    def band_idx(off):
        def idx(b, n, s):
            nn = jnp.clip(n + off, 0, n_blk - 1)
            return (b * n_blk + nn, 0)
        return idx

    assert n_blk >= 2
    n_keys = 3 * blk + c_len
    qi = jnp.arange(blk)[:, None]
    kj = jnp.arange(n_keys)[None, :]
    in_band = (kj - qi >= 0) & (kj - qi <= 2 * blk)
    cases = [in_band & (kj >= blk), in_band, in_band & (kj < 2 * blk), jnp.zeros_like(in_band)]
    bias = jnp.stack([jnp.where((kj < 3 * blk) & ~case, NEG_INF, 0.0).astype(F32) for case in cases])

    def bias_idx(b, n, s):
        return (jnp.where(n == 0, 0, jnp.where(n < n_blk - 1, 1, jnp.where(n == n_blk - 1, 2, 3))), 0, 0)

    kv_spec = [pl.BlockSpec((blk, nk), band_idx(off)) for off in (-1, 0, 1)]
    ctx_spec = pl.BlockSpec((c_len, nk), lambda b, n, s: (ctx_block0 + b, 0))
    grid_spec = pltpu.PrefetchScalarGridSpec(
        num_scalar_prefetch=1,
        grid=(batch, n_blk + c_blk),
        in_specs=[pl.BlockSpec((blk, nq), lambda b, n, s: (q_idx(b, n), 0))] + kv_spec + kv_spec + [ctx_spec, ctx_spec]
        + [pl.BlockSpec((None, blk, n_keys), bias_idx)],
        out_specs=pl.BlockSpec((blk, nq), lambda b, n, s: (q_idx(b, n), 0)),
    )
    return pl.pallas_call(
        _attn_kernel,
        out_shape=jax.ShapeDtypeStruct(q.shape, BF16),
        grid_spec=grid_spec,
        compiler_params=_params("parallel", "arbitrary"),
        name="window_gqa",
    )(sink, q, k, k, k, v, v, v, k, v, bias)


def _s5_kernel(u_ref, t_ref, ws_ref, wc_ref, al_ref, d_ref, y_ref, s_scr, h_scr, *, batch, n_lat, n_ctx):
    half = GROUPS_PER_STEP * SSM_STATE
    u = u_ref[0]
    s_scr[...] = jnp.dot(u, ws_ref[0], preferred_element_type=F32)
    sub = SCAN_ROWS
    ctx0 = batch * n_lat
    zero = jnp.zeros((sub, half), F32)
    sub_row = lax.broadcasted_iota(jnp.int32, (sub, half), 0)

    def rep(v):
        return jnp.broadcast_to(v, (sub, half))

    def tables(row, fwd):
        pw = _powers(al_ref[0, row:row + 1, :], al_ref[0, row + 1:row + 2, :], sub)
        levels = []
        for k in (1 << b for b in range(sub.bit_length() - 1)):
            ok = (sub_row >= k) if fwd else (sub_row <= sub - 1 - k)
            levels.append((k if fwd else sub - k, jnp.where(ok, rep(pw[k][0]), 0.0), jnp.where(ok, rep(pw[k][1]), 0.0)))
        c_re, c_im = zero, zero
        for r in range(sub):
            e = r if fwd else sub - 1 - r
            c_re = jnp.where(sub_row == r, rep(pw[e][0]), c_re)
            c_im = jnp.where(sub_row == r, rep(pw[e][1]), c_im)
        inner = (sub_row >= 1) if fwd else (sub_row <= sub - 2)
        return dict(levels=levels, carry_w=(c_re, c_im), block_w=(rep(pw[sub][0]), rep(pw[sub][1])),
                    shift=1 if fwd else sub - 1, inner=inner, last=sub - 1 if fwd else 0)

    tbl_f, tbl_b = tables(0, True), tables(2, False)

    def scan_block(h, tbl, base, col, keep):
        h_re, h_im = h
        base = pl.multiple_of(base, sub)
        x_re = s_scr[pl.ds(base, sub), col:col + half]
        x_im = s_scr[pl.ds(base, sub), col + half:col + 2 * half]
        for shift, c_re, c_im in tbl["levels"]:
            r_re, r_im = pltpu.roll(x_re, shift, 0), pltpu.roll(x_im, shift, 0)
            x_re, x_im = x_re + c_re * r_re - c_im * r_im, x_im + c_re * r_im + c_im * r_re
        if keep:
            w_re, w_im = tbl["carry_w"]
            p_re = jnp.where(tbl["inner"], pltpu.roll(x_re, tbl["shift"], 0), 0.0)
            p_im = jnp.where(tbl["inner"], pltpu.roll(x_im, tbl["shift"], 0), 0.0)
            h_scr[pl.ds(base, sub), col:col + half] = w_re * h_re - w_im * h_im + p_re
            h_scr[pl.ds(base, sub), col + half:col + 2 * half] = w_re * h_im + w_im * h_re + p_im
        b_re, b_im = tbl["block_w"]
        last = tbl["last"]
        t_re, t_im = rep(x_re[last:last + 1, :]), rep(x_im[last:last + 1, :])
        return b_re * h_re - b_im * h_im + t_re, b_re * h_im + b_im * h_re + t_im

    def make_step(first_row, n_rows, keep):
        def step(m, carry):
            out = []
            for b in range(batch):
                hf, hb = carry[2 * b], carry[2 * b + 1]
                lo = first_row + b * n_rows
                hf = scan_block(hf, tbl_f, lo + m * sub, 0, keep)
                hb = scan_block(hb, tbl_b, lo + n_rows - sub - m * sub, 2 * half, keep)
                out += [hf, hb]
            return tuple(out)
        return step

    carry = lax.fori_loop(0, n_ctx // sub, make_step(ctx0, n_ctx, False), ((zero, zero),) * (2 * batch))
    lax.fori_loop(0, n_lat // sub, make_step(0, n_lat, True), carry)

    rows = batch * n_lat
    u_lat = u[:rows]
    y_ref[0] = (jnp.dot(u_lat, t_ref[0], preferred_element_type=F32)
                + jnp.dot(h_scr[...].astype(BF16), wc_ref[0], preferred_element_type=F32)
                + d_ref[0] * u_lat.astype(F32))


def _s5_core(ug, toep, ws, wc, al, d_lanes, *, batch, n_lat, n_ctx):
    n_steps, chunks, width = ug.shape
    assert n_lat % SCAN_ROWS == 0 and n_ctx % SCAN_ROWS == 0 and SCAN_ROWS & (SCAN_ROWS - 1) == 0
    rows = batch * n_lat
    return pl.pallas_call(
        functools.partial(_s5_kernel, batch=batch, n_lat=n_lat, n_ctx=n_ctx),
        out_shape=jax.ShapeDtypeStruct((n_steps, rows, width), F32),
        grid=(n_steps,),
        in_specs=[
            pl.BlockSpec((1, chunks, width), lambda j: (j, 0, 0)),
            pl.BlockSpec((1,) + toep.shape[1:], lambda j: (j, 0, 0)),
            pl.BlockSpec((1,) + ws.shape[1:], lambda j: (j, 0, 0)),
            pl.BlockSpec((1,) + wc.shape[1:], lambda j: (j, 0, 0)),
            pl.BlockSpec((1,) + al.shape[1:], lambda j: (j, 0, 0)),
            pl.BlockSpec((1, 1, width), lambda j: (j, 0, 0)),
        ],
        out_specs=pl.BlockSpec((1, rows, width), lambda j: (j, 0, 0)),
        scratch_shapes=[pltpu.VMEM((chunks, 4 * GROUPS_PER_STEP * SSM_STATE), F32),
                        pltpu.VMEM((rows, 4 * GROUPS_PER_STEP * SSM_STATE), F32)],
        compiler_params=_params("parallel"),
        name="s5_chunked",
    )(ug, toep, ws, wc, al, d_lanes)


def _cmul(a_re, a_im, b_re, b_im):
    return a_re * b_re - a_im * b_im, a_re * b_im + a_im * b_re


def _zoh(a_re, a_im, log_dt):
    dt = jnp.exp(log_dt)
    mag = jnp.exp(a_re * dt)
    ab_re, ab_im = mag * jnp.cos(a_im * dt), mag * jnp.sin(a_im * dt)
    den = a_re * a_re + a_im * a_im
    x_re, x_im = ab_re - 1.0, ab_im
    return ab_re, ab_im, (x_re * a_re + x_im * a_im) / den, (x_im * a_re - x_re * a_im) / den


def _powers(ab_re, ab_im, n):
    out = [(jnp.ones_like(ab_re), jnp.zeros_like(ab_im))]
    for _ in range(n):
        out.append(_cmul(out[-1][0], out[-1][1], ab_re, ab_im))
    return out


def _s5_op_kernel(row_ref, col_ref, bt_ref, ct_ref, rep_ref, t_ref, ws_ref, wc_ref, al_ref):
    L = SSM_CHUNK
    pair = ct_ref.shape[-1]
    assert L & (L - 1) == 0 and pair & (pair - 1) == 0
    hi = lax.Precision.HIGHEST
    lane_block = lax.broadcasted_iota(jnp.int32, (ct_ref.shape[-2], L * pair), 1) >> (pair.bit_length() - 1)

    def along_chunk(t):
        return jnp.dot(t, rep_ref[...], precision=hi, preferred_element_type=F32)

    inject, taps, read = [], [], []
    for d in range(2):
        ab_re, ab_im, f_re, f_im = _zoh(*(row_ref[0, 3 * d + r:3 * d + r + 1, :] for r in range(3)))
        pw_row = _powers(ab_re, ab_im, L)
        bb_re, bb_im = _cmul(f_re, f_im, bt_ref[0, d, 0], bt_ref[0, d, 1])
        order = range(L - 1, -1, -1) if d == 0 else range(L)
        inject.append([_cmul(pw_row[e][0], pw_row[e][1], bb_re, bb_im) for e in order])
        sq = [_zoh(*(col_ref[0, :, 3 * d + r:3 * d + r + 1] for r in range(3)))[:2]]
        while len(sq) < L.bit_length() - 1:
            sq.append(_cmul(*sq[-1], *sq[-1]))
        expo = lane_block if d == 0 else (L - 1) - lane_block
        low = (expo & 1) == 1
        p_re, p_im = jnp.where(low, sq[0][0], 1.0), jnp.where(low, sq[0][1], 0.0)
        for b in range(1, len(sq)):
            n_re, n_im = _cmul(p_re, p_im, *sq[b])
            bit = ((expo >> b) & 1) == 1
            p_re, p_im = jnp.where(bit, n_re, p_re), jnp.where(bit, n_im, p_im)
        ca_re, ca_im = _cmul(along_chunk(ct_ref[0, d, 0]), along_chunk(ct_ref[0, d, 1]), p_re, p_im)
        taps.append(jnp.dot(bb_re, ca_re, precision=hi, preferred_element_type=F32)
                    - jnp.dot(bb_im, ca_im, precision=hi, preferred_element_type=F32))
        rd_re, rd_im = _cmul(ca_re, ca_im, *sq[0])
        read.append((rd_re, -rd_im))
        al_ref[0, 2 * d:2 * d + 1, :] = pw_row[L][0]
        al_ref[0, 2 * d + 1:2 * d + 2, :] = pw_row[L][1]
    al_ref[0, 4:, :] = jnp.zeros((al_ref.shape[1] - 4, al_ref.shape[2]), F32)

    for s in range(L):
        ws_ref[0, s * pair:(s + 1) * pair, :] = jnp.concatenate(
            [inject[0][s][0], inject[0][s][1], inject[1][s][0], inject[1][s][1]], axis=1).astype(ws_ref.dtype)
    rows = read[0][0].shape[0]
    for q, blk in enumerate((read[0][0], read[0][1], read[1][0], read[1][1])):
        wc_ref[0, q * rows:(q + 1) * rows, :] = blk.astype(wc_ref.dtype)
    k_f, k_b = taps
    keep = (L - 1) * pair
    by_lag = jnp.concatenate([k_b[:, :keep], k_b[:, keep:] + k_f[:, :pair], k_f[:, pair:]], axis=1)
    for s in range(L):
        t_ref[0, s * pair:(s + 1) * pair, :] = by_lag[:, (L - 1 - s) * pair:(2 * L - 1 - s) * pair].astype(t_ref.dtype)


def _s5_operators(a_re, a_im, log_dt, b_re, b_im, c_re, c_im):
    n_g, n_p = a_re.shape[1], a_re.shape[2]
    gc = b_re.shape[-1]
    n_j, per = n_g // GROUPS_PER_STEP, GROUPS_PER_STEP
    states, pair, width = per * n_p, per * gc, SSM_CHUNK * per * gc
    lam = jnp.stack([a_re, a_im, jnp.broadcast_to(log_dt[..., None], a_re.shape)], axis=1)
    lam = jnp.transpose(lam.reshape(6, n_j, states), (1, 0, 2))
    lam = jnp.pad(lam, ((0, 0), (0, SUBLANES - lam.shape[1]), (0, 0)))
    eye = jnp.eye(per, dtype=F32)
    bt = jnp.stack([b_re, b_im], axis=1).reshape(2, 2, n_j, per, n_p, gc)
    bt = jnp.einsum('drjgpc,gh->jdrgchp', bt, eye).reshape(n_j, 2, 2, pair, states)
    ct = jnp.stack([c_re, c_im], axis=1).reshape(2, 2, n_j, per, gc, n_p)
    ct = jnp.einsum('drjgcp,gh->jdrgphc', ct, eye).reshape(n_j, 2, 2, states, pair)
    rep = jnp.tile(jnp.eye(pair, dtype=F32), (1, SSM_CHUNK))
    mat = jax.ShapeDtypeStruct((n_j, width, width), BF16)
    return pl.pallas_call(
        _s5_op_kernel,
        out_shape=(mat, mat, mat, jax.ShapeDtypeStruct((n_j, SUBLANES, states), F32)),
        grid=(n_j,),
        in_specs=[
            pl.BlockSpec((1, SUBLANES, states), lambda j: (j, 0, 0)),
            pl.BlockSpec((1, states, SUBLANES), lambda j: (j, 0, 0)),
            pl.BlockSpec((1, 2, 2, pair, states), lambda j: (j, 0, 0, 0, 0)),
            pl.BlockSpec((1, 2, 2, states, pair), lambda j: (j, 0, 0, 0, 0)),
            pl.BlockSpec((pair, width), lambda j: (0, 0)),
        ],
        out_specs=(pl.BlockSpec((1, width, width), lambda j: (j, 0, 0)),) * 3
        + (pl.BlockSpec((1, SUBLANES, states), lambda j: (j, 0, 0)),),
        compiler_params=_params("parallel"),
        name="s5_operators",
    )(lam, jnp.transpose(lam, (0, 2, 1)), bt, ct, rep)


def _ssm_out_kernel(y_ref, x_ref, mod_ref, g_ref, w_ref, o_ref, slab_scr, act_even, act_odd, out_scr, *, mi, gi):
    i = pl.program_id(0)

    @pl.when(i == 0)
    def _():
        act_odd[...] = jnp.zeros_like(act_odd)

    @pl.when(i % 2 == 0)
    def _():
        _ssm_out_step(y_ref, x_ref, mod_ref, g_ref, w_ref, o_ref, slab_scr, act_even, act_odd, out_scr, mi=mi, gi=gi)

    @pl.when(i % 2 == 1)
    def _():
        _ssm_out_step(y_ref, x_ref, mod_ref, g_ref, w_ref, o_ref, slab_scr, act_odd, act_even, out_scr, mi=mi, gi=gi)


def _ssm_out_step(y_ref, x_ref, mod_ref, g_ref, w_ref, o_ref, slab_scr, act_new, act_old, out_scr, *, mi, gi):
    n_slab, tm, lanes = slab_scr.shape
    pair = GROUPS_PER_STEP * SSM_GROUP_CH
    per_slab = lanes // pair
    for t in range(SSM_CHUNK):
        for k in range(n_slab):
            slab_scr[k, pl.ds(t, tm // SSM_CHUNK, stride=SSM_CHUNK), :] = jnp.concatenate(
                [y_ref[k * per_slab + jj, :, t * pair:(t + 1) * pair] for jj in range(per_slab)], axis=-1)
    for k in range(n_slab):
        act_new[:, k * lanes:(k + 1) * lanes] = _gelu_tanh(slab_scr[k]).astype(BF16)

    n_c, _, tn = out_scr.shape
    d = n_c * tn
    a = act_old[...]
    ssq = jnp.zeros((tm, 1), F32)
    for cc in range(n_c):
        val = jnp.dot(a, w_ref[:, cc * tn:(cc + 1) * tn], preferred_element_type=F32)
        gate = jnp.dot(a, w_ref[:, d + cc * tn:d + (cc + 1) * tn], preferred_element_type=F32)
        blk = val * _sigmoid(gate)
        out_scr[cc] = blk
        ssq = ssq + jnp.sum(blk * blk, axis=-1, keepdims=True)
    inv = lax.rsqrt(ssq / d + EPS)
    for cc in range(n_c):
        sl = slice(cc * tn, (cc + 1) * tn)
        o_ref[:, sl] = x_ref[:, sl] + mod_ref[0, mi:mi + 1, sl] * (out_scr[cc] * inv * g_ref[gi:gi + 1, sl])


def _ssm_out(yg, xs, mod, g, w_glu, *, mi, gi, n_rows, seg_of_row):
    d = xs.shape[1]
    tm, tn = GLU_TOKEN_TILE, GLU_TILE
    n_c = d // tn
    n_pairs, _, width = yg.shape
    n_tiles = n_rows // tm

    def prev(i):
        return jnp.maximum(i - 1, 0)

    return pl.pallas_call(
        functools.partial(_ssm_out_kernel, mi=mi, gi=gi),
        out_shape=jax.ShapeDtypeStruct((n_rows, d), F32),
        grid=(n_tiles + 1,),
        in_specs=[
            pl.BlockSpec((n_pairs, tm // SSM_CHUNK, width), lambda i: (0, jnp.minimum(i, n_tiles - 1), 0)),
            pl.BlockSpec((tm, d), lambda i: (prev(i), 0)),
            pl.BlockSpec((1, MOD_ROWS, d), lambda i: (seg_of_row(prev(i) * tm), 0, 0)),
            pl.BlockSpec((SUBLANES, d), lambda i: (0, 0)),
            pl.BlockSpec(w_glu.shape, lambda i: (0, 0), pipeline_mode=pl.Buffered(1)),
        ],
        out_specs=pl.BlockSpec((tm, d), lambda i: (prev(i), 0)),
        scratch_shapes=[pltpu.VMEM((d // LANES, tm, LANES), F32), pltpu.VMEM((tm, d), BF16),
                        pltpu.VMEM((tm, d), BF16), pltpu.VMEM((n_c, tm, tn), F32)],
        compiler_params=_params("arbitrary"),
        name="ssm_glu_out",
    )(yg, xs, mod, g, w_glu)


def _rope_tables(seq_len, extra_rows):
    rows = seq_len // GRID_W
    axis_dim = HEAD_DIM // 2
    inv_freq = ROPE_BASE ** (-jnp.arange(0, axis_dim, 2, dtype=F32) / axis_dim)
    ang_r = jnp.arange(rows, dtype=F32)[:, None] * inv_freq
    ang_c = jnp.arange(GRID_W, dtype=F32)[:, None] * inv_freq
    by_row = lambda t: jnp.repeat(t, GRID_W, axis=0)
    by_col = lambda t: jnp.tile(t, (rows, 1))
    cos = jnp.concatenate([by_row(jnp.cos(ang_r))] * 2 + [by_col(jnp.cos(ang_c))] * 2, axis=-1)
    sin = jnp.concatenate([by_row(-jnp.sin(ang_r)), by_row(jnp.sin(ang_r)),
                           by_col(-jnp.sin(ang_c)), by_col(jnp.sin(ang_c))], axis=-1)
    cos = jnp.concatenate([cos, jnp.ones((extra_rows, HEAD_DIM), F32)], axis=0)
    sin = jnp.concatenate([sin, jnp.zeros((extra_rows, HEAD_DIM), F32)], axis=0)
    return cos, sin


def kernel(x, c, ctx, c_ctx, ada_w, ada_b, norm_g, ffn_w_in, ffn_w_out, attn_w_in, attn_w_out, attn_sink,
           ssm_w_in, ssm_a_re, ssm_a_im, ssm_log_dt, ssm_b_re, ssm_b_im, ssm_c_re, ssm_c_im, ssm_d, ssm_w_glu):
    batch, seq_len, d = x.shape
    c_len = ctx.shape[1]
    depth = ada_w.shape[0]
    tm = TOKEN_TILE
    n_lat_rows, n_ctx_rows = batch * seq_len, batch * c_len
    assert seq_len % tm == 0 and n_ctx_rows % tm == 0 and n_lat_rows % c_len == 0
    assert seq_len % ATT_BLOCK == 0 and c_len % ATT_BLOCK == 0 and seq_len % GRID_W == 0
    tiles_per_batch = seq_len // tm
    lat_tiles = n_lat_rows // tm
    all_tiles = lat_tiles + n_ctx_rows // tm

    def seg_of(i):
        return jnp.minimum(i // tiles_per_batch, batch)

    def pos_of(i):
        return jnp.where(i < lat_tiles, i % tiles_per_batch, tiles_per_batch + (i - lat_tiles))

    def seg_of_row(r):
        return jnp.minimum(r // seq_len, batch)

    xs, tail = x.reshape(n_lat_rows, d), ctx.reshape(n_ctx_rows, d)
    w_ff_in, w_ff_out = _ffn_w_in_chunks(ffn_w_in), ffn_w_out.astype(BF16)

    n_cond = batch + 1
    cvec = jnp.concatenate([c, c_ctx[None], jnp.zeros((SUBLANES - n_cond, d), F32)], axis=0)
    mods = _modulation(cvec.T, ada_w, ada_b, n_cond)
    mods = mods[:, :n_cond].reshape(depth, n_cond, N_MOD, d)
    mods = jnp.pad(mods, ((0, 0), (0, 0), (0, MOD_ROWS - N_MOD), (0, 0)))
    gains = jnp.pad(norm_g, ((0, 0), (0, SUBLANES - norm_g.shape[1]), (0, 0)))

    cos, sin = _rope_tables(seq_len, n_ctx_rows)

    for i in range(depth):
        need_ctx = i < depth - 1
        mod, g = mods[i], gains[i]
        j = i // 2
        xs = _ffn(xs, mod, g, w_ff_in, w_ff_out, (i, 0), mi=0, gi=0, n_tiles=all_tiles, seg_of=seg_of, tail=tail)
        tail = None
        post_tiles = all_tiles if need_ctx else lat_tiles
        if i % 2 == 0:
            q, k, v = _qkv(xs, mod, g, attn_w_in[j].astype(BF16), cos, sin,
                           mi=3, gi=2, n_tiles=all_tiles, seg_of=seg_of, pos_of=pos_of)
            o = _attention(q, k, v, attn_sink[j], batch=batch, seq_len=seq_len, c_len=c_len)
            xs = _proj_out(o, xs, mod, g, attn_w_out[j].astype(BF16),
                           mi=5, gi=3, n_tiles=post_tiles, seg_of=seg_of)
        else:
            if need_ctx:
                raise NotImplementedError("context output of the S5 mixer is only needed when another layer follows")
            ug = _proj_in(xs, mod, g, ssm_w_in[j].astype(BF16), mi=3, gi=2, n_tiles=all_tiles, seg_of=seg_of)
            L = SSM_CHUNK
            pair = GROUPS_PER_STEP * SSM_GROUP_CH
            toep, ws, wc, al = _s5_operators(ssm_a_re[j], ssm_a_im[j], ssm_log_dt[j], ssm_b_re[j], ssm_b_im[j],
                                             ssm_c_re[j], ssm_c_im[j])
            d_lanes = jnp.tile(ssm_d[j].reshape(d // pair, 1, pair), (1, 1, L))
            yg = _s5_core(ug, toep, ws, wc, al, d_lanes, batch=batch, n_lat=seq_len // L, n_ctx=c_len // L)
            xs = _ssm_out(yg, xs, mod, g, ssm_w_glu[j].astype(BF16),
                          mi=5, gi=3, n_rows=post_tiles * tm, seg_of_row=seg_of_row)
        xs = _ffn(xs, mod, g, w_ff_in, w_ff_out, (i, 1), mi=6, gi=4, n_tiles=post_tiles, seg_of=seg_of)
    return xs[:n_lat_rows].reshape(batch, seq_len, d)
```

```python
import functools
import math

import jax
import jax.numpy as jnp
from jax import lax
from jax.experimental import pallas as pl
from jax.experimental.pallas import tpu as pltpu

EPS = 1e-6
NEG_INF = -1e30
N_MOD = 9
GRID_W = 64
N_HEADS = 16
N_KV_HEADS = 4
HEAD_DIM = 128
GQA_GROUP = N_HEADS // N_KV_HEADS
ATT_BLOCK = 128
ROPE_BASE = 10000.0
ROPE_QUARTER = HEAD_DIM // 4
SSM_GROUP_CH = 16
SSM_STATE = 64
SSM_CHUNK = 16
GROUPS_PER_STEP = 2
SCAN_ROWS = 8

TOKEN_TILE = 512
FF_TILE = 512
FF_STEP_CHUNKS = 2
GLU_TILE = 512
GLU_TOKEN_TILE = 256
MOD_TILE = 1024
MOD_ROWS = 16
VMEM_LIMIT = 58 * 1024 * 1024
LANES = 128
SUBLANES = 8

BF16 = jnp.bfloat16
F32 = jnp.float32


def _params(*sem):
    return pltpu.CompilerParams(dimension_semantics=sem, vmem_limit_bytes=VMEM_LIMIT)


def _rms(t, g):
    return t * lax.rsqrt(jnp.mean(t * t, axis=-1, keepdims=True) + EPS) * g


def _sigmoid(t):
    return 1.0 / (1.0 + jnp.exp(-t))


def _gelu_tanh(t):
    return 0.5 * t * (1.0 + jnp.tanh(math.sqrt(2.0 / math.pi) * (t + 0.044715 * (t * t * t))))


def _mod_kernel(ct_ref, w_ref, b_ref, o_ref, *, n_rows):
    w = w_ref[0]
    for r in range(n_rows):
        col = ct_ref[:, r:r + 1]
        col = col * _sigmoid(col)
        o_ref[0, r:r + 1, :] = jnp.sum(w * col, axis=0, keepdims=True) + b_ref[0]
    o_ref[0, n_rows:, :] = jnp.zeros((o_ref.shape[1] - n_rows, o_ref.shape[2]), F32)


def _modulation(cvec_t, ada_w, ada_b, n_rows):
    depth, d, nd = ada_w.shape
    tn = MOD_TILE
    return pl.pallas_call(
        functools.partial(_mod_kernel, n_rows=n_rows),
        out_shape=jax.ShapeDtypeStruct((depth, SUBLANES, nd), F32),
        grid=(depth, nd // tn),
        in_specs=[
            pl.BlockSpec((d, SUBLANES), lambda l, j: (0, 0)),
            pl.BlockSpec((1, d, tn), lambda l, j: (l, 0, j)),
            pl.BlockSpec((1, 1, tn), lambda l, j: (l, 0, j)),
        ],
        out_specs=pl.BlockSpec((1, SUBLANES, tn), lambda l, j: (l, 0, j)),
        compiler_params=_params("arbitrary", "arbitrary"),
        name="adaln_mod",
    )(cvec_t, ada_w, ada_b.reshape(depth, 1, nd))


def _ffn_w_cast_kernel(w_ref, o_ref):
    o_ref[...] = w_ref[...].astype(o_ref.dtype)


def _ffn_w_in_chunks(w):
    n_l, n_h, d, f2 = w.shape
    tf, per = FF_TILE, FF_STEP_CHUNKS
    nf = f2 // 2 // tf
    return pl.pallas_call(
        _ffn_w_cast_kernel,
        out_shape=jax.ShapeDtypeStruct((n_l, n_h, pl.cdiv(nf, per), per, 2, d, tf), BF16),
        grid=(n_l, n_h, nf, 2),
        in_specs=[pl.BlockSpec((None, None, d, tf), lambda l, h, f, gu: (l, h, 0, gu * nf + f))],
        out_specs=pl.BlockSpec((None, None, None, None, None, d, tf), lambda l, h, f, gu: (l, h, f // per, f % per, gu, 0, 0)),
        compiler_params=_params("parallel", "parallel", "parallel", "parallel"),
        name="ffn_w_cast",
    )(w)


def _ffn_kernel(x_ref, tail_ref, mod_ref, g_ref, wgu_ref, wo_ref, o_ref, h_scr, acc_scr,
                *, mi, gi, head_tiles, n_ff):
    def read_x():
        if head_tiles is None:
            return x_ref[...]
        return jnp.where(pl.program_id(0) < head_tiles, x_ref[...], tail_ref[...])

    tf = wgu_ref.shape[-1]
    n_steps = pl.cdiv(n_ff, FF_STEP_CHUNKS)

    def step(first, last, n_chunks):
        if first:
            xv = read_x()
            inv = lax.rsqrt(jnp.mean(xv * xv, axis=-1, keepdims=True) + EPS)
            h = (xv * inv) * (g_ref[gi:gi + 1, :] * (1.0 + mod_ref[0, mi + 1:mi + 2, :])) + mod_ref[0, mi:mi + 1, :]
            h = h.astype(BF16)
            if not last:
                h_scr[...] = h
        else:
            h = h_scr[...]
        out = None if first else acc_scr[...]
        for c in range(n_chunks):
            gate = jnp.dot(h, wgu_ref[c, 0], preferred_element_type=F32)
            up = jnp.dot(h, wgu_ref[c, 1], preferred_element_type=F32)
            act = (gate * _sigmoid(gate) * up).astype(BF16)
            part = jnp.dot(act, wo_ref[c * tf:(c + 1) * tf, :], preferred_element_type=F32)
            out = part if out is None else out + part
        if last:
            inv = lax.rsqrt(jnp.mean(out * out, axis=-1, keepdims=True) + EPS)
            o_ref[...] = read_x() + (out * inv) * (0.5 * mod_ref[0, mi + 2:mi + 3, :] * g_ref[gi + 1:gi + 2, :])
        else:
            acc_scr[...] = out

    f = pl.program_id(1)
    last_chunks = n_ff - (n_steps - 1) * FF_STEP_CHUNKS
    if n_steps == 1:
        step(True, True, last_chunks)
    else:
        pl.when(f == 0)(functools.partial(step, True, False, FF_STEP_CHUNKS))
        if n_steps > 2:
            pl.when((f > 0) & (f < n_steps - 1))(functools.partial(step, False, False, FF_STEP_CHUNKS))
        pl.when(f == n_steps - 1)(functools.partial(step, False, True, last_chunks))


def _ffn(xs, mod, g, w_in, w_out, which, *, mi, gi, n_tiles, seg_of, tail=None):
    d = xs.shape[1]
    ff = w_out.shape[2]
    tm, tf, per = TOKEN_TILE, FF_TILE, FF_STEP_CHUNKS
    nf = ff // tf
    n_steps = pl.cdiv(nf, per)
    assert w_in.shape[2:4] == (n_steps, per)
    layer, half = which
    if tail is None:
        head_tiles = None
        tokens = [xs, xs]
        token_specs = [pl.BlockSpec((tm, d), lambda i, f: (i, 0)), pl.BlockSpec((SUBLANES, d), lambda i, f: (0, 0))]
    else:
        head_tiles = xs.shape[0] // tm
        tokens = [xs, tail]
        token_specs = [pl.BlockSpec((tm, d), lambda i, f: (jnp.minimum(i, head_tiles - 1), 0)),
                       pl.BlockSpec((tm, d), lambda i, f: (jnp.maximum(i - head_tiles, 0), 0),
                                    pipeline_mode=pl.Buffered(1))]
    return pl.pallas_call(
        functools.partial(_ffn_kernel, mi=mi, gi=gi, head_tiles=head_tiles, n_ff=nf),
        out_shape=jax.ShapeDtypeStruct((n_tiles * tm, d), F32),
        grid=(n_tiles, n_steps),
        in_specs=token_specs + [
            pl.BlockSpec((1, MOD_ROWS, d), lambda i, f: (seg_of(i), 0, 0)),
            pl.BlockSpec((SUBLANES, d), lambda i, f: (0, 0)),
            pl.BlockSpec((None, None, None, per, 2, d, tf), lambda i, f: (layer, half, f, 0, 0, 0, 0)),
            pl.BlockSpec((None, None, per * tf, d), lambda i, f: (layer, half, f, 0)),
        ],
        out_specs=pl.BlockSpec((tm, d), lambda i, f: (i, 0)),
        scratch_shapes=[pltpu.VMEM((tm, d), BF16), pltpu.VMEM((tm, d), F32)],
        compiler_params=_params("parallel", "arbitrary"),
        name="ffn_halfstep",
    )(*tokens, mod, g, w_in, w_out)


def _rope(t, cos, sin, even_quarter):
    partner = jnp.where(even_quarter, pltpu.roll(t, HEAD_DIM - ROPE_QUARTER, 1), pltpu.roll(t, ROPE_QUARTER, 1))
    return t * cos + partner * sin


def _qkv_kernel(x_ref, mod_ref, g_ref, w_ref, cos_ref, sin_ref, q_ref, k_ref, v_ref, *, mi, gi):
    h = _rms(x_ref[...], g_ref[gi:gi + 1, :]) * (1.0 + mod_ref[0, mi + 1:mi + 2, :]) + mod_ref[0, mi:mi + 1, :]
    qkv = jnp.dot(h.astype(BF16), w_ref[...], preferred_element_type=F32)
    cos, sin = cos_ref[...], sin_ref[...]
    lane = lax.broadcasted_iota(jnp.int32, cos.shape, 1)
    even_quarter = (lane & ROPE_QUARTER) == 0
    scale = HEAD_DIM ** -0.5
    nq, nk = N_HEADS * HEAD_DIM, N_KV_HEADS * HEAD_DIM
    for hh in range(N_HEADS):
        sl = slice(hh * HEAD_DIM, (hh + 1) * HEAD_DIM)
        q_ref[:, sl] = (_rope(qkv[:, sl], cos, sin, even_quarter) * scale).astype(BF16)
    for hh in range(N_KV_HEADS):
        sl = slice(hh * HEAD_DIM, (hh + 1) * HEAD_DIM)
        src = slice(nq + hh * HEAD_DIM, nq + (hh + 1) * HEAD_DIM)
        k_ref[:, sl] = _rope(qkv[:, src], cos, sin, even_quarter).astype(BF16)
    v_ref[...] = qkv[:, nq + nk:].astype(BF16)


def _qkv(xs, mod, g, w, cos, sin, *, mi, gi, n_tiles, seg_of, pos_of):
    d = xs.shape[1]
    tm = TOKEN_TILE
    nq, nk = N_HEADS * HEAD_DIM, N_KV_HEADS * HEAD_DIM
    rows = n_tiles * tm
    return pl.pallas_call(
        functools.partial(_qkv_kernel, mi=mi, gi=gi),
        out_shape=(jax.ShapeDtypeStruct((rows, nq), BF16), jax.ShapeDtypeStruct((rows, nk), BF16),
                   jax.ShapeDtypeStruct((rows, nk), BF16)),
        grid=(n_tiles,),
        in_specs=[
            pl.BlockSpec((tm, d), lambda i: (i, 0)),
            pl.BlockSpec((1, MOD_ROWS, d), lambda i: (seg_of(i), 0, 0)),
            pl.BlockSpec((SUBLANES, d), lambda i: (0, 0)),
            pl.BlockSpec((d, nq + 2 * nk), lambda i: (0, 0), pipeline_mode=pl.Buffered(1)),
            pl.BlockSpec((tm, HEAD_DIM), lambda i: (pos_of(i), 0)),
            pl.BlockSpec((tm, HEAD_DIM), lambda i: (pos_of(i), 0)),
        ],
        out_specs=(pl.BlockSpec((tm, nq), lambda i: (i, 0)), pl.BlockSpec((tm, nk), lambda i: (i, 0)),
                   pl.BlockSpec((tm, nk), lambda i: (i, 0))),
        compiler_params=_params("parallel"),
        name="attn_qkv_rope",
    )(xs, mod, g, w, cos, sin)


def _proj_in_kernel(x_ref, mod_ref, g_ref, w_ref, o_ref, slab_even, slab_odd, *, mi, gi):
    i = pl.program_id(0)

    @pl.when(i == 0)
    def _():
        slab_odd[...] = jnp.zeros_like(slab_odd)

    @pl.when(i % 2 == 0)
    def _():
        _proj_in_step(x_ref, mod_ref, g_ref, w_ref, o_ref, slab_even, slab_odd, mi=mi, gi=gi)

    @pl.when(i % 2 == 1)
    def _():
        _proj_in_step(x_ref, mod_ref, g_ref, w_ref, o_ref, slab_odd, slab_even, mi=mi, gi=gi)


def _proj_in_step(x_ref, mod_ref, g_ref, w_ref, o_ref, slab_new, slab_old, *, mi, gi):
    h = _rms(x_ref[...], g_ref[gi:gi + 1, :]) * (1.0 + mod_ref[0, mi + 1:mi + 2, :]) + mod_ref[0, mi:mi + 1, :]
    u = jnp.dot(h.astype(BF16), w_ref[...], preferred_element_type=F32)
    n_slab, tm, lanes = slab_new.shape
    pair = GROUPS_PER_STEP * SSM_GROUP_CH
    per_slab = lanes // pair
    for k in range(n_slab):
        slab_new[k] = u[:, k * lanes:(k + 1) * lanes]
    for t in range(SSM_CHUNK):
        for k in range(n_slab):
            rows = slab_old[k, pl.ds(t, tm // SSM_CHUNK, stride=SSM_CHUNK), :]
            for jj in range(per_slab):
                o_ref[k * per_slab + jj, :, t * pair:(t + 1) * pair] = rows[:, jj * pair:(jj + 1) * pair].astype(o_ref.dtype)


def _proj_in(xs, mod, g, w, *, mi, gi, n_tiles, seg_of):
    d = xs.shape[1]
    tm = TOKEN_TILE
    pair = GROUPS_PER_STEP * SSM_GROUP_CH
    n_pairs = w.shape[1] // pair
    return pl.pallas_call(
        functools.partial(_proj_in_kernel, mi=mi, gi=gi),
        out_shape=jax.ShapeDtypeStruct((n_pairs, n_tiles * tm // SSM_CHUNK, SSM_CHUNK * pair), BF16),
        grid=(n_tiles + 1,),
        in_specs=[
            pl.BlockSpec((tm, d), lambda i: (jnp.minimum(i, n_tiles - 1), 0)),
            pl.BlockSpec((1, MOD_ROWS, d), lambda i: (seg_of(jnp.minimum(i, n_tiles - 1)), 0, 0)),
            pl.BlockSpec((SUBLANES, d), lambda i: (0, 0)),
            pl.BlockSpec(w.shape, lambda i: (0, 0), pipeline_mode=pl.Buffered(1)),
        ],
        out_specs=pl.BlockSpec((n_pairs, tm // SSM_CHUNK, SSM_CHUNK * pair), lambda i: (0, jnp.maximum(i - 1, 0), 0)),
        scratch_shapes=[pltpu.VMEM((w.shape[1] // LANES, tm, LANES), F32)] * 2,
        compiler_params=_params("arbitrary"),
        name="ssm_in_proj",
    )(xs, mod, g, w)


def _proj_out_kernel(a_ref, x_ref, mod_ref, g_ref, w_ref, o_ref, *, mi, gi):
    out = jnp.dot(a_ref[...], w_ref[...], preferred_element_type=F32)
    o_ref[...] = x_ref[...] + mod_ref[0, mi:mi + 1, :] * _rms(out, g_ref[gi:gi + 1, :])


def _proj_out(a, xs, mod, g, w, *, mi, gi, n_tiles, seg_of):
    d = xs.shape[1]
    tm = TOKEN_TILE
    return pl.pallas_call(
        functools.partial(_proj_out_kernel, mi=mi, gi=gi),
        out_shape=jax.ShapeDtypeStruct((n_tiles * tm, d), F32),
        grid=(n_tiles,),
        in_specs=[
            pl.BlockSpec((tm, a.shape[1]), lambda i: (i, 0)),
            pl.BlockSpec((tm, d), lambda i: (i, 0)),
            pl.BlockSpec((1, MOD_ROWS, d), lambda i: (seg_of(i), 0, 0)),
            pl.BlockSpec((SUBLANES, d), lambda i: (0, 0)),
            pl.BlockSpec(w.shape, lambda i: (0, 0), pipeline_mode=pl.Buffered(1)),
        ],
        out_specs=pl.BlockSpec((tm, d), lambda i: (i, 0)),
        compiler_params=_params("parallel"),
        name="attn_out_proj",
    )(a, xs, mod, g, w)


def _attn_kernel(sink_ref, q_ref, kp_ref, ko_ref, kn_ref, vp_ref, vo_ref, vn_ref, kc_ref, vc_ref, bias_ref, o_ref):
    blk = ATT_BLOCK
    bias = jnp.concatenate([bias_ref[...]] * GQA_GROUP, axis=0)
    row = lax.broadcasted_iota(jnp.int32, (GQA_GROUP * blk, 1), 0)
    for h in range(N_KV_HEADS):
        hs = slice(h * HEAD_DIM, (h + 1) * HEAD_DIM)
        qs = jnp.concatenate(
            [q_ref[:, (h * GQA_GROUP + gq) * HEAD_DIM:(h * GQA_GROUP + gq + 1) * HEAD_DIM] for gq in range(GQA_GROUP)],
            axis=0)
        sink = jnp.full((GQA_GROUP * blk, 1), sink_ref[h * GQA_GROUP], F32)
        for gq in range(1, GQA_GROUP):
            sink = jnp.where(row >= gq * blk, sink_ref[h * GQA_GROUP + gq], sink)
        scores, col = [], 0
        for k_ref in (kp_ref, ko_ref, kn_ref, kc_ref):
            s = lax.dot_general(qs, k_ref[:, hs], (((1,), (1,)), ((), ())), preferred_element_type=F32)
            if k_ref is not kc_ref:
                s = s + bias[:, col:col + k_ref.shape[0]]
            col += k_ref.shape[0]
            scores.append(s)

        def lane_slabs(t):
            return [t[:, c:c + LANES] for c in range(0, t.shape[1], LANES)]

        m = functools.reduce(jnp.maximum, [sl for s in scores for sl in lane_slabs(s)])
        m = jnp.maximum(jnp.max(m, axis=-1, keepdims=True), sink)
        probs = [jnp.exp(s - m) for s in scores]
        denom = jnp.sum(functools.reduce(jnp.add, [sl for p in probs for sl in lane_slabs(p)]), axis=-1, keepdims=True)
        denom = denom + jnp.exp(sink - m)
        o = jnp.zeros((GQA_GROUP * blk, HEAD_DIM), F32)
        for p, v_ref in zip(probs, (vp_ref, vo_ref, vn_ref, vc_ref)):
            o = o + jnp.dot(p.astype(BF16), v_ref[:, hs], preferred_element_type=F32)
        o = o / denom
        for gq in range(GQA_GROUP):
            col = (h * GQA_GROUP + gq) * HEAD_DIM
            o_ref[:, col:col + HEAD_DIM] = o[gq * blk:(gq + 1) * blk, :].astype(o_ref.dtype)


def _attention(q, k, v, sink, *, batch, seq_len, c_len):
    blk = ATT_BLOCK
    n_blk, c_blk = seq_len // blk, c_len // blk
    nq, nk = q.shape[1], k.shape[1]
    lat_blocks = batch * n_blk
    ctx_block0 = batch * seq_len // c_len

    def q_idx(b, n):
        return jnp.where(n < n_blk, b * n_blk + n, lat_blocks + b * c_blk + (n - n_blk))

    def band_idx(off):
        def idx(b, n, s):
            nn = jnp.clip(n + off, 0, n_blk - 1)
            return (b * n_blk + nn, 0)
        return idx

    assert n_blk >= 2
    n_keys = 3 * blk + c_len
    qi = jnp.arange(blk)[:, None]
    kj = jnp.arange(n_keys)[None, :]
    in_band = (kj - qi >= 0) & (kj - qi <= 2 * blk)
    cases = [in_band & (kj >= blk), in_band, in_band & (kj < 2 * blk), jnp.zeros_like(in_band)]
    bias = jnp.stack([jnp.where((kj < 3 * blk) & ~case, NEG_INF, 0.0).astype(F32) for case in cases])

    def bias_idx(b, n, s):
        return (jnp.where(n == 0, 0, jnp.where(n < n_blk - 1, 1, jnp.where(n == n_blk - 1, 2, 3))), 0, 0)

    kv_spec = [pl.BlockSpec((blk, nk), band_idx(off)) for off in (-1, 0, 1)]
    ctx_spec = pl.BlockSpec((c_len, nk), lambda b, n, s: (ctx_block0 + b, 0))
    grid_spec = pltpu.PrefetchScalarGridSpec(
        num_scalar_prefetch=1,
        grid=(batch, n_blk + c_blk),
        in_specs=[pl.BlockSpec((blk, nq), lambda b, n, s: (q_idx(b, n), 0))] + kv_spec + kv_spec + [ctx_spec, ctx_spec]
        + [pl.BlockSpec((None, blk, n_keys), bias_idx)],
        out_specs=pl.BlockSpec((blk, nq), lambda b, n, s: (q_idx(b, n), 0)),
    )
    return pl.pallas_call(
        _attn_kernel,
        out_shape=jax.ShapeDtypeStruct(q.shape, BF16),
        grid_spec=grid_spec,
        compiler_params=_params("parallel", "arbitrary"),
        name="window_gqa",
    )(sink, q, k, k, k, v, v, v, k, v, bias)


def _s5_kernel(u_ref, t_ref, ws_ref, wc_ref, al_ref, d_ref, y_ref, s_scr, h_scr, *, batch, n_lat, n_ctx):
    half = GROUPS_PER_STEP * SSM_STATE
    u = u_ref[0]
    s_scr[...] = jnp.dot(u, ws_ref[0], preferred_element_type=F32)
    sub = SCAN_ROWS
    ctx0 = batch * n_lat
    zero = jnp.zeros((sub, half), F32)
    sub_row = lax.broadcasted_iota(jnp.int32, (sub, half), 0)

    def rep(v):
        return jnp.broadcast_to(v, (sub, half))

    def tables(row, fwd):
        pw = _powers(al_ref[0, row:row + 1, :], al_ref[0, row + 1:row + 2, :], sub)
        levels = []
        for k in (1 << b for b in range(sub.bit_length() - 1)):
            ok = (sub_row >= k) if fwd else (sub_row <= sub - 1 - k)
            levels.append((k if fwd else sub - k, jnp.where(ok, rep(pw[k][0]), 0.0), jnp.where(ok, rep(pw[k][1]), 0.0)))
        c_re, c_im = zero, zero
        for r in range(sub):
            e = r if fwd else sub - 1 - r
            c_re = jnp.where(sub_row == r, rep(pw[e][0]), c_re)
            c_im = jnp.where(sub_row == r, rep(pw[e][1]), c_im)
        inner = (sub_row >= 1) if fwd else (sub_row <= sub - 2)
        return dict(levels=levels, carry_w=(c_re, c_im), block_w=(rep(pw[sub][0]), rep(pw[sub][1])),
                    shift=1 if fwd else sub - 1, inner=inner, last=sub - 1 if fwd else 0)

    tbl_f, tbl_b = tables(0, True), tables(2, False)

    def scan_block(h, tbl, base, col, keep):
        h_re, h_im = h
        base = pl.multiple_of(base, sub)
        x_re = s_scr[pl.ds(base, sub), col:col + half]
        x_im = s_scr[pl.ds(base, sub), col + half:col + 2 * half]
        for shift, c_re, c_im in tbl["levels"]:
            r_re, r_im = pltpu.roll(x_re, shift, 0), pltpu.roll(x_im, shift, 0)
            x_re, x_im = x_re + c_re * r_re - c_im * r_im, x_im + c_re * r_im + c_im * r_re
        if keep:
            w_re, w_im = tbl["carry_w"]
            p_re = jnp.where(tbl["inner"], pltpu.roll(x_re, tbl["shift"], 0), 0.0)
            p_im = jnp.where(tbl["inner"], pltpu.roll(x_im, tbl["shift"], 0), 0.0)
            h_scr[pl.ds(base, sub), col:col + half] = w_re * h_re - w_im * h_im + p_re
            h_scr[pl.ds(base, sub), col + half:col + 2 * half] = w_re * h_im + w_im * h_re + p_im
        b_re, b_im = tbl["block_w"]
        last = tbl["last"]
        t_re, t_im = rep(x_re[last:last + 1, :]), rep(x_im[last:last + 1, :])
        return b_re * h_re - b_im * h_im + t_re, b_re * h_im + b_im * h_re + t_im

    def make_step(first_row, n_rows, keep):
        def step(m, carry):
            out = []
            for b in range(batch):
                hf, hb = carry[2 * b], carry[2 * b + 1]
                lo = first_row + b * n_rows
                hf = scan_block(hf, tbl_f, lo + m * sub, 0, keep)
                hb = scan_block(hb, tbl_b, lo + n_rows - sub - m * sub, 2 * half, keep)
                out += [hf, hb]
            return tuple(out)
        return step

    carry = lax.fori_loop(0, n_ctx // sub, make_step(ctx0, n_ctx, False), ((zero, zero),) * (2 * batch))
    lax.fori_loop(0, n_lat // sub, make_step(0, n_lat, True), carry)

    rows = batch * n_lat
    u_lat = u[:rows]
    y_ref[0] = (jnp.dot(u_lat, t_ref[0], preferred_element_type=F32)
                + jnp.dot(h_scr[...].astype(BF16), wc_ref[0], preferred_element_type=F32)
                + d_ref[0] * u_lat.astype(F32))


def _s5_core(ug, toep, ws, wc, al, d_lanes, *, batch, n_lat, n_ctx):
    n_steps, chunks, width = ug.shape
    assert n_lat % SCAN_ROWS == 0 and n_ctx % SCAN_ROWS == 0 and SCAN_ROWS & (SCAN_ROWS - 1) == 0
    rows = batch * n_lat
    return pl.pallas_call(
        functools.partial(_s5_kernel, batch=batch, n_lat=n_lat, n_ctx=n_ctx),
        out_shape=jax.ShapeDtypeStruct((n_steps, rows, width), F32),
        grid=(n_steps,),
        in_specs=[
            pl.BlockSpec((1, chunks, width), lambda j: (j, 0, 0)),
            pl.BlockSpec((1,) + toep.shape[1:], lambda j: (j, 0, 0)),
            pl.BlockSpec((1,) + ws.shape[1:], lambda j: (j, 0, 0)),
            pl.BlockSpec((1,) + wc.shape[1:], lambda j: (j, 0, 0)),
            pl.BlockSpec((1,) + al.shape[1:], lambda j: (j, 0, 0)),
            pl.BlockSpec((1, 1, width), lambda j: (j, 0, 0)),
        ],
        out_specs=pl.BlockSpec((1, rows, width), lambda j: (j, 0, 0)),
        scratch_shapes=[pltpu.VMEM((chunks, 4 * GROUPS_PER_STEP * SSM_STATE), F32),
                        pltpu.VMEM((rows, 4 * GROUPS_PER_STEP * SSM_STATE), F32)],
        compiler_params=_params("parallel"),
        name="s5_chunked",
    )(ug, toep, ws, wc, al, d_lanes)


def _cmul(a_re, a_im, b_re, b_im):
    return a_re * b_re - a_im * b_im, a_re * b_im + a_im * b_re


def _zoh(a_re, a_im, log_dt):
    dt = jnp.exp(log_dt)
    mag = jnp.exp(a_re * dt)
    ab_re, ab_im = mag * jnp.cos(a_im * dt), mag * jnp.sin(a_im * dt)
    den = a_re * a_re + a_im * a_im
    x_re, x_im = ab_re - 1.0, ab_im
    return ab_re, ab_im, (x_re * a_re + x_im * a_im) / den, (x_im * a_re - x_re * a_im) / den


def _powers(ab_re, ab_im, n):
    out = [(jnp.ones_like(ab_re), jnp.zeros_like(ab_im))]
    for _ in range(n):
        out.append(_cmul(out[-1][0], out[-1][1], ab_re, ab_im))
    return out


def _s5_op_kernel(row_ref, col_ref, bt_ref, ct_ref, rep_ref, t_ref, ws_ref, wc_ref, al_ref):
    L = SSM_CHUNK
    pair = ct_ref.shape[-1]
    assert L & (L - 1) == 0 and pair & (pair - 1) == 0
    hi = lax.Precision.HIGHEST
    lane_block = lax.broadcasted_iota(jnp.int32, (ct_ref.shape[-2], L * pair), 1) >> (pair.bit_length() - 1)

    def along_chunk(t):
        return jnp.dot(t, rep_ref[...], precision=hi, preferred_element_type=F32)

    inject, taps, read = [], [], []
    for d in range(2):
        ab_re, ab_im, f_re, f_im = _zoh(*(row_ref[0, 3 * d + r:3 * d + r + 1, :] for r in range(3)))
        pw_row = _powers(ab_re, ab_im, L)
        bb_re, bb_im = _cmul(f_re, f_im, bt_ref[0, d, 0], bt_ref[0, d, 1])
        order = range(L - 1, -1, -1) if d == 0 else range(L)
        inject.append([_cmul(pw_row[e][0], pw_row[e][1], bb_re, bb_im) for e in order])
        sq = [_zoh(*(col_ref[0, :, 3 * d + r:3 * d + r + 1] for r in range(3)))[:2]]
        while len(sq) < L.bit_length() - 1:
            sq.append(_cmul(*sq[-1], *sq[-1]))
        expo = lane_block if d == 0 else (L - 1) - lane_block
        low = (expo & 1) == 1
        p_re, p_im = jnp.where(low, sq[0][0], 1.0), jnp.where(low, sq[0][1], 0.0)
        for b in range(1, len(sq)):
            n_re, n_im = _cmul(p_re, p_im, *sq[b])
            bit = ((expo >> b) & 1) == 1
            p_re, p_im = jnp.where(bit, n_re, p_re), jnp.where(bit, n_im, p_im)
        ca_re, ca_im = _cmul(along_chunk(ct_ref[0, d, 0]), along_chunk(ct_ref[0, d, 1]), p_re, p_im)
        taps.append(jnp.dot(bb_re, ca_re, precision=hi, preferred_element_type=F32)
                    - jnp.dot(bb_im, ca_im, precision=hi, preferred_element_type=F32))
        rd_re, rd_im = _cmul(ca_re, ca_im, *sq[0])
        read.append((rd_re, -rd_im))
        al_ref[0, 2 * d:2 * d + 1, :] = pw_row[L][0]
        al_ref[0, 2 * d + 1:2 * d + 2, :] = pw_row[L][1]
    al_ref[0, 4:, :] = jnp.zeros((al_ref.shape[1] - 4, al_ref.shape[2]), F32)

    for s in range(L):
        ws_ref[0, s * pair:(s + 1) * pair, :] = jnp.concatenate(
            [inject[0][s][0], inject[0][s][1], inject[1][s][0], inject[1][s][1]], axis=1).astype(ws_ref.dtype)
    rows = read[0][0].shape[0]
    for q, blk in enumerate((read[0][0], read[0][1], read[1][0], read[1][1])):
        wc_ref[0, q * rows:(q + 1) * rows, :] = blk.astype(wc_ref.dtype)
    k_f, k_b = taps
    keep = (L - 1) * pair
    by_lag = jnp.concatenate([k_b[:, :keep], k_b[:, keep:] + k_f[:, :pair], k_f[:, pair:]], axis=1)
    for s in range(L):
        t_ref[0, s * pair:(s + 1) * pair, :] = by_lag[:, (L - 1 - s) * pair:(2 * L - 1 - s) * pair].astype(t_ref.dtype)


def _s5_operators(a_re, a_im, log_dt, b_re, b_im, c_re, c_im):
    n_g, n_p = a_re.shape[1], a_re.shape[2]
    gc = b_re.shape[-1]
    n_j, per = n_g // GROUPS_PER_STEP, GROUPS_PER_STEP
    states, pair, width = per * n_p, per * gc, SSM_CHUNK * per * gc
    lam = jnp.stack([a_re, a_im, jnp.broadcast_to(log_dt[..., None], a_re.shape)], axis=1)
    lam = jnp.transpose(lam.reshape(6, n_j, states), (1, 0, 2))
    lam = jnp.pad(lam, ((0, 0), (0, SUBLANES - lam.shape[1]), (0, 0)))
    eye = jnp.eye(per, dtype=F32)
    bt = jnp.stack([b_re, b_im], axis=1).reshape(2, 2, n_j, per, n_p, gc)
    bt = jnp.einsum('drjgpc,gh->jdrgchp', bt, eye).reshape(n_j, 2, 2, pair, states)
    ct = jnp.stack([c_re, c_im], axis=1).reshape(2, 2, n_j, per, gc, n_p)
    ct = jnp.einsum('drjgcp,gh->jdrgphc', ct, eye).reshape(n_j, 2, 2, states, pair)
    rep = jnp.tile(jnp.eye(pair, dtype=F32), (1, SSM_CHUNK))
    mat = jax.ShapeDtypeStruct((n_j, width, width), BF16)
    return pl.pallas_call(
        _s5_op_kernel,
        out_shape=(mat, mat, mat, jax.ShapeDtypeStruct((n_j, SUBLANES, states), F32)),
        grid=(n_j,),
        in_specs=[
            pl.BlockSpec((1, SUBLANES, states), lambda j: (j, 0, 0)),
            pl.BlockSpec((1, states, SUBLANES), lambda j: (j, 0, 0)),
            pl.BlockSpec((1, 2, 2, pair, states), lambda j: (j, 0, 0, 0, 0)),
            pl.BlockSpec((1, 2, 2, states, pair), lambda j: (j, 0, 0, 0, 0)),
            pl.BlockSpec((pair, width), lambda j: (0, 0)),
        ],
        out_specs=(pl.BlockSpec((1, width, width), lambda j: (j, 0, 0)),) * 3
        + (pl.BlockSpec((1, SUBLANES, states), lambda j: (j, 0, 0)),),
        compiler_params=_params("parallel"),
        name="s5_operators",
    )(lam, jnp.transpose(lam, (0, 2, 1)), bt, ct, rep)


def _ssm_out_kernel(y_ref, x_ref, mod_ref, g_ref, w_ref, o_ref, slab_scr, act_even, act_odd, out_scr, *, mi, gi):
    i = pl.program_id(0)

    @pl.when(i == 0)
    def _():
        act_odd[...] = jnp.zeros_like(act_odd)

    @pl.when(i % 2 == 0)
    def _():
        _ssm_out_step(y_ref, x_ref, mod_ref, g_ref, w_ref, o_ref, slab_scr, act_even, act_odd, out_scr, mi=mi, gi=gi)

    @pl.when(i % 2 == 1)
    def _():
        _ssm_out_step(y_ref, x_ref, mod_ref, g_ref, w_ref, o_ref, slab_scr, act_odd, act_even, out_scr, mi=mi, gi=gi)


def _ssm_out_step(y_ref, x_ref, mod_ref, g_ref, w_ref, o_ref, slab_scr, act_new, act_old, out_scr, *, mi, gi):
    n_slab, tm, lanes = slab_scr.shape
    pair = GROUPS_PER_STEP * SSM_GROUP_CH
    per_slab = lanes // pair
    for t in range(SSM_CHUNK):
        for k in range(n_slab):
            slab_scr[k, pl.ds(t, tm // SSM_CHUNK, stride=SSM_CHUNK), :] = jnp.concatenate(
                [y_ref[k * per_slab + jj, :, t * pair:(t + 1) * pair] for jj in range(per_slab)], axis=-1)
    for k in range(n_slab):
        act_new[:, k * lanes:(k + 1) * lanes] = _gelu_tanh(slab_scr[k]).astype(BF16)

    n_c, _, tn = out_scr.shape
    d = n_c * tn
    a = act_old[...]
    ssq = jnp.zeros((tm, 1), F32)
    for cc in range(n_c):
        val = jnp.dot(a, w_ref[:, cc * tn:(cc + 1) * tn], preferred_element_type=F32)
        gate = jnp.dot(a, w_ref[:, d + cc * tn:d + (cc + 1) * tn], preferred_element_type=F32)
        blk = val * _sigmoid(gate)
        out_scr[cc] = blk
        ssq = ssq + jnp.sum(blk * blk, axis=-1, keepdims=True)
    inv = lax.rsqrt(ssq / d + EPS)
    for cc in range(n_c):
        sl = slice(cc * tn, (cc + 1) * tn)
        o_ref[:, sl] = x_ref[:, sl] + mod_ref[0, mi:mi + 1, sl] * (out_scr[cc] * inv * g_ref[gi:gi + 1, sl])


def _ssm_out(yg, xs, mod, g, w_glu, *, mi, gi, n_rows, seg_of_row):
    d = xs.shape[1]
    tm, tn = GLU_TOKEN_TILE, GLU_TILE
    n_c = d // tn
    n_pairs, _, width = yg.shape
    n_tiles = n_rows // tm

    def prev(i):
        return jnp.maximum(i - 1, 0)

    return pl.pallas_call(
        functools.partial(_ssm_out_kernel, mi=mi, gi=gi),
        out_shape=jax.ShapeDtypeStruct((n_rows, d), F32),
        grid=(n_tiles + 1,),
        in_specs=[
            pl.BlockSpec((n_pairs, tm // SSM_CHUNK, width), lambda i: (0, jnp.minimum(i, n_tiles - 1), 0)),
            pl.BlockSpec((tm, d), lambda i: (prev(i), 0)),
            pl.BlockSpec((1, MOD_ROWS, d), lambda i: (seg_of_row(prev(i) * tm), 0, 0)),
            pl.BlockSpec((SUBLANES, d), lambda i: (0, 0)),
            pl.BlockSpec(w_glu.shape, lambda i: (0, 0), pipeline_mode=pl.Buffered(1)),
        ],
        out_specs=pl.BlockSpec((tm, d), lambda i: (prev(i), 0)),
        scratch_shapes=[pltpu.VMEM((d // LANES, tm, LANES), F32), pltpu.VMEM((tm, d), BF16),
                        pltpu.VMEM((tm, d), BF16), pltpu.VMEM((n_c, tm, tn), F32)],
        compiler_params=_params("arbitrary"),
        name="ssm_glu_out",
    )(yg, xs, mod, g, w_glu)


def _rope_tables(seq_len, extra_rows):
    rows = seq_len // GRID_W
    axis_dim = HEAD_DIM // 2
    inv_freq = ROPE_BASE ** (-jnp.arange(0, axis_dim, 2, dtype=F32) / axis_dim)
    ang_r = jnp.arange(rows, dtype=F32)[:, None] * inv_freq
    ang_c = jnp.arange(GRID_W, dtype=F32)[:, None] * inv_freq
    by_row = lambda t: jnp.repeat(t, GRID_W, axis=0)
    by_col = lambda t: jnp.tile(t, (rows, 1))
    cos = jnp.concatenate([by_row(jnp.cos(ang_r))] * 2 + [by_col(jnp.cos(ang_c))] * 2, axis=-1)
    sin = jnp.concatenate([by_row(-jnp.sin(ang_r)), by_row(jnp.sin(ang_r)),
                           by_col(-jnp.sin(ang_c)), by_col(jnp.sin(ang_c))], axis=-1)
    cos = jnp.concatenate([cos, jnp.ones((extra_rows, HEAD_DIM), F32)], axis=0)
    sin = jnp.concatenate([sin, jnp.zeros((extra_rows, HEAD_DIM), F32)], axis=0)
    return cos, sin


def kernel(x, c, ctx, c_ctx, ada_w, ada_b, norm_g, ffn_w_in, ffn_w_out, attn_w_in, attn_w_out, attn_sink,
           ssm_w_in, ssm_a_re, ssm_a_im, ssm_log_dt, ssm_b_re, ssm_b_im, ssm_c_re, ssm_c_im, ssm_d, ssm_w_glu):
    batch, seq_len, d = x.shape
    c_len = ctx.shape[1]
    depth = ada_w.shape[0]
    tm = TOKEN_TILE
    n_lat_rows, n_ctx_rows = batch * seq_len, batch * c_len
    assert seq_len % tm == 0 and n_ctx_rows % tm == 0 and n_lat_rows % c_len == 0
    assert seq_len % ATT_BLOCK == 0 and c_len % ATT_BLOCK == 0 and seq_len % GRID_W == 0
    tiles_per_batch = seq_len // tm
    lat_tiles = n_lat_rows // tm
    all_tiles = lat_tiles + n_ctx_rows // tm

    def seg_of(i):
        return jnp.minimum(i // tiles_per_batch, batch)

    def pos_of(i):
        return jnp.where(i < lat_tiles, i % tiles_per_batch, tiles_per_batch + (i - lat_tiles))

    def seg_of_row(r):
        return jnp.minimum(r // seq_len, batch)

    xs, tail = x.reshape(n_lat_rows, d), ctx.reshape(n_ctx_rows, d)
    w_ff_in, w_ff_out = _ffn_w_in_chunks(ffn_w_in), ffn_w_out.astype(BF16)

    n_cond = batch + 1
    cvec = jnp.concatenate([c, c_ctx[None], jnp.zeros((SUBLANES - n_cond, d), F32)], axis=0)
    mods = _modulation(cvec.T, ada_w, ada_b, n_cond)
    mods = mods[:, :n_cond].reshape(depth, n_cond, N_MOD, d)
    mods = jnp.pad(mods, ((0, 0), (0, 0), (0, MOD_ROWS - N_MOD), (0, 0)))
    gains = jnp.pad(norm_g, ((0, 0), (0, SUBLANES - norm_g.shape[1]), (0, 0)))

    cos, sin = _rope_tables(seq_len, n_ctx_rows)

    for i in range(depth):
        need_ctx = i < depth - 1
        mod, g = mods[i], gains[i]
        j = i // 2
        xs = _ffn(xs, mod, g, w_ff_in, w_ff_out, (i, 0), mi=0, gi=0, n_tiles=all_tiles, seg_of=seg_of, tail=tail)
        tail = None
        post_tiles = all_tiles if need_ctx else lat_tiles
        if i % 2 == 0:
            q, k, v = _qkv(xs, mod, g, attn_w_in[j].astype(BF16), cos, sin,
                           mi=3, gi=2, n_tiles=all_tiles, seg_of=seg_of, pos_of=pos_of)
            o = _attention(q, k, v, attn_sink[j], batch=batch, seq_len=seq_len, c_len=c_len)
            xs = _proj_out(o, xs, mod, g, attn_w_out[j].astype(BF16),
                           mi=5, gi=3, n_tiles=post_tiles, seg_of=seg_of)
        else:
            if need_ctx:
                raise NotImplementedError("context output of the S5 mixer is only needed when another layer follows")
            ug = _proj_in(xs, mod, g, ssm_w_in[j].astype(BF16), mi=3, gi=2, n_tiles=all_tiles, seg_of=seg_of)
            L = SSM_CHUNK
            pair = GROUPS_PER_STEP * SSM_GROUP_CH
            toep, ws, wc, al = _s5_operators(ssm_a_re[j], ssm_a_im[j], ssm_log_dt[j], ssm_b_re[j], ssm_b_im[j],
                                             ssm_c_re[j], ssm_c_im[j])
            d_lanes = jnp.tile(ssm_d[j].reshape(d // pair, 1, pair), (1, 1, L))
            yg = _s5_core(ug, toep, ws, wc, al, d_lanes, batch=batch, n_lat=seq_len // L, n_ctx=c_len // L)
            xs = _ssm_out(yg, xs, mod, g, ssm_w_glu[j].astype(BF16),
                          mi=5, gi=3, n_rows=post_tiles * tm, seg_of_row=seg_of_row)
        xs = _ffn(xs, mod, g, w_ff_in, w_ff_out, (i, 1), mi=6, gi=4, n_tiles=post_tiles, seg_of=seg_of)
    return xs[:n_lat_rows].reshape(batch, seq_len, d)
```

```python
import functools
import math

import jax
import jax.numpy as jnp
from jax import lax
from jax.experimental import pallas as pl
from jax.experimental.pallas import tpu as pltpu

EPS = 1e-6
NEG_INF = -1e30
N_MOD = 9
GRID_W = 64
N_HEADS = 16
N_KV_HEADS = 4
HEAD_DIM = 128
GQA_GROUP = N_HEADS // N_KV_HEADS
ATT_BLOCK = 128
ROPE_BASE = 10000.0
ROPE_QUARTER = HEAD_DIM // 4
SSM_GROUP_CH = 16
SSM_STATE = 64
SSM_CHUNK = 16
GROUPS_PER_STEP = 2
SCAN_ROWS = 8

TOKEN_TILE = 512
FF_TILE = 512
GLU_TILE = 512
GLU_TOKEN_TILE = 256
MOD_TILE = 1024
MOD_ROWS = 16
VMEM_LIMIT = 56 * 1024 * 1024
LANES = 128
SUBLANES = 8

BF16 = jnp.bfloat16
F32 = jnp.float32


def _params(*sem):
    return pltpu.CompilerParams(dimension_semantics=sem, vmem_limit_bytes=VMEM_LIMIT)


def _rms(t, g):
    return t * lax.rsqrt(jnp.mean(t * t, axis=-1, keepdims=True) + EPS) * g


def _sigmoid(t):
    return 0.5 * jnp.tanh(0.5 * t) + 0.5


def _gelu_tanh(t):
    return 0.5 * t * (1.0 + jnp.tanh(math.sqrt(2.0 / math.pi) * (t + 0.044715 * (t * t * t))))


def _mod_kernel(ct_ref, w_ref, b_ref, o_ref, *, n_rows):
    w = w_ref[0]
    for r in range(n_rows):
        col = ct_ref[:, r:r + 1]
        col = col * _sigmoid(col)
        o_ref[0, r:r + 1, :] = jnp.sum(w * col, axis=0, keepdims=True) + b_ref[0]
    o_ref[0, n_rows:, :] = jnp.zeros((o_ref.shape[1] - n_rows, o_ref.shape[2]), F32)


def _modulation(cvec_t, ada_w, ada_b, n_rows):
    depth, d, nd = ada_w.shape
    tn = MOD_TILE
    return pl.pallas_call(
        functools.partial(_mod_kernel, n_rows=n_rows),
        out_shape=jax.ShapeDtypeStruct((depth, SUBLANES, nd), F32),
        grid=(depth, nd // tn),
        in_specs=[
            pl.BlockSpec((d, SUBLANES), lambda l, j: (0, 0)),
            pl.BlockSpec((1, d, tn), lambda l, j: (l, 0, j)),
            pl.BlockSpec((1, 1, tn), lambda l, j: (l, 0, j)),
        ],
        out_specs=pl.BlockSpec((1, SUBLANES, tn), lambda l, j: (l, 0, j)),
        compiler_params=_params("arbitrary", "arbitrary"),
        name="adaln_mod",
    )(cvec_t, ada_w, ada_b.reshape(depth, 1, nd))


def _ffn_w_cast_kernel(w_ref, o_ref):
    o_ref[...] = w_ref[...].astype(o_ref.dtype)


def _ffn_w_in_chunks(w):
    n_l, n_h, d, f2 = w.shape
    tf = FF_TILE
    nf = f2 // 2 // tf
    return pl.pallas_call(
        _ffn_w_cast_kernel,
        out_shape=jax.ShapeDtypeStruct((n_l, n_h, nf, 2, d, tf), BF16),
        grid=(n_l, n_h, nf, 2),
        in_specs=[pl.BlockSpec((None, None, d, tf), lambda l, h, f, gu: (l, h, 0, gu * nf + f))],
        out_specs=pl.BlockSpec((None, None, None, None, d, tf), lambda l, h, f, gu: (l, h, f, gu, 0, 0)),
        compiler_params=_params("parallel", "parallel", "parallel", "parallel"),
        name="ffn_w_cast",
    )(w)


def _ffn_kernel(x_ref, tail_ref, mod_ref, g_ref, wgu_ref, wo_ref, o_ref, h_scr, acc_scr,
                *, mi, gi, head_tiles, n_ff):
    def read_x():
        if head_tiles is None:
            return x_ref[...]
        return jnp.where(pl.program_id(0) < head_tiles, x_ref[...], tail_ref[...])

    def step(first, last):
        if first:
            xv = read_x()
            inv = lax.rsqrt(jnp.mean(xv * xv, axis=-1, keepdims=True) + EPS)
            h = (xv * inv) * (g_ref[gi:gi + 1, :] * (1.0 + mod_ref[0, mi + 1:mi + 2, :])) + mod_ref[0, mi:mi + 1, :]
            h = h.astype(BF16)
            if not last:
                h_scr[...] = h
        else:
            h = h_scr[...]
        gate = jnp.dot(h, wgu_ref[0], preferred_element_type=F32)
        up = jnp.dot(h, wgu_ref[1], preferred_element_type=F32)
        act = (gate * _sigmoid(gate) * up).astype(BF16)
        out = jnp.dot(act, wo_ref[...], preferred_element_type=F32)
        if not first:
            out = acc_scr[...] + out
        if last:
            inv = lax.rsqrt(jnp.mean(out * out, axis=-1, keepdims=True) + EPS)
            o_ref[...] = read_x() + (out * inv) * (0.5 * mod_ref[0, mi + 2:mi + 3, :] * g_ref[gi + 1:gi + 2, :])
        else:
            acc_scr[...] = out

    f = pl.program_id(1)
    if n_ff == 1:
        step(True, True)
    else:
        pl.when(f == 0)(functools.partial(step, True, False))
        if n_ff > 2:
            pl.when((f > 0) & (f < n_ff - 1))(functools.partial(step, False, False))
        pl.when(f == n_ff - 1)(functools.partial(step, False, True))


def _ffn(xs, mod, g, w_in, w_out, which, *, mi, gi, n_tiles, seg_of, tail=None):
    d = xs.shape[1]
    ff = w_out.shape[2]
    tm, tf = TOKEN_TILE, FF_TILE
    nf = ff // tf
    layer, half = which
    if tail is None:
        head_tiles = None
        tokens = [xs, xs]
        token_specs = [pl.BlockSpec((tm, d), lambda i, f: (i, 0)), pl.BlockSpec((SUBLANES, d), lambda i, f: (0, 0))]
    else:
        head_tiles = xs.shape[0] // tm
        tokens = [xs, tail]
        token_specs = [pl.BlockSpec((tm, d), lambda i, f: (jnp.minimum(i, head_tiles - 1), 0)),
                       pl.BlockSpec((tm, d), lambda i, f: (jnp.maximum(i - head_tiles, 0), 0),
                                    pipeline_mode=pl.Buffered(1))]
    return pl.pallas_call(
        functools.partial(_ffn_kernel, mi=mi, gi=gi, head_tiles=head_tiles, n_ff=nf),
        out_shape=jax.ShapeDtypeStruct((n_tiles * tm, d), F32),
        grid=(n_tiles, nf),
        in_specs=token_specs + [
            pl.BlockSpec((1, MOD_ROWS, d), lambda i, f: (seg_of(i), 0, 0)),
            pl.BlockSpec((SUBLANES, d), lambda i, f: (0, 0)),
            pl.BlockSpec((None, None, None, 2, d, tf), lambda i, f: (layer, half, f, 0, 0, 0)),
            pl.BlockSpec((None, None, tf, d), lambda i, f: (layer, half, f, 0)),
        ],
        out_specs=pl.BlockSpec((tm, d), lambda i, f: (i, 0)),
        scratch_shapes=[pltpu.VMEM((tm, d), BF16), pltpu.VMEM((tm, d), F32)],
        compiler_params=_params("parallel", "arbitrary"),
        name="ffn_halfstep",
    )(*tokens, mod, g, w_in, w_out)


def _rope(t, cos, sin, even_quarter):
    partner = jnp.where(even_quarter, pltpu.roll(t, HEAD_DIM - ROPE_QUARTER, 1), pltpu.roll(t, ROPE_QUARTER, 1))
    return t * cos + partner * sin


def _qkv_kernel(x_ref, mod_ref, g_ref, w_ref, cos_ref, sin_ref, q_ref, k_ref, v_ref, *, mi, gi):
    h = _rms(x_ref[...], g_ref[gi:gi + 1, :]) * (1.0 + mod_ref[0, mi + 1:mi + 2, :]) + mod_ref[0, mi:mi + 1, :]
    qkv = jnp.dot(h.astype(BF16), w_ref[...], preferred_element_type=F32)
    cos, sin = cos_ref[...], sin_ref[...]
    lane = lax.broadcasted_iota(jnp.int32, cos.shape, 1)
    even_quarter = (lane & ROPE_QUARTER) == 0
    scale = HEAD_DIM ** -0.5
    nq, nk = N_HEADS * HEAD_DIM, N_KV_HEADS * HEAD_DIM
    for hh in range(N_HEADS):
        sl = slice(hh * HEAD_DIM, (hh + 1) * HEAD_DIM)
        q_ref[:, sl] = (_rope(qkv[:, sl], cos, sin, even_quarter) * scale).astype(BF16)
    for hh in range(N_KV_HEADS):
        sl = slice(hh * HEAD_DIM, (hh + 1) * HEAD_DIM)
        src = slice(nq + hh * HEAD_DIM, nq + (hh + 1) * HEAD_DIM)
        k_ref[:, sl] = _rope(qkv[:, src], cos, sin, even_quarter).astype(BF16)
    v_ref[...] = qkv[:, nq + nk:].astype(BF16)


def _qkv(xs, mod, g, w, cos, sin, *, mi, gi, n_tiles, seg_of, pos_of):
    d = xs.shape[1]
    tm = TOKEN_TILE
    nq, nk = N_HEADS * HEAD_DIM, N_KV_HEADS * HEAD_DIM
    rows = n_tiles * tm
    return pl.pallas_call(
        functools.partial(_qkv_kernel, mi=mi, gi=gi),
        out_shape=(jax.ShapeDtypeStruct((rows, nq), BF16), jax.ShapeDtypeStruct((rows, nk), BF16),
                   jax.ShapeDtypeStruct((rows, nk), BF16)),
        grid=(n_tiles,),
        in_specs=[
            pl.BlockSpec((tm, d), lambda i: (i, 0)),
            pl.BlockSpec((1, MOD_ROWS, d), lambda i: (seg_of(i), 0, 0)),
            pl.BlockSpec((SUBLANES, d), lambda i: (0, 0)),
            pl.BlockSpec((d, nq + 2 * nk), lambda i: (0, 0), pipeline_mode=pl.Buffered(1)),
            pl.BlockSpec((tm, HEAD_DIM), lambda i: (pos_of(i), 0)),
            pl.BlockSpec((tm, HEAD_DIM), lambda i: (pos_of(i), 0)),
        ],
        out_specs=(pl.BlockSpec((tm, nq), lambda i: (i, 0)), pl.BlockSpec((tm, nk), lambda i: (i, 0)),
                   pl.BlockSpec((tm, nk), lambda i: (i, 0))),
        compiler_params=_params("parallel"),
        name="attn_qkv_rope",
    )(xs, mod, g, w, cos, sin)


def _proj_in_kernel(x_ref, mod_ref, g_ref, w_ref, o_ref, slab_even, slab_odd, *, mi, gi):
    i = pl.program_id(0)

    @pl.when(i == 0)
    def _():
        slab_odd[...] = jnp.zeros_like(slab_odd)

    @pl.when(i % 2 == 0)
    def _():
        _proj_in_step(x_ref, mod_ref, g_ref, w_ref, o_ref, slab_even, slab_odd, mi=mi, gi=gi)

    @pl.when(i % 2 == 1)
    def _():
        _proj_in_step(x_ref, mod_ref, g_ref, w_ref, o_ref, slab_odd, slab_even, mi=mi, gi=gi)


def _proj_in_step(x_ref, mod_ref, g_ref, w_ref, o_ref, slab_new, slab_old, *, mi, gi):
    h = _rms(x_ref[...], g_ref[gi:gi + 1, :]) * (1.0 + mod_ref[0, mi + 1:mi + 2, :]) + mod_ref[0, mi:mi + 1, :]
    u = jnp.dot(h.astype(BF16), w_ref[...], preferred_element_type=F32)
    n_slab, tm, lanes = slab_new.shape
    pair = GROUPS_PER_STEP * SSM_GROUP_CH
    per_slab = lanes // pair
    for k in range(n_slab):
        slab_new[k] = u[:, k * lanes:(k + 1) * lanes]
    for t in range(SSM_CHUNK):
        for k in range(n_slab):
            rows = slab_old[k, pl.ds(t, tm // SSM_CHUNK, stride=SSM_CHUNK), :]
            for jj in range(per_slab):
                o_ref[k * per_slab + jj, :, t * pair:(t + 1) * pair] = rows[:, jj * pair:(jj + 1) * pair].astype(o_ref.dtype)


def _proj_in(xs, mod, g, w, *, mi, gi, n_tiles, seg_of):
    d = xs.shape[1]
    tm = TOKEN_TILE
    pair = GROUPS_PER_STEP * SSM_GROUP_CH
    n_pairs = w.shape[1] // pair
    return pl.pallas_call(
        functools.partial(_proj_in_kernel, mi=mi, gi=gi),
        out_shape=jax.ShapeDtypeStruct((n_pairs, n_tiles * tm // SSM_CHUNK, SSM_CHUNK * pair), BF16),
        grid=(n_tiles + 1,),
        in_specs=[
            pl.BlockSpec((tm, d), lambda i: (jnp.minimum(i, n_tiles - 1), 0)),
            pl.BlockSpec((1, MOD_ROWS, d), lambda i: (seg_of(jnp.minimum(i, n_tiles - 1)), 0, 0)),
            pl.BlockSpec((SUBLANES, d), lambda i: (0, 0)),
            pl.BlockSpec(w.shape, lambda i: (0, 0), pipeline_mode=pl.Buffered(1)),
        ],
        out_specs=pl.BlockSpec((n_pairs, tm // SSM_CHUNK, SSM_CHUNK * pair), lambda i: (0, jnp.maximum(i - 1, 0), 0)),
        scratch_shapes=[pltpu.VMEM((w.shape[1] // LANES, tm, LANES), F32)] * 2,
        compiler_params=_params("arbitrary"),
        name="ssm_in_proj",
    )(xs, mod, g, w)


def _proj_out_kernel(a_ref, x_ref, mod_ref, g_ref, w_ref, o_ref, *, mi, gi):
    out = jnp.dot(a_ref[...], w_ref[...], preferred_element_type=F32)
    o_ref[...] = x_ref[...] + mod_ref[0, mi:mi + 1, :] * _rms(out, g_ref[gi:gi + 1, :])


def _proj_out(a, xs, mod, g, w, *, mi, gi, n_tiles, seg_of):
    d = xs.shape[1]
    tm = TOKEN_TILE
    return pl.pallas_call(
        functools.partial(_proj_out_kernel, mi=mi, gi=gi),
        out_shape=jax.ShapeDtypeStruct((n_tiles * tm, d), F32),
        grid=(n_tiles,),
        in_specs=[
            pl.BlockSpec((tm, a.shape[1]), lambda i: (i, 0)),
            pl.BlockSpec((tm, d), lambda i: (i, 0)),
            pl.BlockSpec((1, MOD_ROWS, d), lambda i: (seg_of(i), 0, 0)),
            pl.BlockSpec((SUBLANES, d), lambda i: (0, 0)),
            pl.BlockSpec(w.shape, lambda i: (0, 0), pipeline_mode=pl.Buffered(1)),
        ],
        out_specs=pl.BlockSpec((tm, d), lambda i: (i, 0)),
        compiler_params=_params("parallel"),
        name="attn_out_proj",
    )(a, xs, mod, g, w)


def _attn_kernel(sink_ref, q_ref, kp_ref, ko_ref, kn_ref, vp_ref, vo_ref, vn_ref, kc_ref, vc_ref, bias_ref, o_ref):
    blk = ATT_BLOCK
    bias = jnp.concatenate([bias_ref[...]] * GQA_GROUP, axis=0)
    row = lax.broadcasted_iota(jnp.int32, (GQA_GROUP * blk, 1), 0)
    for h in range(N_KV_HEADS):
        hs = slice(h * HEAD_DIM, (h + 1) * HEAD_DIM)
        qs = jnp.concatenate(
            [q_ref[:, (h * GQA_GROUP + gq) * HEAD_DIM:(h * GQA_GROUP + gq + 1) * HEAD_DIM] for gq in range(GQA_GROUP)],
            axis=0)
        sink = jnp.full((GQA_GROUP * blk, 1), sink_ref[h * GQA_GROUP], F32)
        for gq in range(1, GQA_GROUP):
            sink = jnp.where(row >= gq * blk, sink_ref[h * GQA_GROUP + gq], sink)
        scores, col = [], 0
        for k_ref in (kp_ref, ko_ref, kn_ref, kc_ref):
            s = lax.dot_general(qs, k_ref[:, hs], (((1,), (1,)), ((), ())), preferred_element_type=F32)
            if k_ref is not kc_ref:
                s = s + bias[:, col:col + k_ref.shape[0]]
            col += k_ref.shape[0]
            scores.append(s)

        def lane_slabs(t):
            return [t[:, c:c + LANES] for c in range(0, t.shape[1], LANES)]

        m = functools.reduce(jnp.maximum, [sl for s in scores for sl in lane_slabs(s)])
        m = jnp.maximum(jnp.max(m, axis=-1, keepdims=True), sink)
        probs = [jnp.exp(s - m) for s in scores]
        denom = jnp.sum(functools.reduce(jnp.add, [sl for p in probs for sl in lane_slabs(p)]), axis=-1, keepdims=True)
        denom = denom + jnp.exp(sink - m)
        o = jnp.zeros((GQA_GROUP * blk, HEAD_DIM), F32)
        for p, v_ref in zip(probs, (vp_ref, vo_ref, vn_ref, vc_ref)):
            o = o + jnp.dot(p.astype(BF16), v_ref[:, hs], preferred_element_type=F32)
        o = o / denom
        for gq in range(GQA_GROUP):
            col = (h * GQA_GROUP + gq) * HEAD_DIM
            o_ref[:, col:col + HEAD_DIM] = o[gq * blk:(gq + 1) * blk, :].astype(o_ref.dtype)


def _attention(q, k, v, sink, *, batch, seq_len, c_len):
    blk = ATT_BLOCK
    n_blk, c_blk = seq_len // blk, c_len // blk
    nq, nk = q.shape[1], k.shape[1]
    lat_blocks = batch * n_blk
    ctx_block0 = batch * seq_len // c_len

    def q_idx(b, n):
        return jnp.where(n < n_blk, b * n_blk + n, lat_blocks + b * c_blk + (n - n_blk))

    def band_idx(off):
        def idx(b, n, s):
            nn = jnp.clip(n + off, 0, n_blk - 1)
            return (b * n_blk + nn, 0)
        return idx

    assert n_blk >= 2
    n_keys = 3 * blk + c_len
    qi = jnp.arange(blk)[:, None]
    kj = jnp.arange(n_keys)[None, :]
    in_band = (kj - qi >= 0) & (kj - qi <= 2 * blk)
    cases = [in_band & (kj >= blk), in_band, in_band & (kj < 2 * blk), jnp.zeros_like(in_band)]
    bias = jnp.stack([jnp.where((kj < 3 * blk) & ~case, NEG_INF, 0.0).astype(F32) for case in cases])

    def bias_idx(b, n, s):
        return (jnp.where(n == 0, 0, jnp.where(n < n_blk - 1, 1, jnp.where(n == n_blk - 1, 2, 3))), 0, 0)

    kv_spec = [pl.BlockSpec((blk, nk), band_idx(off)) for off in (-1, 0, 1)]
    ctx_spec = pl.BlockSpec((c_len, nk), lambda b, n, s: (ctx_block0 + b, 0))
    grid_spec = pltpu.PrefetchScalarGridSpec(
        num_scalar_prefetch=1,
        grid=(batch, n_blk + c_blk),
        in_specs=[pl.BlockSpec((blk, nq), lambda b, n, s: (q_idx(b, n), 0))] + kv_spec + kv_spec + [ctx_spec, ctx_spec]
        + [pl.BlockSpec((None, blk, n_keys), bias_idx)],
        out_specs=pl.BlockSpec((blk, nq), lambda b, n, s: (q_idx(b, n), 0)),
    )
    return pl.pallas_call(
        _attn_kernel,
        out_shape=jax.ShapeDtypeStruct(q.shape, BF16),
        grid_spec=grid_spec,
        compiler_params=_params("parallel", "arbitrary"),
        name="window_gqa",
    )(sink, q, k, k, k, v, v, v, k, v, bias)


def _s5_kernel(u_ref, t_ref, ws_ref, wc_ref, al_ref, d_ref, y_ref, s_scr, h_scr, *, batch, n_lat, n_ctx):
    half = GROUPS_PER_STEP * SSM_STATE
    u = u_ref[0]
    s_scr[...] = jnp.dot(u, ws_ref[0], preferred_element_type=F32)
    sub = SCAN_ROWS
    ctx0 = batch * n_lat
    zero = jnp.zeros((sub, half), F32)
    sub_row = lax.broadcasted_iota(jnp.int32, (sub, half), 0)

    def rep(v):
        return jnp.broadcast_to(v, (sub, half))

    def tables(row, fwd):
        pw = _powers(al_ref[0, row:row + 1, :], al_ref[0, row + 1:row + 2, :], sub)
        levels = []
        for k in (1 << b for b in range(sub.bit_length() - 1)):
            ok = (sub_row >= k) if fwd else (sub_row <= sub - 1 - k)
            levels.append((k if fwd else sub - k, jnp.where(ok, rep(pw[k][0]), 0.0), jnp.where(ok, rep(pw[k][1]), 0.0)))
        c_re, c_im = zero, zero
        for r in range(sub):
            e = r if fwd else sub - 1 - r
            c_re = jnp.where(sub_row == r, rep(pw[e][0]), c_re)
            c_im = jnp.where(sub_row == r, rep(pw[e][1]), c_im)
        inner = (sub_row >= 1) if fwd else (sub_row <= sub - 2)
        return dict(levels=levels, carry_w=(c_re, c_im), block_w=(rep(pw[sub][0]), rep(pw[sub][1])),
                    shift=1 if fwd else sub - 1, inner=inner, last=sub - 1 if fwd else 0)

    tbl_f, tbl_b = tables(0, True), tables(2, False)

    def scan_block(h, tbl, base, col, keep):
        h_re, h_im = h
        base = pl.multiple_of(base, sub)
        x_re = s_scr[pl.ds(base, sub), col:col + half]
        x_im = s_scr[pl.ds(base, sub), col + half:col + 2 * half]
        for shift, c_re, c_im in tbl["levels"]:
            r_re, r_im = pltpu.roll(x_re, shift, 0), pltpu.roll(x_im, shift, 0)
            x_re, x_im = x_re + c_re * r_re - c_im * r_im, x_im + c_re * r_im + c_im * r_re
        if keep:
            w_re, w_im = tbl["carry_w"]
            p_re = jnp.where(tbl["inner"], pltpu.roll(x_re, tbl["shift"], 0), 0.0)
            p_im = jnp.where(tbl["inner"], pltpu.roll(x_im, tbl["shift"], 0), 0.0)
            h_scr[pl.ds(base, sub), col:col + half] = w_re * h_re - w_im * h_im + p_re
            h_scr[pl.ds(base, sub), col + half:col + 2 * half] = w_re * h_im + w_im * h_re + p_im
        b_re, b_im = tbl["block_w"]
        last = tbl["last"]
        t_re, t_im = rep(x_re[last:last + 1, :]), rep(x_im[last:last + 1, :])
        return b_re * h_re - b_im * h_im + t_re, b_re * h_im + b_im * h_re + t_im

    def make_step(first_row, n_rows, keep):
        def step(m, carry):
            out = []
            for b in range(batch):
                hf, hb = carry[2 * b], carry[2 * b + 1]
                lo = first_row + b * n_rows
                hf = scan_block(hf, tbl_f, lo + m * sub, 0, keep)
                hb = scan_block(hb, tbl_b, lo + n_rows - sub - m * sub, 2 * half, keep)
                out += [hf, hb]
            return tuple(out)
        return step

    carry = lax.fori_loop(0, n_ctx // sub, make_step(ctx0, n_ctx, False), ((zero, zero),) * (2 * batch))
    lax.fori_loop(0, n_lat // sub, make_step(0, n_lat, True), carry)

    rows = batch * n_lat
    u_lat = u[:rows]
    y_ref[0] = (jnp.dot(u_lat, t_ref[0], preferred_element_type=F32)
                + jnp.dot(h_scr[...].astype(BF16), wc_ref[0], preferred_element_type=F32)
                + d_ref[0] * u_lat.astype(F32))


def _s5_core(ug, toep, ws, wc, al, d_lanes, *, batch, n_lat, n_ctx):
    n_steps, chunks, width = ug.shape
    assert n_lat % SCAN_ROWS == 0 and n_ctx % SCAN_ROWS == 0 and SCAN_ROWS & (SCAN_ROWS - 1) == 0
    rows = batch * n_lat
    return pl.pallas_call(
        functools.partial(_s5_kernel, batch=batch, n_lat=n_lat, n_ctx=n_ctx),
        out_shape=jax.ShapeDtypeStruct((n_steps, rows, width), F32),
        grid=(n_steps,),
        in_specs=[
            pl.BlockSpec((1, chunks, width), lambda j: (j, 0, 0)),
            pl.BlockSpec((1,) + toep.shape[1:], lambda j: (j, 0, 0)),
            pl.BlockSpec((1,) + ws.shape[1:], lambda j: (j, 0, 0)),
            pl.BlockSpec((1,) + wc.shape[1:], lambda j: (j, 0, 0)),
            pl.BlockSpec((1,) + al.shape[1:], lambda j: (j, 0, 0)),
            pl.BlockSpec((1, 1, width), lambda j: (j, 0, 0)),
        ],
        out_specs=pl.BlockSpec((1, rows, width), lambda j: (j, 0, 0)),
        scratch_shapes=[pltpu.VMEM((chunks, 4 * GROUPS_PER_STEP * SSM_STATE), F32),
                        pltpu.VMEM((rows, 4 * GROUPS_PER_STEP * SSM_STATE), F32)],
        compiler_params=_params("parallel"),
        name="s5_chunked",
    )(ug, toep, ws, wc, al, d_lanes)


def _cmul(a_re, a_im, b_re, b_im):
    return a_re * b_re - a_im * b_im, a_re * b_im + a_im * b_re


def _zoh(a_re, a_im, log_dt):
    dt = jnp.exp(log_dt)
    mag = jnp.exp(a_re * dt)
    ab_re, ab_im = mag * jnp.cos(a_im * dt), mag * jnp.sin(a_im * dt)
    den = a_re * a_re + a_im * a_im
    x_re, x_im = ab_re - 1.0, ab_im
    return ab_re, ab_im, (x_re * a_re + x_im * a_im) / den, (x_im * a_re - x_re * a_im) / den


def _powers(ab_re, ab_im, n):
    out = [(jnp.ones_like(ab_re), jnp.zeros_like(ab_im))]
    for _ in range(n):
        out.append(_cmul(out[-1][0], out[-1][1], ab_re, ab_im))
    return out


def _s5_op_kernel(row_ref, col_ref, bt_ref, ct_ref, rep_ref, t_ref, ws_ref, wc_ref, al_ref):
    L = SSM_CHUNK
    pair = ct_ref.shape[-1]
    assert L & (L - 1) == 0 and pair & (pair - 1) == 0
    hi = lax.Precision.HIGHEST
    lane_block = lax.broadcasted_iota(jnp.int32, (ct_ref.shape[-2], L * pair), 1) >> (pair.bit_length() - 1)

    def along_chunk(t):
        return jnp.dot(t, rep_ref[...], precision=hi, preferred_element_type=F32)

    inject, taps, read = [], [], []
    for d in range(2):
        ab_re, ab_im, f_re, f_im = _zoh(*(row_ref[0, 3 * d + r:3 * d + r + 1, :] for r in range(3)))
        pw_row = _powers(ab_re, ab_im, L)
        bb_re, bb_im = _cmul(f_re, f_im, bt_ref[0, d, 0], bt_ref[0, d, 1])
        order = range(L - 1, -1, -1) if d == 0 else range(L)
        inject.append([_cmul(pw_row[e][0], pw_row[e][1], bb_re, bb_im) for e in order])
        sq = [_zoh(*(col_ref[0, :, 3 * d + r:3 * d + r + 1] for r in range(3)))[:2]]
        while len(sq) < L.bit_length() - 1:
            sq.append(_cmul(*sq[-1], *sq[-1]))
        expo = lane_block if d == 0 else (L - 1) - lane_block
        low = (expo & 1) == 1
        p_re, p_im = jnp.where(low, sq[0][0], 1.0), jnp.where(low, sq[0][1], 0.0)
        for b in range(1, len(sq)):
            n_re, n_im = _cmul(p_re, p_im, *sq[b])
            bit = ((expo >> b) & 1) == 1
            p_re, p_im = jnp.where(bit, n_re, p_re), jnp.where(bit, n_im, p_im)
        ca_re, ca_im = _cmul(along_chunk(ct_ref[0, d, 0]), along_chunk(ct_ref[0, d, 1]), p_re, p_im)
        taps.append(jnp.dot(bb_re, ca_re, precision=hi, preferred_element_type=F32)
                    - jnp.dot(bb_im, ca_im, precision=hi, preferred_element_type=F32))
        rd_re, rd_im = _cmul(ca_re, ca_im, *sq[0])
        read.append((rd_re, -rd_im))
        al_ref[0, 2 * d:2 * d + 1, :] = pw_row[L][0]
        al_ref[0, 2 * d + 1:2 * d + 2, :] = pw_row[L][1]
    al_ref[0, 4:, :] = jnp.zeros((al_ref.shape[1] - 4, al_ref.shape[2]), F32)

    for s in range(L):
        ws_ref[0, s * pair:(s + 1) * pair, :] = jnp.concatenate(
            [inject[0][s][0], inject[0][s][1], inject[1][s][0], inject[1][s][1]], axis=1).astype(ws_ref.dtype)
    rows = read[0][0].shape[0]
    for q, blk in enumerate((read[0][0], read[0][1], read[1][0], read[1][1])):
        wc_ref[0, q * rows:(q + 1) * rows, :] = blk.astype(wc_ref.dtype)
    k_f, k_b = taps
    keep = (L - 1) * pair
    by_lag = jnp.concatenate([k_b[:, :keep], k_b[:, keep:] + k_f[:, :pair], k_f[:, pair:]], axis=1)
    for s in range(L):
        t_ref[0, s * pair:(s + 1) * pair, :] = by_lag[:, (L - 1 - s) * pair:(2 * L - 1 - s) * pair].astype(t_ref.dtype)


def _s5_operators(a_re, a_im, log_dt, b_re, b_im, c_re, c_im):
    n_g, n_p = a_re.shape[1], a_re.shape[2]
    gc = b_re.shape[-1]
    n_j, per = n_g // GROUPS_PER_STEP, GROUPS_PER_STEP
    states, pair, width = per * n_p, per * gc, SSM_CHUNK * per * gc
    lam = jnp.stack([a_re, a_im, jnp.broadcast_to(log_dt[..., None], a_re.shape)], axis=1)
    lam = jnp.transpose(lam.reshape(6, n_j, states), (1, 0, 2))
    lam = jnp.pad(lam, ((0, 0), (0, SUBLANES - lam.shape[1]), (0, 0)))
    eye = jnp.eye(per, dtype=F32)
    bt = jnp.stack([b_re, b_im], axis=1).reshape(2, 2, n_j, per, n_p, gc)
    bt = jnp.einsum('drjgpc,gh->jdrgchp', bt, eye).reshape(n_j, 2, 2, pair, states)
    ct = jnp.stack([c_re, c_im], axis=1).reshape(2, 2, n_j, per, gc, n_p)
    ct = jnp.einsum('drjgcp,gh->jdrgphc', ct, eye).reshape(n_j, 2, 2, states, pair)
    rep = jnp.tile(jnp.eye(pair, dtype=F32), (1, SSM_CHUNK))
    mat = jax.ShapeDtypeStruct((n_j, width, width), BF16)
    return pl.pallas_call(
        _s5_op_kernel,
        out_shape=(mat, mat, mat, jax.ShapeDtypeStruct((n_j, SUBLANES, states), F32)),
        grid=(n_j,),
        in_specs=[
            pl.BlockSpec((1, SUBLANES, states), lambda j: (j, 0, 0)),
            pl.BlockSpec((1, states, SUBLANES), lambda j: (j, 0, 0)),
            pl.BlockSpec((1, 2, 2, pair, states), lambda j: (j, 0, 0, 0, 0)),
            pl.BlockSpec((1, 2, 2, states, pair), lambda j: (j, 0, 0, 0, 0)),
            pl.BlockSpec((pair, width), lambda j: (0, 0)),
        ],
        out_specs=(pl.BlockSpec((1, width, width), lambda j: (j, 0, 0)),) * 3
        + (pl.BlockSpec((1, SUBLANES, states), lambda j: (j, 0, 0)),),
        compiler_params=_params("parallel"),
        name="s5_operators",
    )(lam, jnp.transpose(lam, (0, 2, 1)), bt, ct, rep)


def _ssm_out_kernel(y_ref, x_ref, mod_ref, g_ref, w_ref, o_ref, slab_scr, act_even, act_odd, out_scr, *, mi, gi):
    i = pl.program_id(0)

    @pl.when(i == 0)
    def _():
        act_odd[...] = jnp.zeros_like(act_odd)

    @pl.when(i % 2 == 0)
    def _():
        _ssm_out_step(y_ref, x_ref, mod_ref, g_ref, w_ref, o_ref, slab_scr, act_even, act_odd, out_scr, mi=mi, gi=gi)

    @pl.when(i % 2 == 1)
    def _():
        _ssm_out_step(y_ref, x_ref, mod_ref, g_ref, w_ref, o_ref, slab_scr, act_odd, act_even, out_scr, mi=mi, gi=gi)


def _ssm_out_step(y_ref, x_ref, mod_ref, g_ref, w_ref, o_ref, slab_scr, act_new, act_old, out_scr, *, mi, gi):
    n_slab, tm, lanes = slab_scr.shape
    pair = GROUPS_PER_STEP * SSM_GROUP_CH
    per_slab = lanes // pair
    for t in range(SSM_CHUNK):
        for k in range(n_slab):
            slab_scr[k, pl.ds(t, tm // SSM_CHUNK, stride=SSM_CHUNK), :] = jnp.concatenate(
                [y_ref[k * per_slab + jj, :, t * pair:(t + 1) * pair] for jj in range(per_slab)], axis=-1)
    for k in range(n_slab):
        act_new[:, k * lanes:(k + 1) * lanes] = _gelu_tanh(slab_scr[k]).astype(BF16)

    n_c, _, tn = out_scr.shape
    d = n_c * tn
    a = act_old[...]
    ssq = jnp.zeros((tm, 1), F32)
    for cc in range(n_c):
        val = jnp.dot(a, w_ref[:, cc * tn:(cc + 1) * tn], preferred_element_type=F32)
        gate = jnp.dot(a, w_ref[:, d + cc * tn:d + (cc + 1) * tn], preferred_element_type=F32)
        blk = val * _sigmoid(gate)
        out_scr[cc] = blk
        ssq = ssq + jnp.sum(blk * blk, axis=-1, keepdims=True)
    inv = lax.rsqrt(ssq / d + EPS)
    for cc in range(n_c):
        sl = slice(cc * tn, (cc + 1) * tn)
        o_ref[:, sl] = x_ref[:, sl] + mod_ref[0, mi:mi + 1, sl] * (out_scr[cc] * inv * g_ref[gi:gi + 1, sl])


def _ssm_out(yg, xs, mod, g, w_glu, *, mi, gi, n_rows, seg_of_row):
    d = xs.shape[1]
    tm, tn = GLU_TOKEN_TILE, GLU_TILE
    n_c = d // tn
    n_pairs, _, width = yg.shape
    n_tiles = n_rows // tm

    def prev(i):
        return jnp.maximum(i - 1, 0)

    return pl.pallas_call(
        functools.partial(_ssm_out_kernel, mi=mi, gi=gi),
        out_shape=jax.ShapeDtypeStruct((n_rows, d), F32),
        grid=(n_tiles + 1,),
        in_specs=[
            pl.BlockSpec((n_pairs, tm // SSM_CHUNK, width), lambda i: (0, jnp.minimum(i, n_tiles - 1), 0)),
            pl.BlockSpec((tm, d), lambda i: (prev(i), 0)),
            pl.BlockSpec((1, MOD_ROWS, d), lambda i: (seg_of_row(prev(i) * tm), 0, 0)),
            pl.BlockSpec((SUBLANES, d), lambda i: (0, 0)),
            pl.BlockSpec(w_glu.shape, lambda i: (0, 0), pipeline_mode=pl.Buffered(1)),
        ],
        out_specs=pl.BlockSpec((tm, d), lambda i: (prev(i), 0)),
        scratch_shapes=[pltpu.VMEM((d // LANES, tm, LANES), F32), pltpu.VMEM((tm, d), BF16),
                        pltpu.VMEM((tm, d), BF16), pltpu.VMEM((n_c, tm, tn), F32)],
        compiler_params=_params("arbitrary"),
        name="ssm_glu_out",
    )(yg, xs, mod, g, w_glu)


def _rope_tables(seq_len, extra_rows):
    rows = seq_len // GRID_W
    axis_dim = HEAD_DIM // 2
    inv_freq = ROPE_BASE ** (-jnp.arange(0, axis_dim, 2, dtype=F32) / axis_dim)
    ang_r = jnp.arange(rows, dtype=F32)[:, None] * inv_freq
    ang_c = jnp.arange(GRID_W, dtype=F32)[:, None] * inv_freq
    by_row = lambda t: jnp.repeat(t, GRID_W, axis=0)
    by_col = lambda t: jnp.tile(t, (rows, 1))
    cos = jnp.concatenate([by_row(jnp.cos(ang_r))] * 2 + [by_col(jnp.cos(ang_c))] * 2, axis=-1)
    sin = jnp.concatenate([by_row(-jnp.sin(ang_r)), by_row(jnp.sin(ang_r)),
                           by_col(-jnp.sin(ang_c)), by_col(jnp.sin(ang_c))], axis=-1)
    cos = jnp.concatenate([cos, jnp.ones((extra_rows, HEAD_DIM), F32)], axis=0)
    sin = jnp.concatenate([sin, jnp.zeros((extra_rows, HEAD_DIM), F32)], axis=0)
    return cos, sin


def kernel(x, c, ctx, c_ctx, ada_w, ada_b, norm_g, ffn_w_in, ffn_w_out, attn_w_in, attn_w_out, attn_sink,
           ssm_w_in, ssm_a_re, ssm_a_im, ssm_log_dt, ssm_b_re, ssm_b_im, ssm_c_re, ssm_c_im, ssm_d, ssm_w_glu):
    batch, seq_len, d = x.shape
    c_len = ctx.shape[1]
    depth = ada_w.shape[0]
    tm = TOKEN_TILE
    n_lat_rows, n_ctx_rows = batch * seq_len, batch * c_len
    assert seq_len % tm == 0 and n_ctx_rows % tm == 0 and n_lat_rows % c_len == 0
    assert seq_len % ATT_BLOCK == 0 and c_len % ATT_BLOCK == 0 and seq_len % GRID_W == 0
    tiles_per_batch = seq_len // tm
    lat_tiles = n_lat_rows // tm
    all_tiles = lat_tiles + n_ctx_rows // tm

    def seg_of(i):
        return jnp.minimum(i // tiles_per_batch, batch)

    def pos_of(i):
        return jnp.where(i < lat_tiles, i % tiles_per_batch, tiles_per_batch + (i - lat_tiles))

    def seg_of_row(r):
        return jnp.minimum(r // seq_len, batch)

    xs, tail = x.reshape(n_lat_rows, d), ctx.reshape(n_ctx_rows, d)
    w_ff_in, w_ff_out = _ffn_w_in_chunks(ffn_w_in), ffn_w_out.astype(BF16)

    n_cond = batch + 1
    cvec = jnp.concatenate([c, c_ctx[None], jnp.zeros((SUBLANES - n_cond, d), F32)], axis=0)
    mods = _modulation(cvec.T, ada_w, ada_b, n_cond)
    mods = mods[:, :n_cond].reshape(depth, n_cond, N_MOD, d)
    mods = jnp.pad(mods, ((0, 0), (0, 0), (0, MOD_ROWS - N_MOD), (0, 0)))
    gains = jnp.pad(norm_g, ((0, 0), (0, SUBLANES - norm_g.shape[1]), (0, 0)))

    cos, sin = _rope_tables(seq_len, n_ctx_rows)

    for i in range(depth):
        need_ctx = i < depth - 1
        mod, g = mods[i], gains[i]
        j = i // 2
        xs = _ffn(xs, mod, g, w_ff_in, w_ff_out, (i, 0), mi=0, gi=0, n_tiles=all_tiles, seg_of=seg_of, tail=tail)
        tail = None
        post_tiles = all_tiles if need_ctx else lat_tiles
        if i % 2 == 0:
            q, k, v = _qkv(xs, mod, g, attn_w_in[j].astype(BF16), cos, sin,
                           mi=3, gi=2, n_tiles=all_tiles, seg_of=seg_of, pos_of=pos_of)
            o = _attention(q, k, v, attn_sink[j], batch=batch, seq_len=seq_len, c_len=c_len)
            xs = _proj_out(o, xs, mod, g, attn_w_out[j].astype(BF16),
                           mi=5, gi=3, n_tiles=post_tiles, seg_of=seg_of)
        else:
            if need_ctx:
                raise NotImplementedError("context output of the S5 mixer is only needed when another layer follows")
            ug = _proj_in(xs, mod, g, ssm_w_in[j].astype(BF16), mi=3, gi=2, n_tiles=all_tiles, seg_of=seg_of)
            L = SSM_CHUNK
            pair = GROUPS_PER_STEP * SSM_GROUP_CH
            toep, ws, wc, al = _s5_operators(ssm_a_re[j], ssm_a_im[j], ssm_log_dt[j], ssm_b_re[j], ssm_b_im[j],
                                             ssm_c_re[j], ssm_c_im[j])
            d_lanes = jnp.tile(ssm_d[j].reshape(d // pair, 1, pair), (1, 1, L))
            yg = _s5_core(ug, toep, ws, wc, al, d_lanes, batch=batch, n_lat=seq_len // L, n_ctx=c_len // L)
            xs = _ssm_out(yg, xs, mod, g, ssm_w_glu[j].astype(BF16),
                          mi=5, gi=3, n_rows=post_tiles * tm, seg_of_row=seg_of_row)
        xs = _ffn(xs, mod, g, w_ff_in, w_ff_out, (i, 1), mi=6, gi=4, n_tiles=post_tiles, seg_of=seg_of)
    return xs[:n_lat_rows].reshape(batch, seq_len, d)
```
